```python
import math
import jax, jax.numpy as jnp
from jax import lax
import numpy as np

D_MODEL = 1024
BATCH = 8
SEQ = 8192
DEPTH = 2

HEAD_DIM = 64
ROT_DIM = HEAD_DIM // 4
ROPE_THETA = 500000.0
MAX_POS_OFFSET = 4096

DN_HEADS = 4
DN_HEAD_DIM = 128
DN_CHUNK = 64
DN_CONV = 5
DN_WIDTH = DN_HEADS * DN_HEAD_DIM

DIL_PAIRS = ((128, 1), (512, 4), (2048, 16))
DIL_HEADS_PER_GROUP = 4
DIL_HEADS = DIL_HEADS_PER_GROUP * len(DIL_PAIRS)
DIL_WIDTH = DIL_HEADS * HEAD_DIM

SWA_Q_HEADS = 16
SWA_KV_HEADS = 4
SWA_WINDOW = 128

N_EXPERTS = 16
EC_FACTOR = 2
EXPERT_FF = 1024

AB_IN = 4 * DN_WIDTH + 4 * DN_HEADS + 3 * DIL_WIDTH
AB_OUT = DN_WIDTH + DIL_HEADS_PER_GROUP * HEAD_DIM
C_OUT = SWA_Q_HEADS * HEAD_DIM
C_IN = C_OUT + 2 * SWA_KV_HEADS * HEAD_DIM
N_EVEN = (DEPTH + 1) // 2
N_ODD = DEPTH // 2

ALPHA = (2.0 * DEPTH) ** 0.25
BETA = (8.0 * DEPTH) ** -0.25
LN_EPS = 1e-5
NORM_EPS = 1e-6
NEG = -1e30

kernel_name = 'hybrid_deltanet_dilated_swa_ec_moe'


def layer_norm(x, g, b):
    xf = x.astype(jnp.float32)
    mu = xf.mean(-1, keepdims=True)
    var = jnp.square(xf - mu).mean(-1, keepdims=True)
    return ((xf - mu) * lax.rsqrt(var + LN_EPS) * g + b).astype(x.dtype)


def l2_normalize(x):
    xf = x.astype(jnp.float32)
    return xf * lax.rsqrt(jnp.sum(xf * xf, -1, keepdims=True) + NORM_EPS)


def rope_tables(positions):
    inv_freq = jnp.power(ROPE_THETA, -jnp.arange(0, ROT_DIM, 2, dtype=jnp.float32) / ROT_DIM)
    ang = positions.astype(jnp.float32)[..., None] * inv_freq
    return jnp.cos(ang)[:, :, None, :], jnp.sin(ang)[:, :, None, :]


def partial_rope(x, cos, sin):
    half = ROT_DIM // 2
    x1, x2, rest = x[..., :half], x[..., half:ROT_DIM], x[..., ROT_DIM:]
    cos, sin = cos.astype(x.dtype), sin.astype(x.dtype)
    return jnp.concatenate([x1 * cos - x2 * sin, x2 * cos + x1 * sin, rest], axis=-1)


def centred_depthwise_conv(x, w):
    pad = (w.shape[0] - 1) // 2
    return lax.conv_general_dilated(x, w[:, None, :].astype(x.dtype), window_strides=(1,),
                                    padding=[(pad, pad)], dimension_numbers=('NWC', 'WIO', 'NWC'),
                                    feature_group_count=x.shape[-1])


def gated_delta_rule(q, k, v, g, beta):
    Bx, S, H, dk = q.shape
    dv = v.shape[-1]
    C = DN_CHUNK
    N = S // C

    def chunks(a):
        return jnp.moveaxis(a.reshape(Bx, N, C, H, *a.shape[3:]), 3, 2)

    qc, kc, vc = chunks(q), chunks(k), chunks(v)
    gc = jnp.cumsum(chunks(g), axis=-1)
    bc = chunks(beta)
    causal = jnp.tril(jnp.ones((C, C), dtype=bool))
    strict = jnp.tril(jnp.ones((C, C), dtype=bool), -1)
    diff = gc[..., :, None] - gc[..., None, :]
    decay = jnp.where(causal, jnp.exp(jnp.where(causal, diff, 0.0)), 0.0)
    kb = kc * bc[..., None]
    lower = jnp.where(strict, jnp.einsum('bnhik,bnhjk->bnhij', kb, kc) * decay, 0.0)
    eye = jnp.eye(C, dtype=q.dtype)
    rhs = jnp.concatenate([vc * bc[..., None], kb * jnp.exp(gc)[..., None]], axis=-1)
    sol = lax.linalg.triangular_solve(eye + lower, rhs, left_side=True, lower=True, unit_diagonal=True)
    u, w = sol[..., :dv], sol[..., dv:]
    intra = jnp.einsum('bnhik,bnhjk->bnhij', qc, kc) * decay
    q_dec = qc * jnp.exp(gc)[..., None]
    g_last = gc[..., -1]
    k_dec = kc * jnp.exp(g_last[..., None] - gc)[..., None]

    def step(state, xs):
        q_i, k_i, u_i, w_i, a_i, gl_i = xs
        v_new = u_i - jnp.einsum('bhck,bhkv->bhcv', w_i, state)
        o_i = jnp.einsum('bhck,bhkv->bhcv', q_i, state) + jnp.einsum('bhij,bhjv->bhiv', a_i, v_new)
        state = state * jnp.exp(gl_i)[..., None, None] + jnp.einsum('bhck,bhcv->bhkv', k_i, v_new)
        return state, o_i

    xs = tuple(jnp.moveaxis(a, 1, 0) for a in (q_dec, k_dec, u, w, intra, g_last))
    state0 = jnp.zeros((Bx, H, dk, dv), q.dtype)
    _, o = lax.scan(step, state0, xs)
    return jnp.moveaxis(o, 0, 1).swapaxes(2, 3).reshape(Bx, S, H, dv)


def banded_attention(q, k, v, key_valid, half_window, sink=None):
    Bx, T, Hq, dh = q.shape
    Hkv = k.shape[2]
    G = Hq // Hkv
    blk = half_window
    nb = T // blk

    def neighbours(a):
        ab = a.reshape(Bx, nb, blk, *a.shape[2:])
        ap = jnp.pad(ab, [(0, 0), (1, 1)] + [(0, 0)] * (ab.ndim - 2))
        return jnp.concatenate([ap[:, :-2], ap[:, 1:-1], ap[:, 2:]], axis=2)

    kw, vw, valid_w = neighbours(k), neighbours(v), neighbours(key_valid)
    qb = q.reshape(Bx, nb, blk, Hkv, G, dh)
    s = jnp.einsum('bnqhgd,bnkhd->bnhgqk', qb, kw, preferred_element_type=jnp.float32) * (dh ** -0.5)
    rel = jnp.arange(3 * blk)[None, :] - blk - jnp.arange(blk)[:, None]
    mask = (jnp.abs(rel) <= half_window)[None, None, None, None] & valid_w[:, :, None, None, None, :]
    s = jnp.where(mask, s, NEG)
    m = jnp.max(s, axis=-1, keepdims=True)
    if sink is not None:
        sk = sink.astype(jnp.float32).reshape(Hkv, G)[None, None, :, :, None, None]
        m = jnp.maximum(m, sk)
    p = jnp.exp(s - m)
    denom = jnp.sum(p, axis=-1, keepdims=True)
    if sink is not None:
        denom = denom + jnp.exp(sk - m)
    o = jnp.einsum('bnhgqk,bnkhd->bnqhgd', (p / denom).astype(v.dtype), vw)
    lse = (m + jnp.log(denom))[..., 0]
    lse = jnp.transpose(lse, (0, 1, 4, 2, 3)).reshape(Bx, T, Hq)
    return o.reshape(Bx, T, Hq, dh), lse


def dilated_window_attention(q, k, v, window, dil):
    Bx, S, H, dh = q.shape
    steps = window // (2 * dil)
    T = S // dil
    Tp = -(-T // steps) * steps

    def to_strided(a):
        a = jnp.moveaxis(a.reshape(Bx, T, dil, *a.shape[2:]), 2, 1).reshape(Bx * dil, T, *a.shape[2:])
        return jnp.pad(a, [(0, 0), (0, Tp - T)] + [(0, 0)] * (a.ndim - 2))

    def from_strided(a):
        a = a[:, :T].reshape(Bx, dil, T, *a.shape[2:])
        return jnp.moveaxis(a, 1, 2).reshape(Bx, S, *a.shape[3:])

    valid = jnp.broadcast_to(jnp.arange(Tp) < T, (Bx * dil, Tp))
    o, lse = banded_attention(to_strided(q), to_strided(k), to_strided(v), valid, steps)
    return from_strided(o), from_strided(lse)


def mixer_deltanet_dilated(h, cos, sin, w_in, conv_w, a_log, dt_bias, dn_norm, w_out):
    Bx, S, _ = h.shape
    proj = h @ w_in
    qkv_a, z, gates, qkv_b = jnp.split(proj, [3 * DN_WIDTH, 4 * DN_WIDTH, 4 * DN_WIDTH + 4 * DN_HEADS], axis=-1)
    qkv_a = jax.nn.silu(centred_depthwise_conv(qkv_a, conv_w))
    qa, ka, va = (a.reshape(Bx, S, DN_HEADS, DN_HEAD_DIM) for a in jnp.split(qkv_a, 3, axis=-1))
    qa = l2_normalize(qa) * (DN_HEAD_DIM ** -0.5)
    ka = l2_normalize(ka)
    va = va.astype(jnp.float32)
    gates = gates.astype(jnp.float32).reshape(Bx, S, 4, DN_HEADS)
    decay = -jnp.exp(a_log.astype(jnp.float32)) * jax.nn.softplus(gates[:, :, :2] + dt_bias)
    beta = jax.nn.sigmoid(gates[:, :, 2:])
    flip = lambda a: jnp.flip(a, axis=1)
    o_fwd = gated_delta_rule(qa, ka, va, decay[:, :, 0], beta[:, :, 0])
    o_bwd = flip(gated_delta_rule(flip(qa), flip(ka), flip(va), flip(decay[:, :, 1]), flip(beta[:, :, 1])))
    o_dn = o_fwd + o_bwd
    o_dn = o_dn * lax.rsqrt(jnp.mean(o_dn * o_dn, -1, keepdims=True) + NORM_EPS) * dn_norm
    o_dn = o_dn * jax.nn.silu(z.astype(jnp.float32)).reshape(Bx, S, DN_HEADS, DN_HEAD_DIM)
    o_dn = o_dn.astype(h.dtype).reshape(Bx, S, DN_WIDTH)
    qb, kb, vb = (a.reshape(Bx, S, DIL_HEADS, HEAD_DIM) for a in jnp.split(qkv_b, 3, axis=-1))
    qb, kb = partial_rope(qb, cos, sin), partial_rope(kb, cos, sin)
    outs, lses = [], []
    for gi, (window, dil) in enumerate(DIL_PAIRS):
        hs = slice(gi * DIL_HEADS_PER_GROUP, (gi + 1) * DIL_HEADS_PER_GROUP)
        o_g, lse_g = dilated_window_attention(qb[:, :, hs], kb[:, :, hs], vb[:, :, hs], window, dil)
        outs.append(o_g)
        lses.append(lse_g)
    wts = jax.nn.softmax(jnp.stack(lses), axis=0)
    o_dil = jnp.einsum('gbsh,gbshd->bshd', wts.astype(h.dtype), jnp.stack(outs)).reshape(Bx, S, -1)
    return jnp.concatenate([o_dn, o_dil], axis=-1) @ w_out


def mixer_window_gqa_sink(h, cos, sin, w_in, sinks, w_out):
    Bx, S, _ = h.shape
    q, k, v = jnp.split(h @ w_in, [C_OUT, C_OUT + SWA_KV_HEADS * HEAD_DIM], axis=-1)
    q = partial_rope(q.reshape(Bx, S, SWA_Q_HEADS, HEAD_DIM), cos, sin)
    k = partial_rope(k.reshape(Bx, S, SWA_KV_HEADS, HEAD_DIM), cos, sin)
    v = v.reshape(Bx, S, SWA_KV_HEADS, HEAD_DIM)
    o, _ = banded_attention(q, k, v, jnp.ones((Bx, S), dtype=bool), SWA_WINDOW, sinks)
    return o.reshape(Bx, S, C_OUT) @ w_out


def expert_choice_ffn(h, router_w, w_gate, w_up, w_down):
    Bx, S, _ = h.shape
    cap = (EC_FACTOR * S) // N_EXPERTS
    aff = jax.nn.softmax(jnp.einsum('bsd,de->bse', h, router_w, preferred_element_type=jnp.float32), axis=-1)
    gate, idx = lax.top_k(jnp.swapaxes(aff, 1, 2), cap)
    bidx = jnp.arange(Bx)[:, None, None]
    xg = h[bidx, idx]
    hid = jax.nn.silu(jnp.einsum('becd,edf->becf', xg, w_gate)) * jnp.einsum('becd,edf->becf', xg, w_up)
    out = jnp.einsum('becf,efd->becd', hid, w_down) * gate[..., None].astype(h.dtype)
    return jnp.zeros_like(h).at[bidx, idx].add(out)


def setup_inputs(seed: int = 0) -> dict:
    key = jax.random.key(seed)
    ks = iter(jax.random.split(key, 24))

    def normal(shape, scale):
        return jax.random.normal(next(ks), shape, jnp.float32) * scale

    x = normal((BATCH, SEQ, D_MODEL), 1.0)
    c = normal((BATCH, D_MODEL), 1.0)
    offset = jax.random.randint(next(ks), (BATCH, 1), 0, MAX_POS_OFFSET, dtype=jnp.int32)
    positions = offset + jnp.arange(SEQ, dtype=jnp.int32)[None, :]
    ada_w = normal((DEPTH, D_MODEL, 6 * D_MODEL), 0.1 * D_MODEL ** -0.5)
    ada_b = normal((DEPTH, 6 * D_MODEL), 0.01)
    ab_w_in = normal((N_EVEN, D_MODEL, AB_IN), D_MODEL ** -0.5)
    ab_conv_w = normal((N_EVEN, DN_CONV, 3 * DN_WIDTH), DN_CONV ** -0.5)
    ab_a_log = jnp.log(jax.random.uniform(next(ks), (N_EVEN, 2, DN_HEADS), jnp.float32, 1.0, 16.0))
    dt = jnp.exp(jax.random.uniform(next(ks), (N_EVEN, 2, DN_HEADS), jnp.float32, math.log(1e-3), math.log(1e-1)))
    ab_dt_bias = dt + jnp.log(-jnp.expm1(-dt))
    ab_dn_norm = 1.0 + normal((N_EVEN, DN_HEAD_DIM), 0.01)
    ab_w_out = normal((N_EVEN, AB_OUT, D_MODEL), BETA * AB_OUT ** -0.5)
    swa_w_in = normal((N_ODD, D_MODEL, C_IN), D_MODEL ** -0.5)
    swa_sinks = normal((N_ODD, SWA_Q_HEADS), 0.5)
    swa_w_out = normal((N_ODD, C_OUT, D_MODEL), BETA * C_OUT ** -0.5)
    ln_mix_g = 1.0 + normal((DEPTH, D_MODEL), 0.01)
    ln_mix_b = normal((DEPTH, D_MODEL), 0.01)
    router_w = normal((DEPTH, D_MODEL, N_EXPERTS), D_MODEL ** -0.5)
    moe_w_gate = normal((DEPTH, N_EXPERTS, D_MODEL, EXPERT_FF), D_MODEL ** -0.5)
    moe_w_up = normal((DEPTH, N_EXPERTS, D_MODEL, EXPERT_FF), D_MODEL ** -0.5)
    moe_w_down = normal((DEPTH, N_EXPERTS, EXPERT_FF, D_MODEL), BETA * EXPERT_FF ** -0.5)
    ln_ffn_g = 1.0 + normal((DEPTH, D_MODEL), 0.01)
    ln_ffn_b = normal((DEPTH, D_MODEL), 0.01)
    return {'x': x, 'c': c, 'positions': positions, 'ada_w': ada_w, 'ada_b': ada_b,
            'ab_w_in': ab_w_in, 'ab_conv_w': ab_conv_w, 'ab_a_log': ab_a_log, 'ab_dt_bias': ab_dt_bias,
            'ab_dn_norm': ab_dn_norm, 'ab_w_out': ab_w_out, 'swa_w_in': swa_w_in, 'swa_sinks': swa_sinks,
            'swa_w_out': swa_w_out, 'ln_mix_g': ln_mix_g, 'ln_mix_b': ln_mix_b, 'router_w': router_w,
            'moe_w_gate': moe_w_gate, 'moe_w_up': moe_w_up, 'moe_w_down': moe_w_down,
            'ln_ffn_g': ln_ffn_g, 'ln_ffn_b': ln_ffn_b}


def reference(x, c, positions, ada_w, ada_b, ab_w_in, ab_conv_w, ab_a_log, ab_dt_bias, ab_dn_norm,
              ab_w_out, swa_w_in, swa_sinks, swa_w_out, ln_mix_g, ln_mix_b, router_w, moe_w_gate,
              moe_w_up, moe_w_down, ln_ffn_g, ln_ffn_b):
    cos, sin = rope_tables(positions)
    cond = jax.nn.silu(c)
    for i in range(DEPTH):
        mod = cond @ ada_w[i] + ada_b[i]
        sh1, sc1, g1, sh2, sc2, g2 = jnp.split(mod[:, None, :], 6, axis=-1)
        j = i // 2
        h = x * (1 + sc1) + sh1
        if i % 2 == 0:
            y = mixer_deltanet_dilated(h, cos, sin, ab_w_in[j], ab_conv_w[j], ab_a_log[j], ab_dt_bias[j],
                                       ab_dn_norm[j], ab_w_out[j])
        else:
            y = mixer_window_gqa_sink(h, cos, sin, swa_w_in[j], swa_sinks[j], swa_w_out[j])
        x = layer_norm(ALPHA * x + (1 + g1) * y, ln_mix_g[i], ln_mix_b[i])
        h = x * (1 + sc2) + sh2
        y = expert_choice_ffn(h, router_w[i], moe_w_gate[i], moe_w_up[i], moe_w_down[i])
        x = layer_norm(ALPHA * x + (1 + g2) * y, ln_ffn_g[i], ln_ffn_b[i])
    return x
```

```python
import functools
import math

import jax
import jax.numpy as jnp
from jax import lax
from jax.experimental import pallas as pl
from jax.experimental.pallas import tpu as pltpu

F32 = jnp.float32
BF16 = jnp.bfloat16
I32 = jnp.int32
HI = lax.Precision.HIGHEST

DEPTH = 2
HEAD_DIM = 64
ROT_DIM = HEAD_DIM // 4
ROPE_THETA = 500000.0
DN_HEADS = 4
DN_HEAD_DIM = 128
DN_CHUNK = 64
DN_CONV = 5
DN_WIDTH = DN_HEADS * DN_HEAD_DIM
DIL_PAIRS = ((128, 1), (512, 4), (2048, 16))
DIL_HEADS_PER_GROUP = 4
DIL_GROUP_WIDTH = DIL_HEADS_PER_GROUP * HEAD_DIM
DIL_WIDTH = DIL_GROUP_WIDTH * len(DIL_PAIRS)
SWA_Q_HEADS = 16
SWA_KV_HEADS = 4
SWA_WINDOW = 128
N_EXPERTS = 16
EC_FACTOR = 2
ALPHA = (2.0 * DEPTH) ** 0.25
LN_EPS = 1e-5
NORM_EPS = 1e-6
NEG = -1e30
LANES = 128
HALO = 8
VMEM_LIMIT = 56 * 1024 * 1024


def _dot(a, b, prec=None):
    return jnp.dot(a, b, preferred_element_type=F32, precision=prec)


def _dot_nt(a, b, prec=None):
    return lax.dot_general(a, b, (((1,), (1,)), ((), ())), preferred_element_type=F32, precision=prec)


def _dot_tn(a, b, prec=None):
    return lax.dot_general(a, b, (((0,), (0,)), ((), ())), preferred_element_type=F32, precision=prec)


def _silu(x):
    return x * jax.nn.sigmoid(x)


def _params(sem):
    return pltpu.CompilerParams(dimension_semantics=sem, vmem_limit_bytes=VMEM_LIMIT)


def _mod_kernel(c_ref, w_ref, b_ref, o_ref):
    o_ref[...] = _dot(_silu(c_ref[...]), w_ref[...], HI) + b_ref[...]


def _mod_call(c, ada_w, ada_b):
    depth, d, n6 = ada_w.shape
    bsz = c.shape[0]
    tn = n6 // 4
    return pl.pallas_call(
        _mod_kernel,
        grid=(depth, n6 // tn),
        in_specs=[pl.BlockSpec((bsz, d), lambda i, j: (0, 0)),
                  pl.BlockSpec((None, d, tn), lambda i, j: (i, 0, j)),
                  pl.BlockSpec((None, 1, tn), lambda i, j: (i, 0, j))],
        out_specs=pl.BlockSpec((None, bsz, tn), lambda i, j: (i, 0, j)),
        out_shape=jax.ShapeDtypeStruct((depth, bsz, n6), F32),
        compiler_params=_params(("arbitrary", "arbitrary")),
        name="adaln_mod",
    )(c, ada_w, ada_b.reshape(depth, 1, n6))


def _inproj_kernel(x_ref, mod_ref, w_ref, rope_ref, *out_refs, plan):
    m = mod_ref[...]
    h = (x_ref[...] * (1.0 + m[1:2]) + m[0:1]).astype(BF16)
    for c0, width, oi, o0, rope in plan:
        acc = _dot(h, w_ref[:, c0:c0 + width])
        if rope:
            reps = width // LANES
            cosv = jnp.concatenate([rope_ref[:, 0:LANES]] * reps, axis=1)
            sa = jnp.concatenate([rope_ref[:, LANES:2 * LANES]] * reps, axis=1)
            sb = jnp.concatenate([rope_ref[:, 2 * LANES:3 * LANES]] * reps, axis=1)
            half = ROT_DIM // 2
            acc = acc * cosv + pltpu.roll(acc, width - half, 1) * sa + pltpu.roll(acc, half, 1) * sb
        out_refs[oi][:, o0:o0 + width] = acc.astype(out_refs[oi].dtype)


def _inproj_call(x, mod, w, rope, plan, out_widths, out_dtypes, tm, name):
    bsz, s, d = x.shape
    n = w.shape[1]
    out_shape = [jax.ShapeDtypeStruct((bsz, s, ow), od) for ow, od in zip(out_widths, out_dtypes)]
    out_specs = [pl.BlockSpec((None, tm, ow), lambda b, i: (b, i, 0)) for ow in out_widths]
    return pl.pallas_call(
        functools.partial(_inproj_kernel, plan=plan),
        grid=(bsz, s // tm),
        in_specs=[pl.BlockSpec((None, tm, d), lambda b, i: (b, i, 0)),
                  pl.BlockSpec((None, 6, d), lambda b, i: (b, 0, 0)),
                  pl.BlockSpec((d, n), lambda b, i: (0, 0)),
                  pl.BlockSpec((None, tm, 3 * LANES), lambda b, i: (b, i, 0))],
        out_specs=out_specs,
        out_shape=out_shape,
        compiler_params=_params(("parallel", "parallel")),
        name=name,
    )(x, mod, w, rope)


def _chunk_plan(groups, chunk=256):
    plan = []
    for c0, width, oi, rope in groups:
        off = 0
        while off < width:
            wd = min(chunk, width - off)
            plan.append((c0 + off, wd, oi, off, rope))
            off += wd
    return tuple(plan)


def _rope_tables(positions):
    inv_freq = jnp.power(ROPE_THETA, -jnp.arange(0, ROT_DIM, 2, dtype=F32) / ROT_DIM)
    ang = positions.astype(F32)[..., None] * inv_freq
    cos, sin = jnp.cos(ang), jnp.sin(ang)
    half = ROT_DIM // 2
    rest = HEAD_DIM - ROT_DIM
    ones = jnp.ones(cos.shape[:-1] + (rest,), F32)
    z_half = jnp.zeros(cos.shape[:-1] + (half,), F32)
    z_rest = jnp.zeros(cos.shape[:-1] + (rest,), F32)
    c64 = jnp.concatenate([cos, cos, ones], -1)
    sa64 = jnp.concatenate([-sin, z_half, z_rest], -1)
    sb64 = jnp.concatenate([z_half, sin, z_rest], -1)
    return jnp.concatenate([c64, c64, sa64, sa64, sb64, sb64], -1)


def _dn_prep_kernel(xa_ref, top_ref, bot_ref, gt_ref, cw_ref, alog_ref, dtb_ref,
                    u_ref, w_ref, qd_ref, kd_ref, at_ref, egl_ref,
                    q_s, k_s, v_s, gc_s, *, t):
    ch = DN_CHUNK
    dk = DN_HEAD_DIM
    pad = (DN_CONV - 1) // 2
    for grp, dst in enumerate((q_s, k_s, v_s)):
        cols = slice(grp * DN_WIDTH, (grp + 1) * DN_WIDTH)
        xe = jnp.concatenate([top_ref[:, cols], xa_ref[:, cols], bot_ref[:, cols]], axis=0)
        y = jnp.zeros((t, DN_WIDTH), F32)
        for k in range(DN_CONV):
            y = y + xe[HALO - pad + k:HALO - pad + k + t, :] * cw_ref[k:k + 1, cols]
        y = _silu(y)
        if grp < 2:
            scale = dk ** -0.5 if grp == 0 else 1.0
            parts = []
            for h in range(DN_HEADS):
                yh = y[:, h * dk:(h + 1) * dk]
                parts.append(yh * lax.rsqrt(jnp.sum(yh * yh, -1, keepdims=True) + NORM_EPS) * scale)
            y = jnp.concatenate(parts, axis=1)
        dst[...] = y

    g = gt_ref[...]
    lane = lax.broadcasted_iota(I32, (t, LANES), 1)
    z = g + dtb_ref[...]
    softplus = jnp.maximum(z, 0.0) + jnp.log1p(jnp.exp(-jnp.abs(z)))
    dec = -jnp.exp(alog_ref[...]) * softplus
    gv = jnp.where(lane < 2 * DN_HEADS, dec, jnp.where(lane < 4 * DN_HEADS, jax.nn.sigmoid(g), 0.0))
    ri = lax.broadcasted_iota(I32, (t, t), 0)
    ci = lax.broadcasted_iota(I32, (t, t), 1)
    shift = int(math.log2(ch))
    same = (ri >> shift) == (ci >> shift)
    pre = jnp.where(same & (ci <= ri), 1.0, 0.0).astype(F32)
    suf = jnp.where(same & (ci >= ri), 1.0, 0.0).astype(F32)
    gcf = _dot(pre, gv, HI)
    gcb = _dot(suf, gv, HI)
    gc_s[...] = jnp.where(lane < DN_HEADS, gcf, jnp.where(lane < 2 * DN_HEADS, gcb, gv))

    r64 = lax.broadcasted_iota(I32, (ch, ch), 0)
    c64 = lax.broadcasted_iota(I32, (ch, ch), 1)
    eye = jnp.where(r64 == c64, 1.0, 0.0).astype(F32)
    lane64 = lax.broadcasted_iota(I32, (ch, LANES), 1)

    def chunk_body(c, carry):
        r0 = pl.multiple_of(c * ch, ch)
        rows = pl.ds(r0, ch)
        gcc = gc_s[rows, :]
        for d in range(2):
            incl = (c64 <= r64) if d == 0 else (c64 >= r64)
            strict = (c64 < r64) if d == 0 else (c64 > r64)
            last = ch - 1 if d == 0 else 0
            egl_rows = []
            for h in range(DN_HEADS):
                col = d * DN_HEADS + h
                bcol = 2 * DN_HEADS + col
                gcl = jnp.broadcast_to(gcc[:, col:col + 1], (ch, LANES))
                bb = jnp.broadcast_to(gcc[:, bcol:bcol + 1], (ch, LANES))
                hs = slice(h * dk, (h + 1) * dk)
                q = q_s[rows, hs]
                k = k_s[rows, hs]
                v = v_s[rows, hs]
                xm = jnp.where(lane64 == 0, gcl, jnp.where(lane64 == 1, 1.0, 0.0))
                ym = jnp.where(lane64 == 0, 1.0, jnp.where(lane64 == 1, -gcl, 0.0))
                diff = _dot_nt(xm, ym, HI)
                decay = jnp.where(incl, jnp.exp(jnp.where(incl, diff, 0.0)), 0.0)
                kbeta = k * bb
                kb16 = k.astype(BF16)
                lower = jnp.where(strict, _dot_nt(kbeta.astype(BF16), kb16) * decay, 0.0)
                intra = _dot_nt(q.astype(BF16), kb16) * decay
                pw = -lower
                ainv = eye + pw
                for _ in range(int(math.log2(ch)) - 1):
                    pw = _dot(pw, pw, HI)
                    ainv = ainv + _dot(ainv, pw, HI)
                eg = jnp.exp(gcl)
                uu = _dot(ainv, v * bb, HI)
                ww = _dot(ainv, kbeta * eg, HI)
                gl = jnp.broadcast_to(gcl[last:last + 1, :], (ch, LANES))
                u_ref[d, rows, hs] = uu
                w_ref[d, rows, hs] = ww.astype(BF16)
                qd_ref[d, rows, hs] = (q * eg).astype(BF16)
                kd_ref[d, rows, hs] = (k * jnp.exp(gl - gcl)).astype(BF16)
                at_ref[d, rows, h * ch:(h + 1) * ch] = intra.astype(BF16)
                egl_rows.append(jnp.exp(gl[0:1, :]))
            egl_rows.append(jnp.zeros((8 - DN_HEADS, LANES), F32))
            egl_ref[d, c] = jnp.concatenate(egl_rows, axis=0)
        return carry

    lax.fori_loop(0, t // ch, chunk_body, 0)


def _dn_prep_call(qkv_a, gates, conv_w, a_log, dt_bias, t):
    bsz, s, cw = qkv_a.shape
    nt = s // t
    r = qkv_a.reshape(bsz, nt, t, cw)
    zero = jnp.zeros((bsz, 1, HALO, cw), F32)
    top = jnp.concatenate([zero, r[:, :-1, t - HALO:]], axis=1)
    bot = jnp.concatenate([r[:, 1:, :HALO], zero], axis=1)
    cwp = jnp.zeros((8, cw), F32).at[:DN_CONV].set(conv_w)
    alog = jnp.zeros((1, LANES), F32).at[0, :2 * DN_HEADS].set(a_log.reshape(-1))
    dtb = jnp.zeros((1, LANES), F32).at[0, :2 * DN_HEADS].set(dt_bias.reshape(-1))
    nch = s // DN_CHUNK
    wide = lambda dt, wd: jax.ShapeDtypeStruct((2, bsz, s, wd), dt)
    spec = lambda wd: pl.BlockSpec((2, None, t, wd), lambda b, i: (0, b, i, 0))
    return pl.pallas_call(
        functools.partial(_dn_prep_kernel, t=t),
        grid=(bsz, nt),
        in_specs=[pl.BlockSpec((None, t, cw), lambda b, i: (b, i, 0)),
                  pl.BlockSpec((None, None, HALO, cw), lambda b, i: (b, i, 0, 0)),
                  pl.BlockSpec((None, None, HALO, cw), lambda b, i: (b, i, 0, 0)),
                  pl.BlockSpec((None, t, LANES), lambda b, i: (b, i, 0)),
                  pl.BlockSpec((8, cw), lambda b, i: (0, 0)),
                  pl.BlockSpec((1, LANES), lambda b, i: (0, 0)),
                  pl.BlockSpec((1, LANES), lambda b, i: (0, 0))],
        out_specs=[spec(DN_WIDTH), spec(DN_WIDTH), spec(DN_WIDTH), spec(DN_WIDTH), spec(DN_HEADS * DN_CHUNK),
                   pl.BlockSpec((2, None, t // DN_CHUNK, 8, LANES), lambda b, i: (0, b, i, 0, 0))],
        out_shape=[wide(F32, DN_WIDTH), wide(BF16, DN_WIDTH), wide(BF16, DN_WIDTH), wide(BF16, DN_WIDTH),
                   wide(BF16, DN_HEADS * DN_CHUNK),
                   jax.ShapeDtypeStruct((2, bsz, nch, 8, LANES), F32)],
        scratch_shapes=[pltpu.VMEM((t, DN_WIDTH), F32), pltpu.VMEM((t, DN_WIDTH), F32),
                        pltpu.VMEM((t, DN_WIDTH), F32), pltpu.VMEM((t, LANES), F32)],
        compiler_params=_params(("parallel", "parallel")),
        name="deltanet_prep",
    )(qkv_a, top, bot, gates, cwp, alog, dtb)


def _dn_scan_kernel(u_ref, w_ref, qd_ref, kd_ref, at_ref, egl_ref, o_ref, st_ref, *, nc):
    ch = DN_CHUNK
    dk = DN_HEAD_DIM
    d = pl.program_id(0)

    @pl.when(pl.program_id(2) == 0)
    def _():
        st_ref[...] = jnp.zeros(st_ref.shape, st_ref.dtype)

    for j in range(nc):
        cc = j + d * (nc - 1 - 2 * j)
        rows = pl.ds(pl.multiple_of(cc * ch, ch), ch)
        egl = egl_ref[cc]
        for h in range(DN_HEADS):
            hs = slice(h * dk, (h + 1) * dk)
            st = st_ref[h]
            sb = st.astype(BF16)
            vnew = u_ref[rows, hs] - _dot(w_ref[rows, hs], sb)
            vb = vnew.astype(BF16)
            o_ref[rows, hs] = _dot(qd_ref[rows, hs], sb) + _dot(at_ref[rows, h * ch:(h + 1) * ch], vb)
            st_ref[h] = st * egl[h:h + 1, :] + _dot_tn(kd_ref[rows, hs], vb)


def _dn_scan_call(u, w, qd, kd, at, egl, tc):
    _, bsz, s, wd = u.shape
    nb = s // tc
    nc = tc // DN_CHUNK

    def blk(d, n):
        return n + d * (nb - 1 - 2 * n)

    spec = lambda width: pl.BlockSpec((None, None, tc, width), lambda d, b, n: (d, b, blk(d, n), 0))
    return pl.pallas_call(
        functools.partial(_dn_scan_kernel, nc=nc),
        grid=(2, bsz, nb),
        in_specs=[spec(wd), spec(wd), spec(wd), spec(wd), spec(DN_HEADS * DN_CHUNK),
                  pl.BlockSpec((None, None, nc, 8, LANES), lambda d, b, n: (d, b, blk(d, n), 0, 0))],
        out_specs=spec(wd),
        out_shape=jax.ShapeDtypeStruct((2, bsz, s, wd), F32),
        scratch_shapes=[pltpu.VMEM((DN_HEADS, DN_HEAD_DIM, DN_HEAD_DIM), F32)],
        compiler_params=_params(("parallel", "parallel", "arbitrary")),
        name="deltanet_scan",
    )(u, w, qd, kd, at, egl)


def _band_attn_kernel(*refs, n_kv, grp, blk, tq, qs, t_len, with_sink, with_lse):
    q_ref, kp_ref, kc_ref, kn_ref, vp_ref, vc_ref, vn_ref = refs[:7]
    pos = 7
    sink_ref = None
    if with_sink:
        sink_ref = refs[pos]
        pos += 1
    o_ref = refs[pos]
    lse_ref = refs[pos + 1] if with_lse else None
    hd = HEAD_DIM
    i0 = pl.program_id(2) * tq
    kcat = jnp.concatenate([kp_ref[...], kc_ref[...], kn_ref[...]], axis=0)
    vcat = jnp.concatenate([vp_ref[...], vc_ref[...], vn_ref[...]], axis=0)
    kwin = qs + 2 * blk
    for sub in range(tq // qs):
        k0 = sub * qs
        rowpos = i0 + k0 + lax.broadcasted_iota(I32, (qs, kwin), 0)
        keypos = i0 - blk + k0 + lax.broadcasted_iota(I32, (qs, kwin), 1)
        mask = (jnp.abs(keypos - rowpos) <= blk) & (keypos >= 0) & (keypos < t_len)
        outs, lses = [], []
        for kv in range(n_kv):
            kh = kcat[k0:k0 + kwin, kv * hd:(kv + 1) * hd]
            vh = vcat[k0:k0 + kwin, kv * hd:(kv + 1) * hd]
            for g in range(grp):
                hq = kv * grp + g
                qh = q_ref[k0:k0 + qs, hq * hd:(hq + 1) * hd]
                sc = jnp.where(mask, _dot_nt(qh, kh) * (hd ** -0.5), NEG)
                m = jnp.max(sc, axis=-1, keepdims=True)
                if with_sink:
                    sk = sink_ref[hq]
                    m = jnp.maximum(m, sk)
                p = jnp.exp(sc - m)
                den = jnp.sum(p, axis=-1, keepdims=True)
                if with_sink:
                    den = den + jnp.exp(sk - m)
                outs.append(_dot(p.astype(BF16), vh) / den)
                if with_lse:
                    lses.append(jnp.broadcast_to(m + jnp.log(den), (qs, hd)))
        o_ref[k0:k0 + qs, :] = jnp.concatenate(outs, axis=1).astype(o_ref.dtype)
        if with_lse:
            lse_ref[k0:k0 + qs, :] = jnp.concatenate(lses, axis=1)


def _band_attn_call(q, k, v, *, n_res, q_cols, kv_cols, q_col_blk, kv_col_blk, o_cols, o_col_blk, n_kv, grp, blk,
                    sinks, with_lse, out_dtype, name):
    bsz, t_len, _ = q.shape
    tq = min(256, t_len)
    qs = min(128, tq)
    nt = t_len // tq
    ratio = tq // blk
    nblk = t_len // blk
    qw = n_kv * grp * HEAD_DIM
    kw = n_kv * HEAD_DIM
    cur = lambda wd, cb: pl.BlockSpec((None, tq, wd), lambda b, r, i: (b, i, cb(r)))
    prev = lambda wd, cb: pl.BlockSpec((None, blk, wd), lambda b, r, i: (b, jnp.maximum(i * ratio - 1, 0), cb(r)))
    nxt = lambda wd, cb: pl.BlockSpec((None, blk, wd),
                                      lambda b, r, i: (b, jnp.minimum((i + 1) * ratio, nblk - 1), cb(r)))
    in_specs = [cur(qw, q_col_blk), prev(kw, kv_col_blk), cur(kw, kv_col_blk), nxt(kw, kv_col_blk),
                prev(kw, kv_col_blk), cur(kw, kv_col_blk), nxt(kw, kv_col_blk)]
    args = [q, k, k, k, v, v, v]
    if sinks is not None:
        in_specs.append(pl.BlockSpec(memory_space=pltpu.SMEM))
        args.append(sinks)
    out_shape = [jax.ShapeDtypeStruct((bsz, t_len, n_res * o_cols), out_dtype)]
    out_specs = [pl.BlockSpec((None, tq, qw), lambda b, r, i: (b, i, o_col_blk(r)))]
    if with_lse:
        out_shape.append(jax.ShapeDtypeStruct((bsz, t_len, n_res * o_cols), F32))
        out_specs.append(pl.BlockSpec((None, tq, qw), lambda b, r, i: (b, i, o_col_blk(r))))
    return pl.pallas_call(
        functools.partial(_band_attn_kernel, n_kv=n_kv, grp=grp, blk=blk, tq=tq, qs=qs, t_len=t_len,
                          with_sink=sinks is not None, with_lse=with_lse),
        grid=(bsz, n_res, nt),
        in_specs=in_specs,
        out_specs=out_specs,
        out_shape=out_shape,
        compiler_params=_params(("parallel", "parallel", "parallel")),
        name=name,
    )(*args)


def _dilated_group(qb, kb, vb, gi, window, dil):
    bsz, s, _ = qb.shape
    t_len = s // dil
    steps = window // (2 * dil)
    ngrp = len(DIL_PAIRS)
    view = lambda a: a.reshape(bsz, t_len, dil * DIL_WIDTH)
    cb = lambda r: r * ngrp + gi
    o, lse = _band_attn_call(
        view(qb), view(kb), view(vb), n_res=dil, q_cols=DIL_WIDTH, kv_cols=DIL_WIDTH, q_col_blk=cb, kv_col_blk=cb,
        o_cols=DIL_GROUP_WIDTH, o_col_blk=lambda r: r, n_kv=DIL_HEADS_PER_GROUP, grp=1, blk=steps, sinks=None,
        with_lse=True, out_dtype=F32, name=f"dilated_attn_{dil}")
    return o.reshape(bsz, s, DIL_GROUP_WIDTH), lse.reshape(bsz, s, DIL_GROUP_WIDTH)


def _layer_norm(r, g, b):
    mu = jnp.mean(r, -1, keepdims=True)
    var = jnp.mean(jnp.square(r - mu), -1, keepdims=True)
    return (r - mu) * lax.rsqrt(var + LN_EPS) * g + b


def _post_mix(y, x_ref, mod_ref, lng_ref, lnb_ref, rwt_ref, x1_ref, aff_ref):
    m = mod_ref[...]
    x1 = _layer_norm(ALPHA * x_ref[...] + (1.0 + m[2:3]) * y, lng_ref[...], lnb_ref[...])
    x1_ref[...] = x1
    h2 = x1 * (1.0 + m[4:5]) + m[3:4]
    logits = _dot_nt(rwt_ref[...], h2, HI)
    e = jnp.exp(logits - jnp.max(logits, axis=0, keepdims=True))
    aff_ref[...] = e / jnp.sum(e, axis=0, keepdims=True)


def _outproj0_kernel(of_ref, ob_ref, z_ref, dnn_ref, og0_ref, og1_ref, og2_ref, l0_ref, l1_ref, l2_ref,
                     w_ref, x_ref, mod_ref, lng_ref, lnb_ref, rwt_ref, x1_ref, aff_ref):
    dk = DN_HEAD_DIM
    od = of_ref[...] + ob_ref[...]
    z = z_ref[...]
    parts = []
    for h in range(DN_HEADS):
        oh = od[:, h * dk:(h + 1) * dk]
        oh = oh * lax.rsqrt(jnp.mean(oh * oh, -1, keepdims=True) + NORM_EPS) * dnn_ref[...]
        parts.append(oh * _silu(z[:, h * dk:(h + 1) * dk]))
    o_dn = jnp.concatenate(parts, axis=1).astype(BF16)
    l0, l1, l2 = l0_ref[...], l1_ref[...], l2_ref[...]
    mx = jnp.maximum(jnp.maximum(l0, l1), l2)
    e0, e1, e2 = jnp.exp(l0 - mx), jnp.exp(l1 - mx), jnp.exp(l2 - mx)
    den = e0 + e1 + e2
    o_dil = ((e0 / den) * og0_ref[...] + (e1 / den) * og1_ref[...] + (e2 / den) * og2_ref[...]).astype(BF16)
    y = _dot(o_dn, w_ref[0:DN_WIDTH, :]) + _dot(o_dil, w_ref[DN_WIDTH:DN_WIDTH + DIL_GROUP_WIDTH, :])
    _post_mix(y, x_ref, mod_ref, lng_ref, lnb_ref, rwt_ref, x1_ref, aff_ref)


def _outproj1_kernel(o_ref, w_ref, x_ref, mod_ref, lng_ref, lnb_ref, rwt_ref, x1_ref, aff_ref):
    y = _dot(o_ref[...], w_ref[...])
    _post_mix(y, x_ref, mod_ref, lng_ref, lnb_ref, rwt_ref, x1_ref, aff_ref)


def _tail_specs(bsz, s, d, tm, n_e):
    row = lambda wd: pl.BlockSpec((None, tm, wd), lambda b, i: (b, i, 0))
    const = lambda shp: pl.BlockSpec(shp, lambda b, i: tuple(0 for _ in shp))
    in_specs = [row(d), pl.BlockSpec((None, 6, d), lambda b, i: (b, 0, 0)), const((1, d)), const((1, d)),
                const((n_e, d))]
    out_specs = [row(d), pl.BlockSpec((None, n_e, tm), lambda b, i: (b, 0, i))]
    out_shape = [jax.ShapeDtypeStruct((bsz, s, d), F32), jax.ShapeDtypeStruct((bsz, n_e, s), F32)]
    return in_specs, out_specs, out_shape


def _outproj0_call(o_scan, z, dn_norm, ogs, lses, w_out, x, mod, ln_g, ln_b, router_w, tm):
    bsz, s, d = x.shape
    n_e = router_w.shape[1]
    tail_in, out_specs, out_shape = _tail_specs(bsz, s, d, tm, n_e)
    row = lambda wd: pl.BlockSpec((None, tm, wd), lambda b, i: (b, i, 0))
    dirspec = lambda dd: pl.BlockSpec((None, None, tm, DN_WIDTH), lambda b, i: (dd, b, i, 0))
    gw = DIL_GROUP_WIDTH
    in_specs = [dirspec(0), dirspec(1), row(DN_WIDTH), pl.BlockSpec((1, DN_HEAD_DIM), lambda b, i: (0, 0)),
                row(gw), row(gw), row(gw), row(gw), row(gw), row(gw),
                pl.BlockSpec(w_out.shape, lambda b, i: (0, 0))] + tail_in
    return pl.pallas_call(
        _outproj0_kernel, grid=(bsz, s // tm), in_specs=in_specs, out_specs=out_specs, out_shape=out_shape,
        compiler_params=_params(("parallel", "parallel")), name="outproj_deltanet_dilated",
    )(o_scan, o_scan, z, dn_norm.reshape(1, -1), *ogs, *lses, w_out, x, mod, ln_g.reshape(1, d), ln_b.reshape(1, d),
      router_w.T)


def _outproj1_call(o, w_out, x, mod, ln_g, ln_b, router_w, tm):
    bsz, s, d = x.shape
    n_e = router_w.shape[1]
    tail_in, out_specs, out_shape = _tail_specs(bsz, s, d, tm, n_e)
    in_specs = [pl.BlockSpec((None, tm, o.shape[-1]), lambda b, i: (b, i, 0)),
                pl.BlockSpec(w_out.shape, lambda b, i: (0, 0))] + tail_in
    return pl.pallas_call(
        _outproj1_kernel, grid=(bsz, s // tm), in_specs=in_specs, out_specs=out_specs, out_shape=out_shape,
        compiler_params=_params(("parallel", "parallel")), name="outproj_swa",
    )(o, w_out, x, mod, ln_g.reshape(1, d), ln_b.reshape(1, d), router_w.T)


def _fold_cumsum(x01, upper, lstrict):
    within = _dot(x01.astype(BF16), upper)
    rowtot = jnp.broadcast_to(within[:, LANES - 1:LANES], within.shape)
    return within + _dot(lstrict, rowtot.astype(BF16))


def _topk_select_kernel(a_ref, c_ref, *, cap):
    a = a_ref[...]
    rows = a.shape[0]
    bits = pltpu.bitcast(a, I32)
    thr = jnp.zeros((1, 1), I32)
    for bit in range(30, -1, -1):
        cand = thr | (1 << bit)
        cnt = jnp.sum(jnp.where(bits >= cand, 1, 0), keepdims=True)
        thr = jnp.where(cnt >= cap, cand, thr)
    gt = bits > thr
    eq = bits == thr
    need = (cap - jnp.sum(jnp.where(gt, 1, 0), keepdims=True)).astype(F32)
    ru = lax.broadcasted_iota(I32, (LANES, LANES), 0)
    cu = lax.broadcasted_iota(I32, (LANES, LANES), 1)
    upper = jnp.where(ru <= cu, 1.0, 0.0).astype(BF16)
    rl = lax.broadcasted_iota(I32, (rows, rows), 0)
    cl = lax.broadcasted_iota(I32, (rows, rows), 1)
    lstrict = jnp.where(cl < rl, 1.0, 0.0).astype(BF16)
    eqf = jnp.where(eq, 1.0, 0.0).astype(F32)
    eq_before = _fold_cumsum(eqf, upper, lstrict) - eqf
    sel = jnp.where(gt, 1.0, jnp.where(eq & (eq_before < need), 1.0, 0.0)).astype(F32)
    c_ref[...] = _fold_cumsum(sel, upper, lstrict)


def _topk_select_call(aff, cap):
    bsz, n_e, s = aff.shape
    rows = s // LANES
    spec = pl.BlockSpec((None, None, rows, LANES), lambda b, e: (b, e, 0, 0))
    return pl.pallas_call(
        functools.partial(_topk_select_kernel, cap=cap),
        grid=(bsz, n_e), in_specs=[spec], out_specs=spec,
        out_shape=jax.ShapeDtypeStruct((bsz, n_e, rows, LANES), F32),
        compiler_params=_params(("parallel", "parallel")), name="topk_select",
    )(aff.reshape(bsz, n_e, rows, LANES)).reshape(bsz, n_e, s)


def _topk_extract_kernel(c_ref, idx_ref, *, cap, jb):
    c = c_ref[...]
    for j0 in range(0, cap, jb):
        jv = (j0 + lax.broadcasted_iota(I32, (jb, 1), 0)).astype(F32)
        idx_ref[j0:j0 + jb, :] = jnp.sum(jnp.where(c <= jv, 1, 0), axis=1, keepdims=True)


def _topk_extract_call(counts, cap):
    bsz, n_e, s = counts.shape
    jb = min(128, cap)
    out = pl.pallas_call(
        functools.partial(_topk_extract_kernel, cap=cap, jb=jb),
        grid=(bsz, n_e),
        in_specs=[pl.BlockSpec((None, None, 1, s), lambda b, e: (b, e, 0, 0))],
        out_specs=pl.BlockSpec((None, None, cap, 1), lambda b, e: (b, e, 0, 0)),
        out_shape=jax.ShapeDtypeStruct((bsz, n_e, cap, 1), I32),
        compiler_params=_params(("parallel", "parallel")), name="topk_extract",
    )(counts.reshape(bsz, n_e, 1, s))
    return out.reshape(bsz, n_e, cap)


def _moe_kernel(idx_hbm, aff_hbm, x_hbm, mod_ref, wg_ref, wu_ref, wd_ref, out_hbm,
                idx_s, aff_s, xg, xb, y, acc, sem_i, sem_a, sem_g, sem_o, *, cap, unroll):
    b = pl.program_id(0)
    e = pl.program_id(1)
    f = pl.program_id(2)
    n_e = pl.num_programs(1)
    n_f = pl.num_programs(2)

    @pl.when(f == 0)
    def _():
        @pl.when(e == 0)
        def _():
            acc[...] = jnp.zeros(acc.shape, acc.dtype)

        ci = pltpu.make_async_copy(idx_hbm.at[b, e], idx_s, sem_i)
        ca = pltpu.make_async_copy(aff_hbm.at[b, e], aff_s, sem_a)
        ci.start()
        ca.start()
        ci.wait()
        ca.wait()

        def gather(j, carry):
            t = idx_s[j]
            pltpu.make_async_copy(x_hbm.at[b, pl.ds(t, 1), :], xg.at[pl.ds(j, 1), :], sem_g).start()
            return carry

        lax.fori_loop(0, cap, gather, 0)
        pltpu.make_async_copy(x_hbm.at[b, pl.ds(0, cap), :], xg, sem_g).wait()
        m = mod_ref[...]
        xb[...] = (xg[...] * (1.0 + m[4:5]) + m[3:4]).astype(BF16)

    xv = xb[...]
    hid = (_silu(_dot(xv, wg_ref[...])) * _dot(xv, wu_ref[...])).astype(BF16)
    part = _dot(hid, wd_ref[...])

    @pl.when(f == 0)
    def _():
        y[...] = part

    @pl.when(f > 0)
    def _():
        y[...] = y[...] + part

    @pl.when(f == n_f - 1)
    def _():
        def scatter(jo, carry):
            for r in range(unroll):
                j = jo * unroll + r
                t = idx_s[j]
                acc[pl.ds(t, 1), :] = acc[pl.ds(t, 1), :] + aff_s[t] * y[pl.ds(j, 1), :]
            return carry

        lax.fori_loop(0, cap // unroll, scatter, 0)

        @pl.when(e == n_e - 1)
        def _():
            co = pltpu.make_async_copy(acc, out_hbm.at[b], sem_o)
            co.start()
            co.wait()


def _moe_call(x1, mod, idx, aff, wg, wu, wd, fcw):
    bsz, s, d = x1.shape
    n_e, _, ff = wg.shape
    cap = idx.shape[-1]
    fcw = min(fcw, ff)
    anyspec = pl.BlockSpec(memory_space=pl.ANY)
    return pl.pallas_call(
        functools.partial(_moe_kernel, cap=cap, unroll=8),
        grid=(bsz, n_e, ff // fcw),
        in_specs=[anyspec, anyspec, anyspec,
                  pl.BlockSpec((None, 6, d), lambda b, e, f: (b, 0, 0)),
                  pl.BlockSpec((None, d, fcw), lambda b, e, f: (e, 0, f)),
                  pl.BlockSpec((None, d, fcw), lambda b, e, f: (e, 0, f)),
                  pl.BlockSpec((None, fcw, d), lambda b, e, f: (e, f, 0))],
        out_specs=anyspec,
        out_shape=jax.ShapeDtypeStruct((bsz, s, d), F32),
        scratch_shapes=[pltpu.SMEM((cap,), I32), pltpu.SMEM((s,), F32),
                        pltpu.VMEM((cap, d), F32), pltpu.VMEM((cap, d), BF16), pltpu.VMEM((cap, d), F32),
                        pltpu.VMEM((s, d), F32),
                        pltpu.SemaphoreType.DMA(()), pltpu.SemaphoreType.DMA(()),
                        pltpu.SemaphoreType.DMA(()), pltpu.SemaphoreType.DMA(())],
        compiler_params=_params(("arbitrary", "arbitrary", "arbitrary")),
        name="moe_experts",
    )(idx, aff, x1, mod, wg, wu, wd)


def _ln2_kernel(x_ref, y_ref, mod_ref, g_ref, b_ref, o_ref):
    m = mod_ref[...]
    o_ref[...] = _layer_norm(ALPHA * x_ref[...] + (1.0 + m[5:6]) * y_ref[...], g_ref[...], b_ref[...])


def _ln2_call(x1, moe, mod, g, b, tm):
    bsz, s, d = x1.shape
    row = pl.BlockSpec((None, tm, d), lambda bb, i: (bb, i, 0))
    vec = pl.BlockSpec((1, d), lambda bb, i: (0, 0))
    return pl.pallas_call(
        _ln2_kernel, grid=(bsz, s // tm),
        in_specs=[row, row, pl.BlockSpec((None, 6, d), lambda bb, i: (bb, 0, 0)), vec, vec],
        out_specs=row, out_shape=jax.ShapeDtypeStruct((bsz, s, d), F32),
        compiler_params=_params(("parallel", "parallel")), name="ffn_postnorm",
    )(x1, moe, mod, g.reshape(1, d), b.reshape(1, d))


def _ffn_block(x1, aff, mod, wg, wu, wd, ln_g, ln_b, tm):
    s = x1.shape[1]
    cap = (EC_FACTOR * s) // N_EXPERTS
    counts = _topk_select_call(aff, cap)
    idx = _topk_extract_call(counts, cap)
    moe = _moe_call(x1, mod, idx, aff, wg.astype(BF16), wu.astype(BF16), wd.astype(BF16), fcw=256)
    return _ln2_call(x1, moe, mod, ln_g, ln_b, tm)


def kernel(x, c, positions, ada_w, ada_b, ab_w_in, ab_conv_w, ab_a_log, ab_dt_bias, ab_dn_norm, ab_w_out, swa_w_in,
           swa_sinks, swa_w_out, ln_mix_g, ln_mix_b, router_w, moe_w_gate, moe_w_up, moe_w_down, ln_ffn_g, ln_ffn_b):
    bsz, s, d = x.shape
    tm = min(512, s)
    mod = _mod_call(c, ada_w, ada_b).reshape(DEPTH, bsz, 6, d)
    rope = _rope_tables(positions)

    w_in = ab_w_in[0]
    n_a = 4 * DN_WIDTH
    n_g = 4 * DN_HEADS
    w0 = jnp.concatenate([w_in[:, :n_a], jnp.pad(w_in[:, n_a:n_a + n_g], ((0, 0), (0, LANES - n_g))),
                          w_in[:, n_a + n_g:]], axis=1).astype(BF16)
    c_z = 3 * DN_WIDTH
    c_g = c_z + DN_WIDTH
    c_q = c_g + LANES
    plan0 = _chunk_plan([(0, 3 * DN_WIDTH, 0, False), (c_z, DN_WIDTH, 1, False), (c_g, LANES, 2, False),
                         (c_q, DIL_WIDTH, 3, True), (c_q + DIL_WIDTH, DIL_WIDTH, 4, True),
                         (c_q + 2 * DIL_WIDTH, DIL_WIDTH, 5, False)])
    qkv_a, z, gates, qb, kb, vb = _inproj_call(
        x, mod[0], w0, rope, plan0, (3 * DN_WIDTH, DN_WIDTH, LANES, DIL_WIDTH, DIL_WIDTH, DIL_WIDTH),
        (F32, F32, F32, BF16, BF16, BF16), tm, "inproj_deltanet_dilated")
    u, w, qd, kd, at, egl = _dn_prep_call(qkv_a, gates, ab_conv_w[0], ab_a_log[0], ab_dt_bias[0], tm)
    o_scan = _dn_scan_call(u, w, qd, kd, at, egl, min(256, s))
    ogs, lses = [], []
    for gi, (window, dil) in enumerate(DIL_PAIRS):
        o_g, lse_g = _dilated_group(qb, kb, vb, gi, window, dil)
        ogs.append(o_g)
        lses.append(lse_g)
    x1, aff = _outproj0_call(o_scan, z, ab_dn_norm[0], ogs, lses, ab_w_out[0].astype(BF16), x, mod[0],
                             ln_mix_g[0], ln_mix_b[0], router_w[0], tm)
    x = _ffn_block(x1, aff, mod[0], moe_w_gate[0], moe_w_up[0], moe_w_down[0], ln_ffn_g[0], ln_ffn_b[0], tm)

    qw = SWA_Q_HEADS * HEAD_DIM
    kw = SWA_KV_HEADS * HEAD_DIM
    plan1 = _chunk_plan([(0, qw, 0, True), (qw, kw, 1, True), (qw + kw, kw, 2, False)])
    q1, k1, v1 = _inproj_call(x, mod[1], swa_w_in[0].astype(BF16), rope, plan1, (qw, kw, kw), (BF16, BF16, BF16), tm,
                              "inproj_swa")
    zero = lambda r: 0
    (o1,) = _band_attn_call(
        q1, k1, v1, n_res=1, q_cols=qw, kv_cols=kw, q_col_blk=zero, kv_col_blk=zero, o_cols=qw, o_col_blk=zero,
        n_kv=SWA_KV_HEADS, grp=SWA_Q_HEADS // SWA_KV_HEADS, blk=SWA_WINDOW, sinks=swa_sinks[0], with_lse=False,
        out_dtype=BF16, name="swa_attn")
    x1, aff = _outproj1_call(o1, swa_w_out[0].astype(BF16), x, mod[1], ln_mix_g[1], ln_mix_b[1], router_w[1], tm)
    x = _ffn_block(x1, aff, mod[1], moe_w_gate[1], moe_w_up[1], moe_w_down[1], ln_ffn_g[1], ln_ffn_b[1], tm)
    return x
```

```python
import functools
import math

import jax
import jax.numpy as jnp
from jax import lax
from jax.experimental import pallas as pl
from jax.experimental.pallas import tpu as pltpu

F32 = jnp.float32
BF16 = jnp.bfloat16
I32 = jnp.int32
HI = lax.Precision.HIGHEST

DEPTH = 2
HEAD_DIM = 64
ROT_DIM = HEAD_DIM // 4
ROPE_THETA = 500000.0
DN_HEADS = 4
DN_HEAD_DIM = 128
DN_CHUNK = 64
DN_CONV = 5
DN_WIDTH = DN_HEADS * DN_HEAD_DIM
DIL_PAIRS = ((128, 1), (512, 4), (2048, 16))
DIL_HEADS_PER_GROUP = 4
DIL_GROUP_WIDTH = DIL_HEADS_PER_GROUP * HEAD_DIM
DIL_WIDTH = DIL_GROUP_WIDTH * len(DIL_PAIRS)
SWA_Q_HEADS = 16
SWA_KV_HEADS = 4
SWA_WINDOW = 128
N_EXPERTS = 16
EC_FACTOR = 2
ALPHA = (2.0 * DEPTH) ** 0.25
LN_EPS = 1e-5
NORM_EPS = 1e-6
NEG = -1e30
LANES = 128
HALO = 8
VMEM_LIMIT = 56 * 1024 * 1024


def _dot(a, b, prec=None):
    return jnp.dot(a, b, preferred_element_type=F32, precision=prec)


def _dot_nt(a, b, prec=None):
    return lax.dot_general(a, b, (((1,), (1,)), ((), ())), preferred_element_type=F32, precision=prec)


def _dot_tn(a, b, prec=None):
    return lax.dot_general(a, b, (((0,), (0,)), ((), ())), preferred_element_type=F32, precision=prec)


def _silu(x):
    return x * jax.nn.sigmoid(x)


def _params(sem):
    return pltpu.CompilerParams(dimension_semantics=sem, vmem_limit_bytes=VMEM_LIMIT)


def _mod_kernel(c_ref, w_ref, b_ref, o_ref):
    o_ref[...] = _dot(_silu(c_ref[...]), w_ref[...], HI) + b_ref[...]


def _mod_call(c, ada_w, ada_b):
    depth, d, n6 = ada_w.shape
    bsz = c.shape[0]
    tn = n6 // 4
    return pl.pallas_call(
        _mod_kernel,
        grid=(depth, n6 // tn),
        in_specs=[pl.BlockSpec((bsz, d), lambda i, j: (0, 0)),
                  pl.BlockSpec((None, d, tn), lambda i, j: (i, 0, j)),
                  pl.BlockSpec((None, 1, tn), lambda i, j: (i, 0, j))],
        out_specs=pl.BlockSpec((None, bsz, tn), lambda i, j: (i, 0, j)),
        out_shape=jax.ShapeDtypeStruct((depth, bsz, n6), F32),
        compiler_params=_params(("arbitrary", "arbitrary")),
        name="adaln_mod",
    )(c, ada_w, ada_b.reshape(depth, 1, n6))


def _inproj_kernel(x_ref, mod_ref, w_ref, rope_ref, *out_refs, plan):
    m = mod_ref[...]
    h = (x_ref[...] * (1.0 + m[1:2]) + m[0:1]).astype(BF16)
    for c0, width, oi, o0, rope in plan:
        acc = _dot(h, w_ref[:, c0:c0 + width])
        if rope:
            reps = width // LANES
            cosv = jnp.concatenate([rope_ref[:, 0:LANES]] * reps, axis=1)
            sa = jnp.concatenate([rope_ref[:, LANES:2 * LANES]] * reps, axis=1)
            sb = jnp.concatenate([rope_ref[:, 2 * LANES:3 * LANES]] * reps, axis=1)
            half = ROT_DIM // 2
            acc = acc * cosv + pltpu.roll(acc, width - half, 1) * sa + pltpu.roll(acc, half, 1) * sb
        out_refs[oi][:, o0:o0 + width] = acc.astype(out_refs[oi].dtype)


def _inproj_call(x, mod, w, rope, plan, out_widths, out_dtypes, tm, name):
    bsz, s, d = x.shape
    n = w.shape[1]
    out_shape = [jax.ShapeDtypeStruct((bsz, s, ow), od) for ow, od in zip(out_widths, out_dtypes)]
    out_specs = [pl.BlockSpec((None, tm, ow), lambda b, i: (b, i, 0)) for ow in out_widths]
    return pl.pallas_call(
        functools.partial(_inproj_kernel, plan=plan),
        grid=(bsz, s // tm),
        in_specs=[pl.BlockSpec((None, tm, d), lambda b, i: (b, i, 0)),
                  pl.BlockSpec((None, 6, d), lambda b, i: (b, 0, 0)),
                  pl.BlockSpec((d, n), lambda b, i: (0, 0)),
                  pl.BlockSpec((None, tm, 3 * LANES), lambda b, i: (b, i, 0))],
        out_specs=out_specs,
        out_shape=out_shape,
        compiler_params=_params(("parallel", "parallel")),
        name=name,
    )(x, mod, w, rope)


def _chunk_plan(groups, chunk=256):
    plan = []
    for c0, width, oi, rope in groups:
        off = 0
        while off < width:
            wd = min(chunk, width - off)
            plan.append((c0 + off, wd, oi, off, rope))
            off += wd
    return tuple(plan)


def _rope_tables(positions):
    inv_freq = jnp.power(ROPE_THETA, -jnp.arange(0, ROT_DIM, 2, dtype=F32) / ROT_DIM)
    ang = positions.astype(F32)[..., None] * inv_freq
    cos, sin = jnp.cos(ang), jnp.sin(ang)
    half = ROT_DIM // 2
    rest = HEAD_DIM - ROT_DIM
    ones = jnp.ones(cos.shape[:-1] + (rest,), F32)
    z_half = jnp.zeros(cos.shape[:-1] + (half,), F32)
    z_rest = jnp.zeros(cos.shape[:-1] + (rest,), F32)
    c64 = jnp.concatenate([cos, cos, ones], -1)
    sa64 = jnp.concatenate([-sin, z_half, z_rest], -1)
    sb64 = jnp.concatenate([z_half, sin, z_rest], -1)
    return jnp.concatenate([c64, c64, sa64, sa64, sb64, sb64], -1)


def _dn_prep_kernel(xa_ref, top_ref, bot_ref, gt_ref, cw_ref, alog_ref, dtb_ref,
                    u_ref, w_ref, qd_ref, kd_ref, at_ref, egl_ref,
                    q_s, k_s, v_s, gc_s, *, t):
    ch = DN_CHUNK
    dk = DN_HEAD_DIM
    pad = (DN_CONV - 1) // 2
    for grp, dst in enumerate((q_s, k_s, v_s)):
        cols = slice(grp * DN_WIDTH, (grp + 1) * DN_WIDTH)
        xe = jnp.concatenate([top_ref[:, cols], xa_ref[:, cols], bot_ref[:, cols]], axis=0)
        y = jnp.zeros((t, DN_WIDTH), F32)
        for k in range(DN_CONV):
            y = y + xe[HALO - pad + k:HALO - pad + k + t, :] * cw_ref[k:k + 1, cols]
        y = _silu(y)
        if grp < 2:
            scale = dk ** -0.5 if grp == 0 else 1.0
            parts = []
            for h in range(DN_HEADS):
                yh = y[:, h * dk:(h + 1) * dk]
                parts.append(yh * lax.rsqrt(jnp.sum(yh * yh, -1, keepdims=True) + NORM_EPS) * scale)
            y = jnp.concatenate(parts, axis=1)
        dst[...] = y

    g = gt_ref[...]
    lane = lax.broadcasted_iota(I32, (t, LANES), 1)
    z = g + dtb_ref[...]
    softplus = jnp.maximum(z, 0.0) + jnp.log1p(jnp.exp(-jnp.abs(z)))
    dec = -jnp.exp(alog_ref[...]) * softplus
    gv = jnp.where(lane < 2 * DN_HEADS, dec, jnp.where(lane < 4 * DN_HEADS, jax.nn.sigmoid(g), 0.0))
    ri = lax.broadcasted_iota(I32, (t, t), 0)
    ci = lax.broadcasted_iota(I32, (t, t), 1)
    shift = int(math.log2(ch))
    same = (ri >> shift) == (ci >> shift)
    pre = jnp.where(same & (ci <= ri), 1.0, 0.0).astype(F32)
    suf = jnp.where(same & (ci >= ri), 1.0, 0.0).astype(F32)
    gcf = _dot(pre, gv, HI)
    gcb = _dot(suf, gv, HI)
    gc_s[...] = jnp.where(lane < DN_HEADS, gcf, jnp.where(lane < 2 * DN_HEADS, gcb, gv))

    rr = lax.broadcasted_iota(I32, (ch, 2 * ch), 0)
    cc = lax.broadcasted_iota(I32, (ch, 2 * ch), 1)
    fwd = cc < ch
    cj = jnp.where(fwd, cc, cc - ch)
    ahead = jnp.where(fwd, cj - rr, rr - cj)
    incl = ahead <= 0
    strict = ahead < 0
    eye2 = jnp.where(cj == rr, 1.0, 0.0).astype(F32)

    def blockdiag(p):
        return jnp.concatenate([jnp.where(fwd, p, 0.0), jnp.where(fwd, 0.0, p)], axis=0).astype(BF16)

    n_sq = int(math.log2(ch)) - 1
    per_iter = 2 if (t // ch) % 2 == 0 else 1

    def chunk_body(ci, carry):
        units = []
        for sub in range(per_iter):
            c = ci * per_iter + sub
            rows = pl.ds(pl.multiple_of(c * ch, ch), ch)
            gcc = gc_s[rows, :]
            gct = gcc.T
            for h in range(DN_HEADS):
                units.append((c, rows, gcc, gct, h))

        st, pws = [], []
        for c, rows, gcc, gct, h in units:
            bcast = lambda col, gcc=gcc: jnp.broadcast_to(gcc[:, col:col + 1], (ch, LANES))
            gf, gb = bcast(h), bcast(DN_HEADS + h)
            bf, bb = bcast(2 * DN_HEADS + h), bcast(3 * DN_HEADS + h)
            grow = jnp.concatenate([gct[h:h + 1, :], gct[DN_HEADS + h:DN_HEADS + h + 1, :]], axis=1)
            diff = jnp.where(fwd, gf, gb) - grow
            decay = jnp.where(incl, jnp.exp(jnp.where(incl, diff, 0.0)), 0.0)
            hs = slice(h * dk, (h + 1) * dk)
            k16 = k_s[rows, hs].astype(BF16)
            kk = jnp.concatenate([k16, k16], axis=0)
            lower = jnp.where(strict, jnp.where(fwd, bf, bb) * _dot_nt(k16, kk) * decay, 0.0)
            intra = (_dot_nt(q_s[rows, hs].astype(BF16), kk) * decay).astype(BF16)
            at_ref[0, rows, h * ch:(h + 1) * ch] = intra[:, :ch]
            at_ref[1, rows, h * ch:(h + 1) * ch] = intra[:, ch:]
            st.append((gf, gb, bf, bb))
            pws.append(-lower)
        ainvs = [eye2 + p for p in pws]
        pbds = [blockdiag(p) for p in pws]
        for _ in range(n_sq):
            pws = [_dot(p.astype(BF16), bd) for p, bd in zip(pws, pbds)]
            pbds = [blockdiag(p) for p in pws]
            ainvs = [a + _dot(a.astype(BF16), bd) for a, bd in zip(ainvs, pbds)]

        egl_f, egl_b = [], []
        for (c, rows, _, _, h), (gf, gb, bf, bb), ainv in zip(units, st, ainvs):
            hs = slice(h * dk, (h + 1) * dk)
            q = q_s[rows, hs]
            k = k_s[rows, hs]
            v = v_s[rows, hs]
            egf, egb = jnp.exp(gf), jnp.exp(gb)
            rhs = jnp.concatenate([jnp.concatenate([v * bf, k * bf * egf], axis=1),
                                   jnp.concatenate([v * bb, k * bb * egb], axis=1)], axis=0).astype(BF16)
            uw_f = _dot(jnp.where(fwd, ainv, 0.0).astype(BF16), rhs)
            uw_b = _dot(jnp.where(fwd, 0.0, ainv).astype(BF16), rhs)
            glf = jnp.broadcast_to(gf[ch - 1:ch, :], (ch, LANES))
            glb = jnp.broadcast_to(gb[0:1, :], (ch, LANES))
            u_ref[0, rows, hs] = uw_f[:, :dk]
            u_ref[1, rows, hs] = uw_b[:, :dk]
            w_ref[0, rows, hs] = uw_f[:, dk:].astype(BF16)
            w_ref[1, rows, hs] = uw_b[:, dk:].astype(BF16)
            qd_ref[0, rows, hs] = (q * egf).astype(BF16)
            qd_ref[1, rows, hs] = (q * egb).astype(BF16)
            kd_ref[0, rows, hs] = (k * jnp.exp(glf - gf)).astype(BF16)
            kd_ref[1, rows, hs] = (k * jnp.exp(glb - gb)).astype(BF16)
            egl_f.append(jnp.exp(glf[0:1, :]))
            egl_b.append(jnp.exp(glb[0:1, :]))
            if h == DN_HEADS - 1:
                fill = [jnp.zeros((8 - DN_HEADS, LANES), F32)]
                egl_ref[0, c] = jnp.concatenate(egl_f + fill, axis=0)
                egl_ref[1, c] = jnp.concatenate(egl_b + fill, axis=0)
                egl_f, egl_b = [], []
        return carry

    lax.fori_loop(0, t // (ch * per_iter), chunk_body, 0)


def _dn_prep_call(qkv_a, gates, conv_w, a_log, dt_bias, t):
    bsz, s, cw = qkv_a.shape
    nt = s // t
    r = qkv_a.reshape(bsz, nt, t, cw)
    zero = jnp.zeros((bsz, 1, HALO, cw), F32)
    top = jnp.concatenate([zero, r[:, :-1, t - HALO:]], axis=1)
    bot = jnp.concatenate([r[:, 1:, :HALO], zero], axis=1)
    cwp = jnp.zeros((8, cw), F32).at[:DN_CONV].set(conv_w)
    alog = jnp.zeros((1, LANES), F32).at[0, :2 * DN_HEADS].set(a_log.reshape(-1))
    dtb = jnp.zeros((1, LANES), F32).at[0, :2 * DN_HEADS].set(dt_bias.reshape(-1))
    nch = s // DN_CHUNK
    wide = lambda dt, wd: jax.ShapeDtypeStruct((2, bsz, s, wd), dt)
    spec = lambda wd: pl.BlockSpec((2, None, t, wd), lambda b, i: (0, b, i, 0))
    return pl.pallas_call(
        functools.partial(_dn_prep_kernel, t=t),
        grid=(bsz, nt),
        in_specs=[pl.BlockSpec((None, t, cw), lambda b, i: (b, i, 0)),
                  pl.BlockSpec((None, None, HALO, cw), lambda b, i: (b, i, 0, 0)),
                  pl.BlockSpec((None, None, HALO, cw), lambda b, i: (b, i, 0, 0)),
                  pl.BlockSpec((None, t, LANES), lambda b, i: (b, i, 0)),
                  pl.BlockSpec((8, cw), lambda b, i: (0, 0)),
                  pl.BlockSpec((1, LANES), lambda b, i: (0, 0)),
                  pl.BlockSpec((1, LANES), lambda b, i: (0, 0))],
        out_specs=[spec(DN_WIDTH), spec(DN_WIDTH), spec(DN_WIDTH), spec(DN_WIDTH), spec(DN_HEADS * DN_CHUNK),
                   pl.BlockSpec((2, None, t // DN_CHUNK, 8, LANES), lambda b, i: (0, b, i, 0, 0))],
        out_shape=[wide(F32, DN_WIDTH), wide(BF16, DN_WIDTH), wide(BF16, DN_WIDTH), wide(BF16, DN_WIDTH),
                   wide(BF16, DN_HEADS * DN_CHUNK),
                   jax.ShapeDtypeStruct((2, bsz, nch, 8, LANES), F32)],
        scratch_shapes=[pltpu.VMEM((t, DN_WIDTH), F32), pltpu.VMEM((t, DN_WIDTH), F32),
                        pltpu.VMEM((t, DN_WIDTH), F32), pltpu.VMEM((t, LANES), F32)],
        compiler_params=_params(("parallel", "parallel")),
        name="deltanet_prep",
    )(qkv_a, top, bot, gates, cwp, alog, dtb)


def _dn_scan_kernel(u_ref, w_ref, qd_ref, kd_ref, at_ref, egl_ref, o_ref, st_ref, *, nc):
    ch = DN_CHUNK
    dk = DN_HEAD_DIM
    d = pl.program_id(0)

    @pl.when(pl.program_id(2) == 0)
    def _():
        st_ref[...] = jnp.zeros(st_ref.shape, st_ref.dtype)

    heads = range(DN_HEADS)
    hsl = [slice(h * dk, (h + 1) * dk) for h in heads]
    sts = [st_ref[h] for h in heads]
    for j in range(nc):
        cc = j + d * (nc - 1 - 2 * j)
        rows = pl.ds(pl.multiple_of(cc * ch, ch), ch)
        egl = egl_ref[cc]
        sbs = [st.astype(BF16) for st in sts]
        vbs = [(u_ref[rows, hsl[h]] - _dot(w_ref[rows, hsl[h]], sbs[h])).astype(BF16) for h in heads]
        qss = [_dot(qd_ref[rows, hsl[h]], sbs[h]) for h in heads]
        sts = [sts[h] * egl[h:h + 1, :] + _dot_tn(kd_ref[rows, hsl[h]], vbs[h]) for h in heads]
        for h in heads:
            o_ref[rows, hsl[h]] = qss[h] + _dot(at_ref[rows, h * ch:(h + 1) * ch], vbs[h])
    for h in heads:
        st_ref[h] = sts[h]


def _dn_scan_call(u, w, qd, kd, at, egl, tc):
    _, bsz, s, wd = u.shape
    nb = s // tc
    nc = tc // DN_CHUNK

    def blk(d, n):
        return n + d * (nb - 1 - 2 * n)

    spec = lambda width: pl.BlockSpec((None, None, tc, width), lambda d, b, n: (d, b, blk(d, n), 0))
    return pl.pallas_call(
        functools.partial(_dn_scan_kernel, nc=nc),
        grid=(2, bsz, nb),
        in_specs=[spec(wd), spec(wd), spec(wd), spec(wd), spec(DN_HEADS * DN_CHUNK),
                  pl.BlockSpec((None, None, nc, 8, LANES), lambda d, b, n: (d, b, blk(d, n), 0, 0))],
        out_specs=spec(wd),
        out_shape=jax.ShapeDtypeStruct((2, bsz, s, wd), F32),
        scratch_shapes=[pltpu.VMEM((DN_HEADS, DN_HEAD_DIM, DN_HEAD_DIM), F32)],
        compiler_params=_params(("parallel", "parallel", "arbitrary")),
        name="deltanet_scan",
    )(u, w, qd, kd, at, egl)


def _band_attn_kernel(*refs, n_kv, grp, blk, tq, qs, t_len, with_sink, with_lse):
    q_ref, kp_ref, kc_ref, kn_ref, vp_ref, vc_ref, vn_ref = refs[:7]
    pos = 7
    sink_ref = None
    if with_sink:
        sink_ref = refs[pos]
        pos += 1
    o_ref = refs[pos]
    lse_ref = refs[pos + 1] if with_lse else None
    hd = HEAD_DIM
    i0 = pl.program_id(2) * tq
    kcat = jnp.concatenate([kp_ref[...], kc_ref[...], kn_ref[...]], axis=0)
    vcat = jnp.concatenate([vp_ref[...], vc_ref[...], vn_ref[...]], axis=0)
    kwin = qs + 2 * blk
    for sub in range(tq // qs):
        k0 = sub * qs
        rowpos = i0 + k0 + lax.broadcasted_iota(I32, (qs, kwin), 0)
        keypos = i0 - blk + k0 + lax.broadcasted_iota(I32, (qs, kwin), 1)
        mask = (jnp.abs(keypos - rowpos) <= blk) & (keypos >= 0) & (keypos < t_len)
        bias = jnp.where(mask, 0.0, NEG)
        lse = jnp.zeros((qs, LSE_COLS), F32)
        lse_lane = lax.broadcasted_iota(I32, (qs, LSE_COLS), 1)
        n_q = n_kv * grp
        khs = [kcat[k0:k0 + kwin, kv * hd:(kv + 1) * hd] for kv in range(n_kv)]
        vhs = [vcat[k0:k0 + kwin, kv * hd:(kv + 1) * hd] for kv in range(n_kv)]
        outs = []
        for h0 in range(0, n_q, HEADS_PER_STAGE):
            hqs = range(h0, min(h0 + HEADS_PER_STAGE, n_q))
            scs = [_dot_nt(q_ref[k0:k0 + qs, hq * hd:(hq + 1) * hd], khs[hq // grp]) + bias for hq in hqs]
            ps, dens = [], []
            for hq, sc in zip(hqs, scs):
                m = jnp.max(sc, axis=-1, keepdims=True)
                if with_sink:
                    sk = sink_ref[hq]
                    m = jnp.maximum(m, sk)
                p = jnp.exp(sc - m)
                den = jnp.sum(p, axis=-1, keepdims=True)
                if with_sink:
                    den = den + jnp.exp(sk - m)
                ps.append(p.astype(BF16))
                dens.append(den)
                if with_lse:
                    lse = jnp.where(lse_lane == hq, m + jnp.log(den), lse)
            outs += [_dot(p, vhs[hq // grp]) / den for hq, p, den in zip(hqs, ps, dens)]
        o_ref[k0:k0 + qs, :] = jnp.concatenate(outs, axis=1).astype(o_ref.dtype)
        if with_lse:
            lse_ref[k0:k0 + qs, :] = lse


LSE_COLS = 8
HEADS_PER_STAGE = 8


def _band_attn_call(q, k, v, *, n_res, n_kv, grp, blk, sinks, with_lse, name):
    bsz, t_len, _ = q.shape
    tq = min(256, t_len)
    qs = min(128, tq)
    nt = t_len // tq
    ratio = tq // blk
    nblk = t_len // blk
    qw = n_kv * grp * HEAD_DIM
    kw = n_kv * HEAD_DIM
    cur = lambda wd: pl.BlockSpec((None, tq, wd), lambda b, r, i: (b, i, r))
    prev = lambda wd: pl.BlockSpec((None, blk, wd), lambda b, r, i: (b, jnp.maximum(i * ratio - 1, 0), r))
    nxt = lambda wd: pl.BlockSpec((None, blk, wd), lambda b, r, i: (b, jnp.minimum((i + 1) * ratio, nblk - 1), r))
    in_specs = [cur(qw), prev(kw), cur(kw), nxt(kw), prev(kw), cur(kw), nxt(kw)]
    args = [q, k, k, k, v, v, v]
    if sinks is not None:
        in_specs.append(pl.BlockSpec(memory_space=pltpu.SMEM))
        args.append(sinks)
    out_shape = [jax.ShapeDtypeStruct((bsz, t_len, n_res * qw), BF16)]
    out_specs = [cur(qw)]
    if with_lse:
        out_shape.append(jax.ShapeDtypeStruct((bsz, n_res, t_len, LSE_COLS), F32))
        out_specs.append(pl.BlockSpec((None, None, tq, LSE_COLS), lambda b, r, i: (b, r, i, 0)))
    return pl.pallas_call(
        functools.partial(_band_attn_kernel, n_kv=n_kv, grp=grp, blk=blk, tq=tq, qs=qs, t_len=t_len,
                          with_sink=sinks is not None, with_lse=with_lse),
        grid=(bsz, n_res, nt),
        in_specs=in_specs,
        out_specs=out_specs,
        out_shape=out_shape,
        compiler_params=_params(("parallel", "parallel", "parallel")),
        name=name,
    )(*args)


def _dilated_group(qg, kg, vg, window, dil):
    bsz, s, wd = qg.shape
    t_len = s // dil
    view = lambda a: a.reshape(bsz, t_len, dil * wd)
    o, lse = _band_attn_call(view(qg), view(kg), view(vg), n_res=dil, n_kv=DIL_HEADS_PER_GROUP, grp=1,
                             blk=window // (2 * dil), sinks=None, with_lse=True, name=f"dilated_attn_{dil}")
    return o.reshape(bsz, s, wd), jnp.swapaxes(lse, 1, 2).reshape(bsz, s, LSE_COLS)


def _layer_norm(r, g, b):
    mu = jnp.mean(r, -1, keepdims=True)
    var = jnp.mean(jnp.square(r - mu), -1, keepdims=True)
    return (r - mu) * lax.rsqrt(var + LN_EPS) * g + b


def _post_mix(y, x_ref, mod_ref, lng_ref, lnb_ref, rwt_ref, x1_ref, aff_ref):
    m = mod_ref[...]
    x1 = _layer_norm(ALPHA * x_ref[...] + (1.0 + m[2:3]) * y, lng_ref[...], lnb_ref[...])
    x1_ref[...] = x1
    h2 = x1 * (1.0 + m[4:5]) + m[3:4]
    logits = _dot_nt(rwt_ref[...], h2, HI)
    e = jnp.exp(logits - jnp.max(logits, axis=0, keepdims=True))
    aff_ref[...] = e / jnp.sum(e, axis=0, keepdims=True)


def _outproj0_kernel(of_ref, ob_ref, z_ref, dnn_ref, og0_ref, og1_ref, og2_ref, l0_ref, l1_ref, l2_ref,
                     w_ref, x_ref, mod_ref, lng_ref, lnb_ref, rwt_ref, x1_ref, aff_ref):
    dk = DN_HEAD_DIM
    od = of_ref[...] + ob_ref[...]
    z = z_ref[...]
    parts = []
    for h in range(DN_HEADS):
        oh = od[:, h * dk:(h + 1) * dk]
        oh = oh * lax.rsqrt(jnp.mean(oh * oh, -1, keepdims=True) + NORM_EPS) * dnn_ref[...]
        parts.append(oh * _silu(z[:, h * dk:(h + 1) * dk]))
    o_dn = jnp.concatenate(parts, axis=1).astype(BF16)
    l0, l1, l2 = l0_ref[...], l1_ref[...], l2_ref[...]
    mx = jnp.maximum(jnp.maximum(l0, l1), l2)
    e0, e1, e2 = jnp.exp(l0 - mx), jnp.exp(l1 - mx), jnp.exp(l2 - mx)
    den = e0 + e1 + e2
    head = lax.broadcasted_iota(I32, (l0.shape[0], DIL_GROUP_WIDTH), 1) >> int(math.log2(HEAD_DIM))

    def per_lane(wt):
        out = jnp.zeros(head.shape, F32)
        for h in range(DIL_HEADS_PER_GROUP):
            out = jnp.where(head == h, wt[:, h:h + 1], out)
        return out

    o_dil = (per_lane(e0 / den) * og0_ref[...].astype(F32) + per_lane(e1 / den) * og1_ref[...].astype(F32)
             + per_lane(e2 / den) * og2_ref[...].astype(F32)).astype(BF16)
    y = _dot(o_dn, w_ref[0:DN_WIDTH, :]) + _dot(o_dil, w_ref[DN_WIDTH:DN_WIDTH + DIL_GROUP_WIDTH, :])
    _post_mix(y, x_ref, mod_ref, lng_ref, lnb_ref, rwt_ref, x1_ref, aff_ref)


def _outproj1_kernel(o_ref, w_ref, x_ref, mod_ref, lng_ref, lnb_ref, rwt_ref, x1_ref, aff_ref):
    y = _dot(o_ref[...], w_ref[...])
    _post_mix(y, x_ref, mod_ref, lng_ref, lnb_ref, rwt_ref, x1_ref, aff_ref)


def _tail_specs(bsz, s, d, tm, n_e):
    row = lambda wd: pl.BlockSpec((None, tm, wd), lambda b, i: (b, i, 0))
    const = lambda shp: pl.BlockSpec(shp, lambda b, i: tuple(0 for _ in shp))
    in_specs = [row(d), pl.BlockSpec((None, 6, d), lambda b, i: (b, 0, 0)), const((1, d)), const((1, d)),
                const((n_e, d))]
    out_specs = [row(d), pl.BlockSpec((None, n_e, tm), lambda b, i: (b, 0, i))]
    out_shape = [jax.ShapeDtypeStruct((bsz, s, d), F32), jax.ShapeDtypeStruct((bsz, n_e, s), F32)]
    return in_specs, out_specs, out_shape


def _outproj0_call(o_scan, z, dn_norm, ogs, lses, w_out, x, mod, ln_g, ln_b, router_w, tm):
    bsz, s, d = x.shape
    n_e = router_w.shape[1]
    tail_in, out_specs, out_shape = _tail_specs(bsz, s, d, tm, n_e)
    row = lambda wd: pl.BlockSpec((None, tm, wd), lambda b, i: (b, i, 0))
    dirspec = lambda dd: pl.BlockSpec((None, None, tm, DN_WIDTH), lambda b, i: (dd, b, i, 0))
    gw = DIL_GROUP_WIDTH
    in_specs = [dirspec(0), dirspec(1), row(DN_WIDTH), pl.BlockSpec((1, DN_HEAD_DIM), lambda b, i: (0, 0)),
                row(gw), row(gw), row(gw), row(LSE_COLS), row(LSE_COLS), row(LSE_COLS),
                pl.BlockSpec(w_out.shape, lambda b, i: (0, 0))] + tail_in
    return pl.pallas_call(
        _outproj0_kernel, grid=(bsz, s // tm), in_specs=in_specs, out_specs=out_specs, out_shape=out_shape,
        compiler_params=_params(("parallel", "parallel")), name="outproj_deltanet_dilated",
    )(o_scan, o_scan, z, dn_norm.reshape(1, -1), *ogs, *lses, w_out, x, mod, ln_g.reshape(1, d), ln_b.reshape(1, d),
      router_w.T)


def _outproj1_call(o, w_out, x, mod, ln_g, ln_b, router_w, tm):
    bsz, s, d = x.shape
    n_e = router_w.shape[1]
    tail_in, out_specs, out_shape = _tail_specs(bsz, s, d, tm, n_e)
    in_specs = [pl.BlockSpec((None, tm, o.shape[-1]), lambda b, i: (b, i, 0)),
                pl.BlockSpec(w_out.shape, lambda b, i: (0, 0))] + tail_in
    return pl.pallas_call(
        _outproj1_kernel, grid=(bsz, s // tm), in_specs=in_specs, out_specs=out_specs, out_shape=out_shape,
        compiler_params=_params(("parallel", "parallel")), name="outproj_swa",
    )(o, w_out, x, mod, ln_g.reshape(1, d), ln_b.reshape(1, d), router_w.T)


def _topk_kernel(a_ref, idx_ref, *, cap, jb):
    a = a_ref[...]
    n_e, rows, _ = a.shape
    bits = pltpu.bitcast(a, I32)
    thr = jnp.zeros((n_e, 1, 1), I32)
    for bit in range(30, -1, -1):
        cand = thr | (1 << bit)
        cnt = jnp.sum(jnp.where(bits >= cand, 1, 0), axis=(1, 2), keepdims=True)
        thr = jnp.where(cnt >= cap, cand, thr)
    gt = jnp.where(bits > thr, 1.0, 0.0).astype(F32)
    eq = jnp.where(bits == thr, 1.0, 0.0).astype(F32)
    need = cap - jnp.sum(gt, axis=(1, 2), keepdims=True)
    ru = lax.broadcasted_iota(I32, (LANES, LANES), 0)
    cu = lax.broadcasted_iota(I32, (LANES, LANES), 1)
    upper = jnp.where(ru <= cu, 1.0, 0.0).astype(BF16)
    rl = lax.broadcasted_iota(I32, (rows, rows), 0)
    cl = lax.broadcasted_iota(I32, (rows, rows), 1)
    lstrict = jnp.where(cl < rl, 1.0, 0.0).astype(BF16)

    def fold_cumsum(x01):
        within = _dot(x01.astype(BF16), upper)
        rowtot = jnp.broadcast_to(within[:, LANES - 1:LANES], within.shape)
        before = _dot(lstrict, rowtot.astype(BF16))
        return within + before, before + rowtot

    rowid = lax.broadcasted_iota(I32, (rows, jb), 0).astype(F32)
    for e in range(n_e):
        eq_before = fold_cumsum(eq[e])[0] - eq[e]
        sel = jnp.maximum(gt[e], jnp.where(eq_before < need[e], eq[e], 0.0))
        count, count_end = fold_cumsum(sel)
        for j0 in range(0, cap, jb):
            slot = (j0 + lax.broadcasted_iota(I32, (1, jb), 1)).astype(F32)
            row = jnp.sum(jnp.where(count_end[:, 0:1] <= slot, 1.0, 0.0), axis=0, keepdims=True)
            onehot = jnp.where(rowid == row, 1.0, 0.0)
            count_row = _dot_tn(count, onehot, HI)
            lane = jnp.sum(jnp.where(count_row <= slot, 1.0, 0.0), axis=0, keepdims=True)
            idx_ref[e:e + 1, j0:j0 + jb] = (row * LANES + lane).astype(I32)


def _topk_call(aff, cap):
    bsz, n_e, s = aff.shape
    rows = s // LANES
    return pl.pallas_call(
        functools.partial(_topk_kernel, cap=cap, jb=min(512, cap)),
        grid=(bsz,),
        in_specs=[pl.BlockSpec((None, n_e, rows, LANES), lambda b: (b, 0, 0, 0))],
        out_specs=pl.BlockSpec((None, n_e, cap), lambda b: (b, 0, 0)),
        out_shape=jax.ShapeDtypeStruct((bsz, n_e, cap), I32),
        compiler_params=_params(("parallel",)), name="topk_route",
    )(aff.reshape(bsz, n_e, rows, LANES))


SUBLANES = 8


def _moe_kernel(idx_hbm, aff_hbm, x_hbm, mod_ref, wg_ref, wu_ref, wd_ref, out_hbm,
                idx_s, aff_s, xg, hid_s, y, acc, sem_i, sem_a, sem_g, sem_o, *, cap, n_f):
    b = pl.program_id(0)
    e = pl.program_id(1)
    f = pl.program_id(2)
    n_e = pl.num_programs(1)

    @pl.when(f == 0)
    def _():
        @pl.when(e == 0)
        def _():
            acc[...] = jnp.zeros(acc.shape, acc.dtype)

        ci = pltpu.make_async_copy(idx_hbm.at[b, e], idx_s, sem_i)
        ca = pltpu.make_async_copy(aff_hbm.at[b, e], aff_s, sem_a)
        ci.start()
        ca.start()
        ci.wait()
        ca.wait()

        def gather(jo, carry):
            for r in range(SUBLANES):
                j = jo * SUBLANES + r
                pltpu.make_async_copy(x_hbm.at[b, pl.ds(idx_s[j], 1), :], xg.at[pl.ds(j, 1), :], sem_g).start()
            return carry

        lax.fori_loop(0, cap // SUBLANES, gather, 0)
        pltpu.make_async_copy(x_hbm.at[b, pl.ds(0, cap), :], xg, sem_g).wait()

    m = mod_ref[...]
    xv = (xg[...] * (1.0 + m[4:5]) + m[3:4]).astype(BF16)
    hid_s[f] = (_silu(_dot(xv, wg_ref[...])) * _dot(xv, wu_ref[...])).astype(BF16)

    @pl.when(f == n_f - 1)
    def _():
        hid = jnp.concatenate([hid_s[i] for i in range(n_f)], axis=1)
        y[...] = _dot(hid, wd_ref[...])
        lanes = acc.shape[-1]

        def scatter(jo, carry):
            j0 = pl.multiple_of(jo * SUBLANES, SUBLANES)
            yc = y[pl.ds(j0, SUBLANES), :].reshape(SUBLANES, acc.shape[1], lanes)
            toks = [idx_s[j0 + r] for r in range(SUBLANES)]
            olds = [acc[toks[r]] for r in range(SUBLANES)]
            for r in range(SUBLANES):
                acc[toks[r]] = olds[r] + aff_s[toks[r]] * yc[r]
            return carry

        lax.fori_loop(0, cap // SUBLANES, scatter, 0)

        @pl.when(e == n_e - 1)
        def _():
            co = pltpu.make_async_copy(acc, out_hbm.at[b], sem_o)
            co.start()
            co.wait()


def _moe_call(x1, mod, idx, aff, wg, wu, wd, fcw):
    bsz, s, d = x1.shape
    n_e, _, ff = wg.shape
    cap = idx.shape[-1]
    fcw = min(fcw, ff)
    n_f = ff // fcw
    anyspec = pl.BlockSpec(memory_space=pl.ANY)
    return pl.pallas_call(
        functools.partial(_moe_kernel, cap=cap, n_f=n_f),
        grid=(bsz, n_e, n_f),
        in_specs=[anyspec, anyspec, anyspec,
                  pl.BlockSpec((None, 6, d), lambda b, e, f: (b, 0, 0)),
                  pl.BlockSpec((None, d, fcw), lambda b, e, f: (e, 0, f)),
                  pl.BlockSpec((None, d, fcw), lambda b, e, f: (e, 0, f)),
                  pl.BlockSpec((None, ff, d), lambda b, e, f: (e, 0, 0))],
        out_specs=anyspec,
        out_shape=jax.ShapeDtypeStruct((bsz, s, d // LANES, LANES), F32),
        scratch_shapes=[pltpu.SMEM((cap,), I32), pltpu.SMEM((s,), F32),
                        pltpu.VMEM((cap, d), F32), pltpu.VMEM((n_f, cap, fcw), BF16), pltpu.VMEM((cap, d), F32),
                        pltpu.VMEM((s, d // LANES, LANES), F32),
                        pltpu.SemaphoreType.DMA(()), pltpu.SemaphoreType.DMA(()),
                        pltpu.SemaphoreType.DMA(()), pltpu.SemaphoreType.DMA(())],
        compiler_params=_params(("arbitrary", "arbitrary", "arbitrary")),
        name="moe_experts",
    )(idx, aff, x1, mod, wg, wu, wd)


def _ln2_kernel(x_ref, y_ref, mod_ref, g_ref, b_ref, o_ref):
    m = mod_ref[...]
    y = y_ref[...].reshape(o_ref.shape)
    o_ref[...] = _layer_norm(ALPHA * x_ref[...] + (1.0 + m[5:6]) * y, g_ref[...], b_ref[...])


def _ln2_call(x1, moe, mod, g, b, tm):
    bsz, s, d = x1.shape
    row = pl.BlockSpec((None, tm, d), lambda bb, i: (bb, i, 0))
    vec = pl.BlockSpec((1, d), lambda bb, i: (0, 0))
    return pl.pallas_call(
        _ln2_kernel, grid=(bsz, s // tm),
        in_specs=[row, pl.BlockSpec((None, tm, d // LANES, LANES), lambda bb, i: (bb, i, 0, 0)),
                  pl.BlockSpec((None, 6, d), lambda bb, i: (bb, 0, 0)), vec, vec],
        out_specs=row, out_shape=jax.ShapeDtypeStruct((bsz, s, d), F32),
        compiler_params=_params(("parallel", "parallel")), name="ffn_postnorm",
    )(x1, moe, mod, g.reshape(1, d), b.reshape(1, d))


def _ffn_block(x1, aff, mod, wg, wu, wd, ln_g, ln_b, tm):
    s = x1.shape[1]
    cap = (EC_FACTOR * s) // N_EXPERTS
    idx = _topk_call(aff, cap)
    moe = _moe_call(x1, mod, idx, aff, wg.astype(BF16), wu.astype(BF16), wd.astype(BF16), fcw=256)
    return _ln2_call(x1, moe, mod, ln_g, ln_b, tm)


def kernel(x, c, positions, ada_w, ada_b, ab_w_in, ab_conv_w, ab_a_log, ab_dt_bias, ab_dn_norm, ab_w_out, swa_w_in,
           swa_sinks, swa_w_out, ln_mix_g, ln_mix_b, router_w, moe_w_gate, moe_w_up, moe_w_down, ln_ffn_g, ln_ffn_b):
    bsz, s, d = x.shape
    tm = min(512, s)
    mod = _mod_call(c, ada_w, ada_b).reshape(DEPTH, bsz, 6, d)
    rope = _rope_tables(positions)

    w_in = ab_w_in[0]
    n_a = 4 * DN_WIDTH
    n_g = 4 * DN_HEADS
    gw = DIL_GROUP_WIDTH
    q_scale = HEAD_DIM ** -0.5
    cols = [w_in[:, :n_a], jnp.pad(w_in[:, n_a:n_a + n_g], ((0, 0), (0, LANES - n_g)))]
    groups = [(0, 3 * DN_WIDTH, 0, False), (3 * DN_WIDTH, DN_WIDTH, 1, False), (n_a, LANES, 2, False)]
    for gi in range(len(DIL_PAIRS)):
        for part in range(3):
            c0 = n_a + n_g + part * DIL_WIDTH + gi * gw
            cols.append(w_in[:, c0:c0 + gw] * (q_scale if part == 0 else 1.0))
            groups.append((n_a + LANES + (3 * gi + part) * gw, gw, 3 + 3 * gi + part, part < 2))
    w0 = jnp.concatenate(cols, axis=1).astype(BF16)
    outs0 = _inproj_call(
        x, mod[0], w0, rope, _chunk_plan(groups), (3 * DN_WIDTH, DN_WIDTH, LANES) + (gw,) * 9,
        (F32, F32, F32) + (BF16,) * 9, tm, "inproj_deltanet_dilated")
    qkv_a, z, gates = outs0[:3]
    u, w, qd, kd, at, egl = _dn_prep_call(qkv_a, gates, ab_conv_w[0], ab_a_log[0], ab_dt_bias[0], tm)
    o_scan = _dn_scan_call(u, w, qd, kd, at, egl, min(256, s))
    ogs, lses = [], []
    for gi, (window, dil) in enumerate(DIL_PAIRS):
        o_g, lse_g = _dilated_group(*outs0[3 + 3 * gi:6 + 3 * gi], window, dil)
        ogs.append(o_g)
        lses.append(lse_g)
    x1, aff = _outproj0_call(o_scan, z, ab_dn_norm[0], ogs, lses, ab_w_out[0].astype(BF16), x, mod[0],
                             ln_mix_g[0], ln_mix_b[0], router_w[0], tm)
    x = _ffn_block(x1, aff, mod[0], moe_w_gate[0], moe_w_up[0], moe_w_down[0], ln_ffn_g[0], ln_ffn_b[0], tm)

    qw = SWA_Q_HEADS * HEAD_DIM
    kw = SWA_KV_HEADS * HEAD_DIM
    plan1 = _chunk_plan([(0, qw, 0, True), (qw, kw, 1, True), (qw + kw, kw, 2, False)])
    w1 = jnp.concatenate([swa_w_in[0][:, :qw] * q_scale, swa_w_in[0][:, qw:]], axis=1).astype(BF16)
    q1, k1, v1 = _inproj_call(x, mod[1], w1, rope, plan1, (qw, kw, kw), (BF16, BF16, BF16), tm, "inproj_swa")
    (o1,) = _band_attn_call(q1, k1, v1, n_res=1, n_kv=SWA_KV_HEADS, grp=SWA_Q_HEADS // SWA_KV_HEADS, blk=SWA_WINDOW,
                            sinks=swa_sinks[0], with_lse=False, name="swa_attn")
    x1, aff = _outproj1_call(o1, swa_w_out[0].astype(BF16), x, mod[1], ln_mix_g[1], ln_mix_b[1], router_w[1], tm)
    x = _ffn_block(x1, aff, mod[1], moe_w_gate[1], moe_w_up[1], moe_w_down[1], ln_ffn_g[1], ln_ffn_b[1], tm)
    return x
```

```python
import functools
import math

import jax
import jax.numpy as jnp
from jax import lax
from jax.experimental import pallas as pl
from jax.experimental.pallas import tpu as pltpu

F32 = jnp.float32
BF16 = jnp.bfloat16
I32 = jnp.int32
HI = lax.Precision.HIGHEST

DEPTH = 2
HEAD_DIM = 64
ROT_DIM = HEAD_DIM // 4
ROPE_THETA = 500000.0
DN_HEADS = 4
DN_HEAD_DIM = 128
DN_CHUNK = 64
DN_CONV = 5
DN_WIDTH = DN_HEADS * DN_HEAD_DIM
DIL_PAIRS = ((128, 1), (512, 4), (2048, 16))
DIL_HEADS_PER_GROUP = 4
DIL_GROUP_WIDTH = DIL_HEADS_PER_GROUP * HEAD_DIM
DIL_WIDTH = DIL_GROUP_WIDTH * len(DIL_PAIRS)
SWA_Q_HEADS = 16
SWA_KV_HEADS = 4
SWA_WINDOW = 128
N_EXPERTS = 16
EC_FACTOR = 2
ALPHA = (2.0 * DEPTH) ** 0.25
LN_EPS = 1e-5
NORM_EPS = 1e-6
NEG = -1e30
LANES = 128
HALO = 8
VMEM_LIMIT = 56 * 1024 * 1024


def _dot(a, b, prec=None):
    return jnp.dot(a, b, preferred_element_type=F32, precision=prec)


def _dot_nt(a, b, prec=None):
    return lax.dot_general(a, b, (((1,), (1,)), ((), ())), preferred_element_type=F32, precision=prec)


def _dot_tn(a, b, prec=None):
    return lax.dot_general(a, b, (((0,), (0,)), ((), ())), preferred_element_type=F32, precision=prec)


def _silu(x):
    return x * jax.nn.sigmoid(x)


def _params(sem):
    return pltpu.CompilerParams(dimension_semantics=sem, vmem_limit_bytes=VMEM_LIMIT)


def _mod_kernel(c_ref, w_ref, b_ref, o_ref):
    o_ref[...] = _dot(_silu(c_ref[...]), w_ref[...], HI) + b_ref[...]


def _mod_call(c, ada_w, ada_b):
    depth, d, n6 = ada_w.shape
    bsz = c.shape[0]
    tn = n6 // 4
    return pl.pallas_call(
        _mod_kernel,
        grid=(depth, n6 // tn),
        in_specs=[pl.BlockSpec((bsz, d), lambda i, j: (0, 0)),
                  pl.BlockSpec((None, d, tn), lambda i, j: (i, 0, j)),
                  pl.BlockSpec((None, 1, tn), lambda i, j: (i, 0, j))],
        out_specs=pl.BlockSpec((None, bsz, tn), lambda i, j: (i, 0, j)),
        out_shape=jax.ShapeDtypeStruct((depth, bsz, n6), F32),
        compiler_params=_params(("arbitrary", "arbitrary")),
        name="adaln_mod",
    )(c, ada_w, ada_b.reshape(depth, 1, n6))


def _inproj_kernel(x_ref, mod_ref, w_ref, rope_ref, *out_refs, plan):
    m = mod_ref[...]
    h = (x_ref[...] * (1.0 + m[1:2]) + m[0:1]).astype(BF16)
    half = ROT_DIM // 2
    if any(p[4] for p in plan):
        src = lax.broadcasted_iota(I32, (ROT_DIM, LANES), 0)
        lane = lax.broadcasted_iota(I32, (ROT_DIM, LANES), 1) & (HEAD_DIM - 1)
        pick_cos = jnp.where(lane < ROT_DIM, jnp.where((lane & (half - 1)) == src, 1.0, 0.0), 0.0)
        pick_sin = jnp.where(lane < half, jnp.where(src == lane + half, -1.0, 0.0),
                             jnp.where(lane < ROT_DIM, jnp.where(src == lane, 1.0, 0.0), 0.0))
        cs = rope_ref[...]
        lane1 = lax.broadcasted_iota(I32, (1, LANES), 1) & (HEAD_DIM - 1)
        cos_t = _dot(cs, pick_cos, HI) + jnp.where(lane1 < ROT_DIM, 0.0, 1.0)
        sin_t = _dot(cs, pick_sin, HI)
        sin_a = jnp.where(lane1 < half, sin_t, 0.0)
        sin_b = sin_t - sin_a
    for c0, width, oi, o0, rope in plan:
        acc = _dot(h, w_ref[:, c0:c0 + width])
        if rope:
            reps = width // LANES
            tile = lambda a: jnp.concatenate([a] * reps, axis=1)
            acc = (acc * tile(cos_t) + pltpu.roll(acc, width - half, 1) * tile(sin_a)
                   + pltpu.roll(acc, half, 1) * tile(sin_b))
        out_refs[oi][:, o0:o0 + width] = acc.astype(out_refs[oi].dtype)


def _inproj_call(x, mod, w, rope, plan, out_widths, out_dtypes, tm, name):
    bsz, s, d = x.shape
    n = w.shape[1]
    out_shape = [jax.ShapeDtypeStruct((bsz, s, ow), od) for ow, od in zip(out_widths, out_dtypes)]
    out_specs = [pl.BlockSpec((None, tm, ow), lambda b, i: (b, i, 0)) for ow in out_widths]
    return pl.pallas_call(
        functools.partial(_inproj_kernel, plan=plan),
        grid=(bsz, s // tm),
        in_specs=[pl.BlockSpec((None, tm, d), lambda b, i: (b, i, 0)),
                  pl.BlockSpec((None, 6, d), lambda b, i: (b, 0, 0)),
                  pl.BlockSpec((d, n), lambda b, i: (0, 0)),
                  pl.BlockSpec((None, tm, ROT_DIM), lambda b, i: (b, i, 0))],
        out_specs=out_specs,
        out_shape=out_shape,
        compiler_params=_params(("parallel", "parallel")),
        name=name,
    )(x, mod, w, rope)


def _chunk_plan(groups, chunk=256):
    plan = []
    for c0, width, oi, rope in groups:
        off = 0
        while off < width:
            wd = min(chunk, width - off)
            plan.append((c0 + off, wd, oi, off, rope))
            off += wd
    return tuple(plan)


def _rope_tables(positions):
    inv_freq = jnp.power(ROPE_THETA, -jnp.arange(0, ROT_DIM, 2, dtype=F32) / ROT_DIM)
    ang = positions.astype(F32)[..., None] * inv_freq
    return jnp.concatenate([jnp.cos(ang), jnp.sin(ang)], -1)


def _dn_prep_kernel(xa_ref, top_ref, bot_ref, gt_ref, cw_ref, alog_ref, dtb_ref,
                    u_ref, w_ref, qd_ref, kd_ref, at_ref, egl_ref,
                    q_s, k_s, v_s, gc_s, *, t):
    ch = DN_CHUNK
    dk = DN_HEAD_DIM
    pad = (DN_CONV - 1) // 2
    for grp, dst in enumerate((q_s, k_s, v_s)):
        cols = slice(grp * DN_WIDTH, (grp + 1) * DN_WIDTH)
        xe = jnp.concatenate([top_ref[:, cols], xa_ref[:, cols], bot_ref[:, cols]], axis=0)
        y = jnp.zeros((t, DN_WIDTH), F32)
        for k in range(DN_CONV):
            y = y + xe[HALO - pad + k:HALO - pad + k + t, :] * cw_ref[k:k + 1, cols]
        y = _silu(y)
        if grp < 2:
            scale = dk ** -0.5 if grp == 0 else 1.0
            parts = []
            for h in range(DN_HEADS):
                yh = y[:, h * dk:(h + 1) * dk]
                parts.append(yh * lax.rsqrt(jnp.sum(yh * yh, -1, keepdims=True) + NORM_EPS) * scale)
            y = jnp.concatenate(parts, axis=1)
        dst[...] = y

    g = gt_ref[...]
    lane = lax.broadcasted_iota(I32, (t, LANES), 1)
    z = g + dtb_ref[...]
    softplus = jnp.maximum(z, 0.0) + jnp.log1p(jnp.exp(-jnp.abs(z)))
    dec = -jnp.exp(alog_ref[...]) * softplus
    gv = jnp.where(lane < 2 * DN_HEADS, dec, jnp.where(lane < 4 * DN_HEADS, jax.nn.sigmoid(g), 0.0))
    ri = lax.broadcasted_iota(I32, (t, t), 0)
    ci = lax.broadcasted_iota(I32, (t, t), 1)
    shift = int(math.log2(ch))
    same = (ri >> shift) == (ci >> shift)
    pre = jnp.where(same & (ci <= ri), 1.0, 0.0).astype(F32)
    suf = jnp.where(same & (ci >= ri), 1.0, 0.0).astype(F32)
    gcf = _dot(pre, gv, HI)
    gcb = _dot(suf, gv, HI)
    gc_s[...] = jnp.where(lane < DN_HEADS, gcf, jnp.where(lane < 2 * DN_HEADS, gcb, gv))

    rr = lax.broadcasted_iota(I32, (ch, 2 * ch), 0)
    cc = lax.broadcasted_iota(I32, (ch, 2 * ch), 1)
    fwd = cc < ch
    cj = jnp.where(fwd, cc, cc - ch)
    ahead = jnp.where(fwd, cj - rr, rr - cj)
    incl = ahead <= 0
    strict = ahead < 0
    eye2 = jnp.where(cj == rr, 1.0, 0.0).astype(F32)

    def blockdiag(p):
        return jnp.concatenate([jnp.where(fwd, p, 0.0), jnp.where(fwd, 0.0, p)], axis=0).astype(BF16)

    n_sq = int(math.log2(ch)) - 1
    per_iter = 2 if (t // ch) % 2 == 0 else 1

    def chunk_body(ci, carry):
        units = []
        for sub in range(per_iter):
            c = ci * per_iter + sub
            rows = pl.ds(pl.multiple_of(c * ch, ch), ch)
            gcc = gc_s[rows, :]
            gct = gcc.T
            for h in range(DN_HEADS):
                units.append((c, rows, gcc, gct, h))

        st, pws = [], []
        for c, rows, gcc, gct, h in units:
            bcast = lambda col, gcc=gcc: jnp.broadcast_to(gcc[:, col:col + 1], (ch, LANES))
            gf, gb = bcast(h), bcast(DN_HEADS + h)
            bf, bb = bcast(2 * DN_HEADS + h), bcast(3 * DN_HEADS + h)
            grow = jnp.concatenate([gct[h:h + 1, :], gct[DN_HEADS + h:DN_HEADS + h + 1, :]], axis=1)
            diff = jnp.where(fwd, gf, gb) - grow
            decay = jnp.where(incl, jnp.exp(jnp.where(incl, diff, 0.0)), 0.0)
            hs = slice(h * dk, (h + 1) * dk)
            k16 = k_s[rows, hs].astype(BF16)
            kk = jnp.concatenate([k16, k16], axis=0)
            lower = jnp.where(strict, jnp.where(fwd, bf, bb) * _dot_nt(k16, kk) * decay, 0.0)
            intra = (_dot_nt(q_s[rows, hs].astype(BF16), kk) * decay).astype(BF16)
            at_ref[0, rows, h * ch:(h + 1) * ch] = intra[:, :ch]
            at_ref[1, rows, h * ch:(h + 1) * ch] = intra[:, ch:]
            st.append((gf, gb, bf, bb))
            pws.append(-lower)
        ainvs = [eye2 + p for p in pws]
        pbds = [blockdiag(p) for p in pws]
        for _ in range(n_sq):
            pws = [_dot(p.astype(BF16), bd) for p, bd in zip(pws, pbds)]
            pbds = [blockdiag(p) for p in pws]
            ainvs = [a + _dot(a.astype(BF16), bd) for a, bd in zip(ainvs, pbds)]

        egl_f, egl_b = [], []
        for (c, rows, _, _, h), (gf, gb, bf, bb), ainv in zip(units, st, ainvs):
            hs = slice(h * dk, (h + 1) * dk)
            q = q_s[rows, hs]
            k = k_s[rows, hs]
            v = v_s[rows, hs]
            egf, egb = jnp.exp(gf), jnp.exp(gb)
            rhs = jnp.concatenate([jnp.concatenate([v * bf, k * bf * egf], axis=1),
                                   jnp.concatenate([v * bb, k * bb * egb], axis=1)], axis=0).astype(BF16)
            uw_f = _dot(jnp.where(fwd, ainv, 0.0).astype(BF16), rhs)
            uw_b = _dot(jnp.where(fwd, 0.0, ainv).astype(BF16), rhs)
            glf = jnp.broadcast_to(gf[ch - 1:ch, :], (ch, LANES))
            glb = jnp.broadcast_to(gb[0:1, :], (ch, LANES))
            u_ref[0, rows, hs] = uw_f[:, :dk]
            u_ref[1, rows, hs] = uw_b[:, :dk]
            w_ref[0, rows, hs] = uw_f[:, dk:].astype(BF16)
            w_ref[1, rows, hs] = uw_b[:, dk:].astype(BF16)
            qd_ref[0, rows, hs] = (q * egf).astype(BF16)
            qd_ref[1, rows, hs] = (q * egb).astype(BF16)
            kd_ref[0, rows, hs] = (k * jnp.exp(glf - gf)).astype(BF16)
            kd_ref[1, rows, hs] = (k * jnp.exp(glb - gb)).astype(BF16)
            egl_f.append(jnp.exp(glf[0:1, :]))
            egl_b.append(jnp.exp(glb[0:1, :]))
            if h == DN_HEADS - 1:
                fill = [jnp.zeros((8 - DN_HEADS, LANES), F32)]
                egl_ref[0, c] = jnp.concatenate(egl_f + fill, axis=0)
                egl_ref[1, c] = jnp.concatenate(egl_b + fill, axis=0)
                egl_f, egl_b = [], []
        return carry

    lax.fori_loop(0, t // (ch * per_iter), chunk_body, 0)


def _dn_prep_call(qkv_a, gates, conv_w, a_log, dt_bias, t):
    bsz, s, cw = qkv_a.shape
    nt = s // t
    r = qkv_a.reshape(bsz, nt, t, cw)
    zero = jnp.zeros((bsz, 1, HALO, cw), F32)
    top = jnp.concatenate([zero, r[:, :-1, t - HALO:]], axis=1)
    bot = jnp.concatenate([r[:, 1:, :HALO], zero], axis=1)
    cwp = jnp.zeros((8, cw), F32).at[:DN_CONV].set(conv_w)
    alog = jnp.zeros((1, LANES), F32).at[0, :2 * DN_HEADS].set(a_log.reshape(-1))
    dtb = jnp.zeros((1, LANES), F32).at[0, :2 * DN_HEADS].set(dt_bias.reshape(-1))
    nch = s // DN_CHUNK
    wide = lambda dt, wd: jax.ShapeDtypeStruct((2, bsz, s, wd), dt)
    spec = lambda wd: pl.BlockSpec((2, None, t, wd), lambda b, i: (0, b, i, 0))
    return pl.pallas_call(
        functools.partial(_dn_prep_kernel, t=t),
        grid=(bsz, nt),
        in_specs=[pl.BlockSpec((None, t, cw), lambda b, i: (b, i, 0)),
                  pl.BlockSpec((None, None, HALO, cw), lambda b, i: (b, i, 0, 0)),
                  pl.BlockSpec((None, None, HALO, cw), lambda b, i: (b, i, 0, 0)),
                  pl.BlockSpec((None, t, LANES), lambda b, i: (b, i, 0)),
                  pl.BlockSpec((8, cw), lambda b, i: (0, 0)),
                  pl.BlockSpec((1, LANES), lambda b, i: (0, 0)),
                  pl.BlockSpec((1, LANES), lambda b, i: (0, 0))],
        out_specs=[spec(DN_WIDTH), spec(DN_WIDTH), spec(DN_WIDTH), spec(DN_WIDTH), spec(DN_HEADS * DN_CHUNK),
                   pl.BlockSpec((2, None, t // DN_CHUNK, 8, LANES), lambda b, i: (0, b, i, 0, 0))],
        out_shape=[wide(F32, DN_WIDTH), wide(BF16, DN_WIDTH), wide(BF16, DN_WIDTH), wide(BF16, DN_WIDTH),
                   wide(BF16, DN_HEADS * DN_CHUNK),
                   jax.ShapeDtypeStruct((2, bsz, nch, 8, LANES), F32)],
        scratch_shapes=[pltpu.VMEM((t, DN_WIDTH), F32), pltpu.VMEM((t, DN_WIDTH), F32),
                        pltpu.VMEM((t, DN_WIDTH), F32), pltpu.VMEM((t, LANES), F32)],
        compiler_params=_params(("parallel", "parallel")),
        name="deltanet_prep",
    )(qkv_a, top, bot, gates, cwp, alog, dtb)


def _dn_scan_kernel(u_ref, w_ref, qd_ref, kd_ref, at_ref, egl_ref, o_ref, st_ref, *, nc):
    ch = DN_CHUNK
    dk = DN_HEAD_DIM
    d = pl.program_id(0)

    @pl.when(pl.program_id(2) == 0)
    def _():
        st_ref[...] = jnp.zeros(st_ref.shape, st_ref.dtype)

    heads = range(DN_HEADS)
    hsl = [slice(h * dk, (h + 1) * dk) for h in heads]
    sts = [st_ref[h] for h in heads]
    for j in range(nc):
        cc = j + d * (nc - 1 - 2 * j)
        rows = pl.ds(pl.multiple_of(cc * ch, ch), ch)
        egl = egl_ref[cc]
        sbs = [st.astype(BF16) for st in sts]
        vbs = [(u_ref[rows, hsl[h]] - _dot(w_ref[rows, hsl[h]], sbs[h])).astype(BF16) for h in heads]
        qss = [_dot(qd_ref[rows, hsl[h]], sbs[h]) for h in heads]
        sts = [sts[h] * egl[h:h + 1, :] + _dot_tn(kd_ref[rows, hsl[h]], vbs[h]) for h in heads]
        for h in heads:
            o_ref[rows, hsl[h]] = qss[h] + _dot(at_ref[rows, h * ch:(h + 1) * ch], vbs[h])
    for h in heads:
        st_ref[h] = sts[h]


def _dn_scan_call(u, w, qd, kd, at, egl, tc):
    _, bsz, s, wd = u.shape
    nb = s // tc
    nc = tc // DN_CHUNK

    def blk(d, n):
        return n + d * (nb - 1 - 2 * n)

    spec = lambda width: pl.BlockSpec((None, None, tc, width), lambda d, b, n: (d, b, blk(d, n), 0))
    return pl.pallas_call(
        functools.partial(_dn_scan_kernel, nc=nc),
        grid=(2, bsz, nb),
        in_specs=[spec(wd), spec(wd), spec(wd), spec(wd), spec(DN_HEADS * DN_CHUNK),
                  pl.BlockSpec((None, None, nc, 8, LANES), lambda d, b, n: (d, b, blk(d, n), 0, 0))],
        out_specs=spec(wd),
        out_shape=jax.ShapeDtypeStruct((2, bsz, s, wd), F32),
        scratch_shapes=[pltpu.VMEM((DN_HEADS, DN_HEAD_DIM, DN_HEAD_DIM), F32)],
        compiler_params=_params(("parallel", "parallel", "arbitrary")),
        name="deltanet_scan",
    )(u, w, qd, kd, at, egl)


def _band_attn_kernel(*refs, n_kv, grp, blk, tq, qs, t_len, with_sink, with_lse):
    q_ref, kp_ref, kc_ref, kn_ref, vp_ref, vc_ref, vn_ref = refs[:7]
    pos = 7
    sink_ref = None
    if with_sink:
        sink_ref = refs[pos]
        pos += 1
    o_ref = refs[pos]
    lse_ref = refs[pos + 1] if with_lse else None
    hd = HEAD_DIM
    i0 = pl.program_id(2) * tq
    kcat = jnp.concatenate([kp_ref[...], kc_ref[...], kn_ref[...]], axis=0)
    vcat = jnp.concatenate([vp_ref[...], vc_ref[...], vn_ref[...]], axis=0)
    kwin = qs + 2 * blk
    for sub in range(tq // qs):
        k0 = sub * qs
        rowpos = i0 + k0 + lax.broadcasted_iota(I32, (qs, kwin), 0)
        keypos = i0 - blk + k0 + lax.broadcasted_iota(I32, (qs, kwin), 1)
        mask = (jnp.abs(keypos - rowpos) <= blk) & (keypos >= 0) & (keypos < t_len)
        bias = jnp.where(mask, 0.0, NEG)
        lse = jnp.zeros((qs, LSE_COLS), F32)
        lse_lane = lax.broadcasted_iota(I32, (qs, LSE_COLS), 1)
        n_q = n_kv * grp
        khs = [kcat[k0:k0 + kwin, kv * hd:(kv + 1) * hd] for kv in range(n_kv)]
        vhs = [vcat[k0:k0 + kwin, kv * hd:(kv + 1) * hd] for kv in range(n_kv)]
        outs = []
        for h0 in range(0, n_q, HEADS_PER_STAGE):
            hqs = range(h0, min(h0 + HEADS_PER_STAGE, n_q))
            scs = [_dot_nt(q_ref[k0:k0 + qs, hq * hd:(hq + 1) * hd], khs[hq // grp]) + bias for hq in hqs]
            ps, dens = [], []
            for hq, sc in zip(hqs, scs):
                m = jnp.max(sc, axis=-1, keepdims=True)
                if with_sink:
                    sk = sink_ref[hq]
                    m = jnp.maximum(m, sk)
                p = jnp.exp(sc - m)
                den = jnp.sum(p, axis=-1, keepdims=True)
                if with_sink:
                    den = den + jnp.exp(sk - m)
                ps.append(p.astype(BF16))
                dens.append(den)
                if with_lse:
                    lse = jnp.where(lse_lane == hq, m + jnp.log(den), lse)
            outs += [_dot(p, vhs[hq // grp]) / den for hq, p, den in zip(hqs, ps, dens)]
        o_ref[k0:k0 + qs, :] = jnp.concatenate(outs, axis=1).astype(o_ref.dtype)
        if with_lse:
            lse_ref[k0:k0 + qs, :] = lse


LSE_COLS = 8
HEADS_PER_STAGE = 8


def _band_attn_call(q, k, v, *, n_res, n_kv, grp, blk, sinks, with_lse, name):
    bsz, t_len, _ = q.shape
    tq = min(512, t_len)
    qs = min(128, tq)
    nt = t_len // tq
    ratio = tq // blk
    nblk = t_len // blk
    qw = n_kv * grp * HEAD_DIM
    kw = n_kv * HEAD_DIM
    cur = lambda wd: pl.BlockSpec((None, tq, wd), lambda b, r, i: (b, i, r))
    prev = lambda wd: pl.BlockSpec((None, blk, wd), lambda b, r, i: (b, jnp.maximum(i * ratio - 1, 0), r))
    nxt = lambda wd: pl.BlockSpec((None, blk, wd), lambda b, r, i: (b, jnp.minimum((i + 1) * ratio, nblk - 1), r))
    in_specs = [cur(qw), prev(kw), cur(kw), nxt(kw), prev(kw), cur(kw), nxt(kw)]
    args = [q, k, k, k, v, v, v]
    if sinks is not None:
        in_specs.append(pl.BlockSpec(memory_space=pltpu.SMEM))
        args.append(sinks)
    out_shape = [jax.ShapeDtypeStruct((bsz, t_len, n_res * qw), BF16)]
    out_specs = [cur(qw)]
    if with_lse:
        out_shape.append(jax.ShapeDtypeStruct((bsz, n_res, t_len, LSE_COLS), F32))
        out_specs.append(pl.BlockSpec((None, None, tq, LSE_COLS), lambda b, r, i: (b, r, i, 0)))
    return pl.pallas_call(
        functools.partial(_band_attn_kernel, n_kv=n_kv, grp=grp, blk=blk, tq=tq, qs=qs, t_len=t_len,
                          with_sink=sinks is not None, with_lse=with_lse),
        grid=(bsz, n_res, nt),
        in_specs=in_specs,
        out_specs=out_specs,
        out_shape=out_shape,
        compiler_params=_params(("parallel", "parallel", "parallel")),
        name=name,
    )(*args)


def _dilated_group(qg, kg, vg, window, dil):
    bsz, s, wd = qg.shape
    t_len = s // dil
    view = lambda a: a.reshape(bsz, t_len, dil * wd)
    o, lse = _band_attn_call(view(qg), view(kg), view(vg), n_res=dil, n_kv=DIL_HEADS_PER_GROUP, grp=1,
                             blk=window // (2 * dil), sinks=None, with_lse=True, name=f"dilated_attn_{dil}")
    return o.reshape(bsz, s, wd), jnp.swapaxes(lse, 1, 2).reshape(bsz, s, LSE_COLS)


def _layer_norm(r, g, b):
    mu = jnp.mean(r, -1, keepdims=True)
    var = jnp.mean(jnp.square(r - mu), -1, keepdims=True)
    return (r - mu) * lax.rsqrt(var + LN_EPS) * g + b


def _post_mix(y, x_ref, mod_ref, lng_ref, lnb_ref, rwt_ref, x1_ref, aff_ref):
    m = mod_ref[...]
    x1 = _layer_norm(ALPHA * x_ref[...] + (1.0 + m[2:3]) * y, lng_ref[...], lnb_ref[...])
    x1_ref[...] = x1.reshape(x1_ref.shape)
    h2 = x1 * (1.0 + m[4:5]) + m[3:4]
    logits = _dot_nt(rwt_ref[...], h2, HI)
    e = jnp.exp(logits - jnp.max(logits, axis=0, keepdims=True))
    aff_ref[...] = e / jnp.sum(e, axis=0, keepdims=True)


def _outproj0_kernel(of_ref, ob_ref, z_ref, dnn_ref, og0_ref, og1_ref, og2_ref, l0_ref, l1_ref, l2_ref,
                     w_ref, x_ref, mod_ref, lng_ref, lnb_ref, rwt_ref, x1_ref, aff_ref):
    dk = DN_HEAD_DIM
    od = of_ref[...] + ob_ref[...]
    z = z_ref[...]
    parts = []
    for h in range(DN_HEADS):
        oh = od[:, h * dk:(h + 1) * dk]
        oh = oh * lax.rsqrt(jnp.mean(oh * oh, -1, keepdims=True) + NORM_EPS) * dnn_ref[...]
        parts.append(oh * _silu(z[:, h * dk:(h + 1) * dk]))
    o_dn = jnp.concatenate(parts, axis=1).astype(BF16)
    l0, l1, l2 = l0_ref[...], l1_ref[...], l2_ref[...]
    mx = jnp.maximum(jnp.maximum(l0, l1), l2)
    e0, e1, e2 = jnp.exp(l0 - mx), jnp.exp(l1 - mx), jnp.exp(l2 - mx)
    den = e0 + e1 + e2
    head = lax.broadcasted_iota(I32, (l0.shape[0], DIL_GROUP_WIDTH), 1) >> int(math.log2(HEAD_DIM))

    def per_lane(wt):
        out = jnp.zeros(head.shape, F32)
        for h in range(DIL_HEADS_PER_GROUP):
            out = jnp.where(head == h, wt[:, h:h + 1], out)
        return out

    o_dil = (per_lane(e0 / den) * og0_ref[...].astype(F32) + per_lane(e1 / den) * og1_ref[...].astype(F32)
             + per_lane(e2 / den) * og2_ref[...].astype(F32)).astype(BF16)
    y = _dot(o_dn, w_ref[0:DN_WIDTH, :]) + _dot(o_dil, w_ref[DN_WIDTH:DN_WIDTH + DIL_GROUP_WIDTH, :])
    _post_mix(y, x_ref, mod_ref, lng_ref, lnb_ref, rwt_ref, x1_ref, aff_ref)


def _outproj1_kernel(o_ref, w_ref, x_ref, mod_ref, lng_ref, lnb_ref, rwt_ref, x1_ref, aff_ref):
    y = _dot(o_ref[...], w_ref[...])
    _post_mix(y, x_ref, mod_ref, lng_ref, lnb_ref, rwt_ref, x1_ref, aff_ref)


def _tail_specs(bsz, s, d, tm, n_e):
    row = lambda wd: pl.BlockSpec((None, tm, wd), lambda b, i: (b, i, 0))
    const = lambda shp: pl.BlockSpec(shp, lambda b, i: tuple(0 for _ in shp))
    in_specs = [row(d), pl.BlockSpec((None, 6, d), lambda b, i: (b, 0, 0)), const((1, d)), const((1, d)),
                const((n_e, d))]
    out_specs = [pl.BlockSpec((None, tm, d // LANES, LANES), lambda b, i: (b, i, 0, 0)),
                 pl.BlockSpec((None, n_e, tm), lambda b, i: (b, 0, i))]
    out_shape = [jax.ShapeDtypeStruct((bsz, s, d // LANES, LANES), F32), jax.ShapeDtypeStruct((bsz, n_e, s), F32)]
    return in_specs, out_specs, out_shape


def _outproj0_call(o_scan, z, dn_norm, ogs, lses, w_out, x, mod, ln_g, ln_b, router_w, tm):
    bsz, s, d = x.shape
    n_e = router_w.shape[1]
    tail_in, out_specs, out_shape = _tail_specs(bsz, s, d, tm, n_e)
    row = lambda wd: pl.BlockSpec((None, tm, wd), lambda b, i: (b, i, 0))
    dirspec = lambda dd: pl.BlockSpec((None, None, tm, DN_WIDTH), lambda b, i: (dd, b, i, 0))
    gw = DIL_GROUP_WIDTH
    in_specs = [dirspec(0), dirspec(1), row(DN_WIDTH), pl.BlockSpec((1, DN_HEAD_DIM), lambda b, i: (0, 0)),
                row(gw), row(gw), row(gw), row(LSE_COLS), row(LSE_COLS), row(LSE_COLS),
                pl.BlockSpec(w_out.shape, lambda b, i: (0, 0))] + tail_in
    return pl.pallas_call(
        _outproj0_kernel, grid=(bsz, s // tm), in_specs=in_specs, out_specs=out_specs, out_shape=out_shape,
        compiler_params=_params(("parallel", "parallel")), name="outproj_deltanet_dilated",
    )(o_scan, o_scan, z, dn_norm.reshape(1, -1), *ogs, *lses, w_out, x, mod, ln_g.reshape(1, d), ln_b.reshape(1, d),
      router_w.T)


def _outproj1_call(o, w_out, x, mod, ln_g, ln_b, router_w, tm):
    bsz, s, d = x.shape
    n_e = router_w.shape[1]
    tail_in, out_specs, out_shape = _tail_specs(bsz, s, d, tm, n_e)
    in_specs = [pl.BlockSpec((None, tm, o.shape[-1]), lambda b, i: (b, i, 0)),
                pl.BlockSpec(w_out.shape, lambda b, i: (0, 0))] + tail_in
    return pl.pallas_call(
        _outproj1_kernel, grid=(bsz, s // tm), in_specs=in_specs, out_specs=out_specs, out_shape=out_shape,
        compiler_params=_params(("parallel", "parallel")), name="outproj_swa",
    )(o, w_out, x, mod, ln_g.reshape(1, d), ln_b.reshape(1, d), router_w.T)


def _topk_kernel(a_ref, idx_ref, *, cap, jb):
    a = a_ref[...]
    n_e, rows, _ = a.shape
    bits = pltpu.bitcast(a, I32)
    thr = jnp.zeros((n_e, 1, 1), I32)
    for bit in range(30, -1, -1):
        cand = thr | (1 << bit)
        cnt = jnp.sum(jnp.where(bits >= cand, 1, 0), axis=(1, 2), keepdims=True)
        thr = jnp.where(cnt >= cap, cand, thr)
    gt = jnp.where(bits > thr, 1.0, 0.0).astype(F32)
    eq = jnp.where(bits == thr, 1.0, 0.0).astype(F32)
    need = cap - jnp.sum(gt, axis=(1, 2), keepdims=True)
    ru = lax.broadcasted_iota(I32, (LANES, LANES), 0)
    cu = lax.broadcasted_iota(I32, (LANES, LANES), 1)
    upper = jnp.where(ru <= cu, 1.0, 0.0).astype(BF16)
    rl = lax.broadcasted_iota(I32, (rows, rows), 0)
    cl = lax.broadcasted_iota(I32, (rows, rows), 1)
    lstrict = jnp.where(cl < rl, 1.0, 0.0).astype(BF16)

    def fold_cumsum(x01):
        within = _dot(x01.astype(BF16), upper)
        rowtot = jnp.broadcast_to(within[:, LANES - 1:LANES], within.shape)
        before = _dot(lstrict, rowtot.astype(BF16))
        return within + before, before + rowtot

    rowid = lax.broadcasted_iota(I32, (rows, jb), 0).astype(F32)
    for e in range(n_e):
        eq_before = fold_cumsum(eq[e])[0] - eq[e]
        sel = jnp.maximum(gt[e], jnp.where(eq_before < need[e], eq[e], 0.0))
        count, count_end = fold_cumsum(sel)
        for j0 in range(0, cap, jb):
            slot = (j0 + lax.broadcasted_iota(I32, (1, jb), 1)).astype(F32)
            row = jnp.sum(jnp.where(count_end[:, 0:1] <= slot, 1.0, 0.0), axis=0, keepdims=True)
            onehot = jnp.where(rowid == row, 1.0, 0.0)
            count_row = _dot_tn(count, onehot, HI)
            lane = jnp.sum(jnp.where(count_row <= slot, 1.0, 0.0), axis=0, keepdims=True)
            idx_ref[e:e + 1, j0:j0 + jb] = (row * LANES + lane).astype(I32)


def _topk_call(aff, cap):
    bsz, n_e, s = aff.shape
    rows = s // LANES
    return pl.pallas_call(
        functools.partial(_topk_kernel, cap=cap, jb=min(512, cap)),
        grid=(bsz,),
        in_specs=[pl.BlockSpec((None, n_e, rows, LANES), lambda b: (b, 0, 0, 0))],
        out_specs=pl.BlockSpec((None, n_e, cap), lambda b: (b, 0, 0)),
        out_shape=jax.ShapeDtypeStruct((bsz, n_e, cap), I32),
        compiler_params=_params(("parallel",)), name="topk_route",
    )(aff.reshape(bsz, n_e, rows, LANES))


SUBLANES = 8


def _moe_kernel(idx_hbm, aff_hbm, x_hbm, mod_ref, wg_ref, wu_ref, wd_ref, out_hbm,
                idx_s, aff_s, xg0, xg1, hid_s, y, acc, sem_i, sem_a, sem_g, sem_o, *, cap, n_f, y_rows):
    b = pl.program_id(0)
    e = pl.program_id(1)
    f = pl.program_id(2)
    n_e = pl.num_programs(1)
    per_f = cap // n_f

    def row_copy(tok, j, dst, sem):
        return pltpu.make_async_copy(x_hbm.at[b, tok], dst.at[j], sem)

    def wait_rows(dst, sem):
        pltpu.make_async_copy(x_hbm.at[b, pl.ds(0, cap)], dst, sem).wait()

    @pl.when((e == 0) & (f == 0))
    def _():
        acc[...] = jnp.zeros(acc.shape, acc.dtype)
        ci = pltpu.make_async_copy(idx_hbm.at[b], idx_s, sem_i)
        ci.start()
        ci.wait()

        def gather(jo, carry):
            for r in range(SUBLANES):
                j = jo * SUBLANES + r
                row_copy(idx_s[j], j, xg0, sem_g.at[0]).start(priority=r % 2)
            return carry

        lax.fori_loop(0, cap // SUBLANES, gather, 0)
        wait_rows(xg0, sem_g.at[0])

    aff_copy = pltpu.make_async_copy(aff_hbm.at[b, e], aff_s, sem_a)

    @pl.when(f == 0)
    def _():
        aff_copy.start()

    def expert_step(xg_cur, xg_next, sem_next):
        next_base = jnp.minimum(e + 1, n_e - 1) * cap + f * per_f
        for r in range(per_f):
            row_copy(idx_s[next_base + r], f * per_f + r, xg_next, sem_next).start(priority=r % 2)
        m = mod_ref[...]
        xv = (xg_cur[...].reshape(cap, m.shape[-1]) * (1.0 + m[4:5]) + m[3:4]).astype(BF16)
        hid_s[f] = (_silu(_dot(xv, wg_ref[...])) * _dot(xv, wu_ref[...])).astype(BF16)

        @pl.when(f == n_f - 1)
        def _():
            aff_copy.wait()
            for r0 in range(0, cap, y_rows):
                hid = jnp.concatenate([hid_s[i, r0:r0 + y_rows, :] for i in range(n_f)], axis=1)
                y[...] = _dot(hid, wd_ref[...])

                def scatter(jo, carry):
                    j0 = pl.multiple_of(jo * SUBLANES, SUBLANES)
                    yc = y[pl.ds(j0, SUBLANES), :].reshape((SUBLANES,) + acc.shape[1:])
                    toks = [idx_s[e * cap + r0 + j0 + r] for r in range(SUBLANES)]
                    olds = [acc[toks[r]] for r in range(SUBLANES)]
                    for r in range(SUBLANES):
                        acc[toks[r]] = olds[r] + aff_s[toks[r]] * yc[r]
                    return carry

                lax.fori_loop(0, y_rows // SUBLANES, scatter, 0)
            wait_rows(xg_next, sem_next)

            @pl.when(e == n_e - 1)
            def _():
                co = pltpu.make_async_copy(acc, out_hbm.at[b], sem_o)
                co.start()
                co.wait()

    @pl.when(e % 2 == 0)
    def _():
        expert_step(xg0, xg1, sem_g.at[1])

    @pl.when(e % 2 == 1)
    def _():
        expert_step(xg1, xg0, sem_g.at[0])


def _moe_call(x1, mod, idx, aff, wg, wu, wd, fcw):
    bsz, s, nt, _ = x1.shape
    d = nt * LANES
    n_e, _, ff = wg.shape
    cap = idx.shape[-1]
    fcw = min(fcw, ff)
    n_f = ff // fcw
    y_rows = min(512, cap)
    assert n_e % 2 == 0 and cap % n_f == 0 and cap % y_rows == 0 and d // LANES == SUBLANES
    anyspec = pl.BlockSpec(memory_space=pl.ANY)
    return pl.pallas_call(
        functools.partial(_moe_kernel, cap=cap, n_f=n_f, y_rows=y_rows),
        grid=(bsz, n_e, n_f),
        in_specs=[anyspec, anyspec, anyspec,
                  pl.BlockSpec((None, 6, d), lambda b, e, f: (b, 0, 0)),
                  pl.BlockSpec((None, d, fcw), lambda b, e, f: (e, 0, f)),
                  pl.BlockSpec((None, d, fcw), lambda b, e, f: (e, 0, f)),
                  pl.BlockSpec((None, ff, d), lambda b, e, f: (e, 0, 0))],
        out_specs=anyspec,
        out_shape=jax.ShapeDtypeStruct((bsz, s, d // LANES, LANES), F32),
        scratch_shapes=[pltpu.SMEM((n_e * cap,), I32), pltpu.SMEM((s,), F32),
                        pltpu.VMEM((cap, nt, LANES), F32), pltpu.VMEM((cap, nt, LANES), F32),
                        pltpu.VMEM((n_f, cap, fcw), BF16), pltpu.VMEM((y_rows, d), F32),
                        pltpu.VMEM((s, nt, LANES), F32),
                        pltpu.SemaphoreType.DMA(()), pltpu.SemaphoreType.DMA(()),
                        pltpu.SemaphoreType.DMA((2,)), pltpu.SemaphoreType.DMA(())],
        compiler_params=_params(("arbitrary", "arbitrary", "arbitrary")),
        name="moe_experts",
    )(idx.reshape(bsz, n_e * cap), aff, x1, mod, wg, wu, wd)


def _ln2_kernel(x_ref, y_ref, g2_ref, g_ref, b_ref, o_ref):
    r = ALPHA * x_ref[...] + (1.0 + g2_ref[...]) * y_ref[...]
    o_ref[...] = _layer_norm(r.reshape(o_ref.shape), g_ref[...], b_ref[...])


def _ln2_call(x1, moe, mod, g, b, tm):
    bsz, s, nt, _ = x1.shape
    d = nt * LANES
    row = pl.BlockSpec((None, tm, d), lambda bb, i: (bb, i, 0))
    tiles = pl.BlockSpec((None, tm, nt, LANES), lambda bb, i: (bb, i, 0, 0))
    vec = pl.BlockSpec((1, d), lambda bb, i: (0, 0))
    return pl.pallas_call(
        _ln2_kernel, grid=(bsz, s // tm),
        in_specs=[tiles, tiles, pl.BlockSpec((None, 1, nt, LANES), lambda bb, i: (bb, 0, 0, 0)), vec, vec],
        out_specs=row, out_shape=jax.ShapeDtypeStruct((bsz, s, d), F32),
        compiler_params=_params(("parallel", "parallel")), name="ffn_postnorm",
    )(x1, moe, mod[:, 5].reshape(bsz, 1, nt, LANES), g.reshape(1, d), b.reshape(1, d))


def _ffn_block(x1, aff, mod, wg, wu, wd, ln_g, ln_b, tm):
    s = x1.shape[1]
    cap = (EC_FACTOR * s) // N_EXPERTS
    idx = _topk_call(aff, cap)
    moe = _moe_call(x1, mod, idx, aff, wg.astype(BF16), wu.astype(BF16), wd.astype(BF16), fcw=256)
    return _ln2_call(x1, moe, mod, ln_g, ln_b, tm)


def kernel(x, c, positions, ada_w, ada_b, ab_w_in, ab_conv_w, ab_a_log, ab_dt_bias, ab_dn_norm, ab_w_out, swa_w_in,
           swa_sinks, swa_w_out, ln_mix_g, ln_mix_b, router_w, moe_w_gate, moe_w_up, moe_w_down, ln_ffn_g, ln_ffn_b):
    bsz, s, d = x.shape
    tm = min(512, s)
    mod = _mod_call(c, ada_w, ada_b).reshape(DEPTH, bsz, 6, d)
    rope = _rope_tables(positions)

    w_in = ab_w_in[0]
    n_a = 4 * DN_WIDTH
    n_g = 4 * DN_HEADS
    gw = DIL_GROUP_WIDTH
    q_scale = HEAD_DIM ** -0.5
    cols = [w_in[:, :n_a], jnp.pad(w_in[:, n_a:n_a + n_g], ((0, 0), (0, LANES - n_g)))]
    groups = [(0, 3 * DN_WIDTH, 0, False), (3 * DN_WIDTH, DN_WIDTH, 1, False), (n_a, LANES, 2, False)]
    for gi in range(len(DIL_PAIRS)):
        for part in range(3):
            c0 = n_a + n_g + part * DIL_WIDTH + gi * gw
            cols.append(w_in[:, c0:c0 + gw] * (q_scale if part == 0 else 1.0))
            groups.append((n_a + LANES + (3 * gi + part) * gw, gw, 3 + 3 * gi + part, part < 2))
    w0 = jnp.concatenate(cols, axis=1).astype(BF16)
    outs0 = _inproj_call(
        x, mod[0], w0, rope, _chunk_plan(groups), (3 * DN_WIDTH, DN_WIDTH, LANES) + (gw,) * 9,
        (F32, F32, F32) + (BF16,) * 9, tm, "inproj_deltanet_dilated")
    qkv_a, z, gates = outs0[:3]
    u, w, qd, kd, at, egl = _dn_prep_call(qkv_a, gates, ab_conv_w[0], ab_a_log[0], ab_dt_bias[0], tm)
    o_scan = _dn_scan_call(u, w, qd, kd, at, egl, min(256, s))
    ogs, lses = [], []
    for gi, (window, dil) in enumerate(DIL_PAIRS):
        o_g, lse_g = _dilated_group(*outs0[3 + 3 * gi:6 + 3 * gi], window, dil)
        ogs.append(o_g)
        lses.append(lse_g)
    x1, aff = _outproj0_call(o_scan, z, ab_dn_norm[0], ogs, lses, ab_w_out[0].astype(BF16), x, mod[0],
                             ln_mix_g[0], ln_mix_b[0], router_w[0], tm)
    x = _ffn_block(x1, aff, mod[0], moe_w_gate[0], moe_w_up[0], moe_w_down[0], ln_ffn_g[0], ln_ffn_b[0], tm)

    qw = SWA_Q_HEADS * HEAD_DIM
    kw = SWA_KV_HEADS * HEAD_DIM
    plan1 = _chunk_plan([(0, qw, 0, True), (qw, kw, 1, True), (qw + kw, kw, 2, False)])
    w1 = jnp.concatenate([swa_w_in[0][:, :qw] * q_scale, swa_w_in[0][:, qw:]], axis=1).astype(BF16)
    q1, k1, v1 = _inproj_call(x, mod[1], w1, rope, plan1, (qw, kw, kw), (BF16, BF16, BF16), tm, "inproj_swa")
    (o1,) = _band_attn_call(q1, k1, v1, n_res=1, n_kv=SWA_KV_HEADS, grp=SWA_Q_HEADS // SWA_KV_HEADS, blk=SWA_WINDOW,
                            sinks=swa_sinks[0], with_lse=False, name="swa_attn")
    x1, aff = _outproj1_call(o1, swa_w_out[0].astype(BF16), x, mod[1], ln_mix_g[1], ln_mix_b[1], router_w[1], tm)
    x = _ffn_block(x1, aff, mod[1], moe_w_gate[1], moe_w_up[1], moe_w_down[1], ln_ffn_g[1], ln_ffn_b[1], tm)
    return x
```

```python
import functools
import math

import jax
import jax.numpy as jnp
from jax import lax
from jax.experimental import pallas as pl
from jax.experimental.pallas import tpu as pltpu

F32 = jnp.float32
BF16 = jnp.bfloat16
I32 = jnp.int32
HI = lax.Precision.HIGHEST

DEPTH = 2
HEAD_DIM = 64
ROT_DIM = HEAD_DIM // 4
ROPE_THETA = 500000.0
DN_HEADS = 4
DN_HEAD_DIM = 128
DN_CHUNK = 64
DN_CONV = 5
DN_WIDTH = DN_HEADS * DN_HEAD_DIM
DIL_PAIRS = ((128, 1), (512, 4), (2048, 16))
DIL_HEADS_PER_GROUP = 4
DIL_GROUP_WIDTH = DIL_HEADS_PER_GROUP * HEAD_DIM
DIL_WIDTH = DIL_GROUP_WIDTH * len(DIL_PAIRS)
SWA_Q_HEADS = 16
SWA_KV_HEADS = 4
SWA_WINDOW = 128
N_EXPERTS = 16
EC_FACTOR = 2
ALPHA = (2.0 * DEPTH) ** 0.25
LN_EPS = 1e-5
NORM_EPS = 1e-6
NEG = -1e30
LANES = 128
HALO = 8
VMEM_LIMIT = 56 * 1024 * 1024


def _dot(a, b, prec=None):
    return jnp.dot(a, b, preferred_element_type=F32, precision=prec)


def _dot_nt(a, b, prec=None):
    return lax.dot_general(a, b, (((1,), (1,)), ((), ())), preferred_element_type=F32, precision=prec)


def _dot_tn(a, b, prec=None):
    return lax.dot_general(a, b, (((0,), (0,)), ((), ())), preferred_element_type=F32, precision=prec)


def _silu(x):
    return x * jax.nn.sigmoid(x)


def _params(sem):
    return pltpu.CompilerParams(dimension_semantics=sem, vmem_limit_bytes=VMEM_LIMIT)


def _mod_kernel(c_ref, w_ref, b_ref, o_ref):
    o_ref[...] = _dot(_silu(c_ref[...]), w_ref[...], HI) + b_ref[...]


def _mod_call(c, ada_w, ada_b):
    depth, d, n6 = ada_w.shape
    bsz = c.shape[0]
    tn = n6 // 4
    return pl.pallas_call(
        _mod_kernel,
        grid=(depth, n6 // tn),
        in_specs=[pl.BlockSpec((bsz, d), lambda i, j: (0, 0)),
                  pl.BlockSpec((None, d, tn), lambda i, j: (i, 0, j)),
                  pl.BlockSpec((None, 1, tn), lambda i, j: (i, 0, j))],
        out_specs=pl.BlockSpec((None, bsz, tn), lambda i, j: (i, 0, j)),
        out_shape=jax.ShapeDtypeStruct((depth, bsz, n6), F32),
        compiler_params=_params(("arbitrary", "arbitrary")),
        name="adaln_mod",
    )(c, ada_w, ada_b.reshape(depth, 1, n6))


def _inproj_kernel(x_ref, mod_ref, w_ref, rope_ref, *refs, plan):
    *out_refs, regroup_ref = refs
    m = mod_ref[...]
    h = (x_ref[...] * (1.0 + m[1:2]) + m[0:1]).astype(BF16)
    half = ROT_DIM // 2
    if any(p[4] for p in plan):
        src = lax.broadcasted_iota(I32, (ROT_DIM, LANES), 0)
        lane = lax.broadcasted_iota(I32, (ROT_DIM, LANES), 1) & (HEAD_DIM - 1)
        pick_cos = jnp.where(lane < ROT_DIM, jnp.where((lane & (half - 1)) == src, 1.0, 0.0), 0.0)
        pick_sin = jnp.where(lane < half, jnp.where(src == lane + half, -1.0, 0.0),
                             jnp.where(lane < ROT_DIM, jnp.where(src == lane, 1.0, 0.0), 0.0))
        cs = rope_ref[...]
        lane1 = lax.broadcasted_iota(I32, (1, LANES), 1) & (HEAD_DIM - 1)
        cos_t = _dot(cs, pick_cos, HI) + jnp.where(lane1 < ROT_DIM, 0.0, 1.0)
        sin_t = _dot(cs, pick_sin, HI)
        sin_a = jnp.where(lane1 < half, sin_t, 0.0)
        sin_b = sin_t - sin_a
    for c0, width, oi, o0, rope in plan:
        acc = _dot(h, w_ref[:, c0:c0 + width])
        if rope:
            reps = width // LANES
            tile = lambda a: jnp.concatenate([a] * reps, axis=1)
            acc = (acc * tile(cos_t) + pltpu.roll(acc, width - half, 1) * tile(sin_a)
                   + pltpu.roll(acc, half, 1) * tile(sin_b))
        out = out_refs[oi]
        if len(out.shape) == 2:
            out[:, o0:o0 + width] = acc.astype(out.dtype)
        else:
            dil, n = out.shape[0], out.shape[1]
            for c in range(width // LANES):
                regroup_ref[c] = acc[:, c * LANES:(c + 1) * LANES]
            for r in range(dil):
                out[r] = jnp.concatenate([regroup_ref[c, pl.ds(r, n, stride=dil), :] for c in range(width // LANES)],
                                         axis=1).astype(out.dtype)


def _inproj_call(x, mod, w, rope, plan, out_widths, out_dtypes, out_dils, tm, name):
    bsz, s, d = x.shape
    n = w.shape[1]
    out_shape, out_specs = [], []
    for ow, od, dil in zip(out_widths, out_dtypes, out_dils):
        if dil == 1:
            out_shape.append(jax.ShapeDtypeStruct((bsz, s, ow), od))
            out_specs.append(pl.BlockSpec((None, tm, ow), lambda b, i: (b, i, 0)))
        else:
            out_shape.append(jax.ShapeDtypeStruct((bsz, dil, s // dil, ow), od))
            out_specs.append(pl.BlockSpec((None, dil, tm // dil, ow), lambda b, i: (b, 0, i, 0)))
    chunk_w = max(p[1] for p in plan)
    return pl.pallas_call(
        functools.partial(_inproj_kernel, plan=plan),
        grid=(bsz, s // tm),
        in_specs=[pl.BlockSpec((None, tm, d), lambda b, i: (b, i, 0)),
                  pl.BlockSpec((None, 6, d), lambda b, i: (b, 0, 0)),
                  pl.BlockSpec((d, n), lambda b, i: (0, 0)),
                  pl.BlockSpec((None, tm, ROT_DIM), lambda b, i: (b, i, 0))],
        out_specs=out_specs,
        out_shape=out_shape,
        scratch_shapes=[pltpu.VMEM((chunk_w // LANES, tm, LANES), F32)],
        compiler_params=_params(("parallel", "parallel")),
        name=name,
    )(x, mod, w, rope)


def _chunk_plan(groups, chunk=256):
    plan = []
    for c0, width, oi, rope in groups:
        off = 0
        while off < width:
            wd = min(chunk, width - off)
            plan.append((c0 + off, wd, oi, off, rope))
            off += wd
    return tuple(plan)


def _rope_tables(positions):
    inv_freq = jnp.power(ROPE_THETA, -jnp.arange(0, ROT_DIM, 2, dtype=F32) / ROT_DIM)
    ang = positions.astype(F32)[..., None] * inv_freq
    return jnp.concatenate([jnp.cos(ang), jnp.sin(ang)], -1)


def _dn_prep_kernel(xa_ref, top_ref, bot_ref, gt_ref, cw_ref, alog_ref, dtb_ref,
                    u_ref, w_ref, qd_ref, kd_ref, at_ref, egl_ref,
                    q_s, k_s, v_s, gc_s, *, t):
    ch = DN_CHUNK
    dk = DN_HEAD_DIM
    pad = (DN_CONV - 1) // 2
    for grp, dst in enumerate((q_s, k_s, v_s)):
        cols = slice(grp * DN_WIDTH, (grp + 1) * DN_WIDTH)
        xe = jnp.concatenate([top_ref[:, cols], xa_ref[:, cols], bot_ref[:, cols]], axis=0)
        y = jnp.zeros((t, DN_WIDTH), F32)
        for k in range(DN_CONV):
            y = y + xe[HALO - pad + k:HALO - pad + k + t, :] * cw_ref[k:k + 1, cols]
        y = _silu(y)
        if grp < 2:
            scale = dk ** -0.5 if grp == 0 else 1.0
            parts = []
            for h in range(DN_HEADS):
                yh = y[:, h * dk:(h + 1) * dk]
                parts.append(yh * lax.rsqrt(jnp.sum(yh * yh, -1, keepdims=True) + NORM_EPS) * scale)
            y = jnp.concatenate(parts, axis=1)
        dst[...] = y

    g = gt_ref[...]
    lane = lax.broadcasted_iota(I32, (t, LANES), 1)
    z = g + dtb_ref[...]
    softplus = jnp.maximum(z, 0.0) + jnp.log1p(jnp.exp(-jnp.abs(z)))
    dec = -jnp.exp(alog_ref[...]) * softplus
    gv = jnp.where(lane < 2 * DN_HEADS, dec, jnp.where(lane < 4 * DN_HEADS, jax.nn.sigmoid(g), 0.0))
    ri = lax.broadcasted_iota(I32, (t, t), 0)
    ci = lax.broadcasted_iota(I32, (t, t), 1)
    shift = int(math.log2(ch))
    same = (ri >> shift) == (ci >> shift)
    pre = jnp.where(same & (ci <= ri), 1.0, 0.0).astype(F32)
    suf = jnp.where(same & (ci >= ri), 1.0, 0.0).astype(F32)
    gcf = _dot(pre, gv, HI)
    gcb = _dot(suf, gv, HI)
    gc_s[...] = jnp.where(lane < DN_HEADS, gcf, jnp.where(lane < 2 * DN_HEADS, gcb, gv))

    rr = lax.broadcasted_iota(I32, (ch, 2 * ch), 0)
    cc = lax.broadcasted_iota(I32, (ch, 2 * ch), 1)
    fwd = cc < ch
    cj = jnp.where(fwd, cc, cc - ch)
    ahead = jnp.where(fwd, cj - rr, rr - cj)
    incl = ahead <= 0
    strict = ahead < 0
    eye2 = jnp.where(cj == rr, 1.0, 0.0).astype(F32)

    def blockdiag(p):
        return jnp.concatenate([jnp.where(fwd, p, 0.0), jnp.where(fwd, 0.0, p)], axis=0).astype(BF16)

    n_sq = int(math.log2(ch)) - 1
    per_iter = 2 if (t // ch) % 2 == 0 else 1

    def chunk_body(ci, carry):
        units = []
        for sub in range(per_iter):
            c = ci * per_iter + sub
            rows = pl.ds(pl.multiple_of(c * ch, ch), ch)
            gcc = gc_s[rows, :]
            gct = gcc.T
            for h in range(DN_HEADS):
                units.append((c, rows, gcc, gct, h))

        st, pws = [], []
        for c, rows, gcc, gct, h in units:
            bcast = lambda col, gcc=gcc: jnp.broadcast_to(gcc[:, col:col + 1], (ch, LANES))
            gf, gb = bcast(h), bcast(DN_HEADS + h)
            bf, bb = bcast(2 * DN_HEADS + h), bcast(3 * DN_HEADS + h)
            grow = jnp.concatenate([gct[h:h + 1, :], gct[DN_HEADS + h:DN_HEADS + h + 1, :]], axis=1)
            diff = jnp.where(fwd, gf, gb) - grow
            decay = jnp.where(incl, jnp.exp(jnp.where(incl, diff, 0.0)), 0.0)
            hs = slice(h * dk, (h + 1) * dk)
            k16 = k_s[rows, hs].astype(BF16)
            kk = jnp.concatenate([k16, k16], axis=0)
            lower = jnp.where(strict, jnp.where(fwd, bf, bb) * _dot_nt(k16, kk) * decay, 0.0)
            intra = (_dot_nt(q_s[rows, hs].astype(BF16), kk) * decay).astype(BF16)
            at_ref[0, rows, h * ch:(h + 1) * ch] = intra[:, :ch]
            at_ref[1, rows, h * ch:(h + 1) * ch] = intra[:, ch:]
            st.append((gf, gb, bf, bb))
            pws.append(-lower)
        ainvs = [eye2 + p for p in pws]
        pbds = [blockdiag(p) for p in pws]
        for _ in range(n_sq):
            pws = [_dot(p.astype(BF16), bd) for p, bd in zip(pws, pbds)]
            pbds = [blockdiag(p) for p in pws]
            ainvs = [a + _dot(a.astype(BF16), bd) for a, bd in zip(ainvs, pbds)]

        egl_f, egl_b = [], []
        for (c, rows, _, _, h), (gf, gb, bf, bb), ainv in zip(units, st, ainvs):
            hs = slice(h * dk, (h + 1) * dk)
            q = q_s[rows, hs]
            k = k_s[rows, hs]
            v = v_s[rows, hs]
            egf, egb = jnp.exp(gf), jnp.exp(gb)
            rhs = jnp.concatenate([jnp.concatenate([v * bf, k * bf * egf], axis=1),
                                   jnp.concatenate([v * bb, k * bb * egb], axis=1)], axis=0).astype(BF16)
            uw_f = _dot(jnp.where(fwd, ainv, 0.0).astype(BF16), rhs)
            uw_b = _dot(jnp.where(fwd, 0.0, ainv).astype(BF16), rhs)
            glf = jnp.broadcast_to(gf[ch - 1:ch, :], (ch, LANES))
            glb = jnp.broadcast_to(gb[0:1, :], (ch, LANES))
            u_ref[0, rows, hs] = uw_f[:, :dk]
            u_ref[1, rows, hs] = uw_b[:, :dk]
            w_ref[0, rows, hs] = uw_f[:, dk:].astype(BF16)
            w_ref[1, rows, hs] = uw_b[:, dk:].astype(BF16)
            qd_ref[0, rows, hs] = (q * egf).astype(BF16)
            qd_ref[1, rows, hs] = (q * egb).astype(BF16)
            kd_ref[0, rows, hs] = (k * jnp.exp(glf - gf)).astype(BF16)
            kd_ref[1, rows, hs] = (k * jnp.exp(glb - gb)).astype(BF16)
            egl_f.append(jnp.exp(glf[0:1, :]))
            egl_b.append(jnp.exp(glb[0:1, :]))
            if h == DN_HEADS - 1:
                fill = [jnp.zeros((8 - DN_HEADS, LANES), F32)]
                egl_ref[0, c] = jnp.concatenate(egl_f + fill, axis=0)
                egl_ref[1, c] = jnp.concatenate(egl_b + fill, axis=0)
                egl_f, egl_b = [], []
        return carry

    lax.fori_loop(0, t // (ch * per_iter), chunk_body, 0)


def _dn_prep_call(qkv_a, gates, conv_w, a_log, dt_bias, t):
    bsz, s, cw = qkv_a.shape
    nt = s // t
    r = qkv_a.reshape(bsz, nt, t, cw)
    zero = jnp.zeros((bsz, 1, HALO, cw), F32)
    top = jnp.concatenate([zero, r[:, :-1, t - HALO:]], axis=1)
    bot = jnp.concatenate([r[:, 1:, :HALO], zero], axis=1)
    cwp = jnp.zeros((8, cw), F32).at[:DN_CONV].set(conv_w)
    alog = jnp.zeros((1, LANES), F32).at[0, :2 * DN_HEADS].set(a_log.reshape(-1))
    dtb = jnp.zeros((1, LANES), F32).at[0, :2 * DN_HEADS].set(dt_bias.reshape(-1))
    nch = s // DN_CHUNK
    wide = lambda dt, wd: jax.ShapeDtypeStruct((2, bsz, s, wd), dt)
    spec = lambda wd: pl.BlockSpec((2, None, t, wd), lambda b, i: (0, b, i, 0))
    return pl.pallas_call(
        functools.partial(_dn_prep_kernel, t=t),
        grid=(bsz, nt),
        in_specs=[pl.BlockSpec((None, t, cw), lambda b, i: (b, i, 0)),
                  pl.BlockSpec((None, None, HALO, cw), lambda b, i: (b, i, 0, 0)),
                  pl.BlockSpec((None, None, HALO, cw), lambda b, i: (b, i, 0, 0)),
                  pl.BlockSpec((None, t, LANES), lambda b, i: (b, i, 0)),
                  pl.BlockSpec((8, cw), lambda b, i: (0, 0)),
                  pl.BlockSpec((1, LANES), lambda b, i: (0, 0)),
                  pl.BlockSpec((1, LANES), lambda b, i: (0, 0))],
        out_specs=[spec(DN_WIDTH), spec(DN_WIDTH), spec(DN_WIDTH), spec(DN_WIDTH), spec(DN_HEADS * DN_CHUNK),
                   pl.BlockSpec((2, None, t // DN_CHUNK, 8, LANES), lambda b, i: (0, b, i, 0, 0))],
        out_shape=[wide(F32, DN_WIDTH), wide(BF16, DN_WIDTH), wide(BF16, DN_WIDTH), wide(BF16, DN_WIDTH),
                   wide(BF16, DN_HEADS * DN_CHUNK),
                   jax.ShapeDtypeStruct((2, bsz, nch, 8, LANES), F32)],
        scratch_shapes=[pltpu.VMEM((t, DN_WIDTH), F32), pltpu.VMEM((t, DN_WIDTH), F32),
                        pltpu.VMEM((t, DN_WIDTH), F32), pltpu.VMEM((t, LANES), F32)],
        compiler_params=_params(("parallel", "parallel")),
        name="deltanet_prep",
    )(qkv_a, top, bot, gates, cwp, alog, dtb)


def _dn_scan_kernel(u_ref, w_ref, qd_ref, kd_ref, at_ref, egl_ref, o_ref, st_ref, *, nc):
    ch = DN_CHUNK
    dk = DN_HEAD_DIM
    d = pl.program_id(0)

    @pl.when(pl.program_id(2) == 0)
    def _():
        st_ref[...] = jnp.zeros(st_ref.shape, st_ref.dtype)

    heads = range(DN_HEADS)
    hsl = [slice(h * dk, (h + 1) * dk) for h in heads]
    sts = [st_ref[h] for h in heads]
    for j in range(nc):
        cc = j + d * (nc - 1 - 2 * j)
        rows = pl.ds(pl.multiple_of(cc * ch, ch), ch)
        egl = egl_ref[cc]
        sbs = [st.astype(BF16) for st in sts]
        vbs = [(u_ref[rows, hsl[h]] - _dot(w_ref[rows, hsl[h]], sbs[h])).astype(BF16) for h in heads]
        qss = [_dot(qd_ref[rows, hsl[h]], sbs[h]) for h in heads]
        sts = [sts[h] * egl[h:h + 1, :] + _dot_tn(kd_ref[rows, hsl[h]], vbs[h]) for h in heads]
        for h in heads:
            o_ref[rows, hsl[h]] = qss[h] + _dot(at_ref[rows, h * ch:(h + 1) * ch], vbs[h])
    for h in heads:
        st_ref[h] = sts[h]


def _dn_scan_call(u, w, qd, kd, at, egl, tc):
    _, bsz, s, wd = u.shape
    nb = s // tc
    nc = tc // DN_CHUNK

    def blk(d, n):
        return n + d * (nb - 1 - 2 * n)

    spec = lambda width: pl.BlockSpec((None, None, tc, width), lambda d, b, n: (d, b, blk(d, n), 0))
    return pl.pallas_call(
        functools.partial(_dn_scan_kernel, nc=nc),
        grid=(2, bsz, nb),
        in_specs=[spec(wd), spec(wd), spec(wd), spec(wd), spec(DN_HEADS * DN_CHUNK),
                  pl.BlockSpec((None, None, nc, 8, LANES), lambda d, b, n: (d, b, blk(d, n), 0, 0))],
        out_specs=spec(wd),
        out_shape=jax.ShapeDtypeStruct((2, bsz, s, wd), F32),
        scratch_shapes=[pltpu.VMEM((DN_HEADS, DN_HEAD_DIM, DN_HEAD_DIM), F32)],
        compiler_params=_params(("parallel", "parallel", "arbitrary")),
        name="deltanet_scan",
    )(u, w, qd, kd, at, egl)


def _band_attn_kernel(*refs, n_kv, grp, blk, tq, qs, t_len, with_sink, with_lse):
    q_ref, kp_ref, kc_ref, kn_ref, vp_ref, vc_ref, vn_ref = refs[:7]
    pos = 7
    sink_ref = None
    if with_sink:
        sink_ref = refs[pos]
        pos += 1
    o_ref = refs[pos]
    lse_ref = refs[pos + 1] if with_lse else None
    hd = HEAD_DIM
    i0 = pl.program_id(2) * tq
    kcat = jnp.concatenate([kp_ref[...], kc_ref[...], kn_ref[...]], axis=0)
    vcat = jnp.concatenate([vp_ref[...], vc_ref[...], vn_ref[...]], axis=0)
    kwin = qs + 2 * blk
    n_q = n_kv * grp
    n_sub = tq // qs
    lse_lane = lax.broadcasted_iota(I32, (qs, LSE_COLS), 1)
    lses = [jnp.zeros((qs, LSE_COLS), F32) for _ in range(n_sub)]
    outs = [[] for _ in range(n_sub)]
    biases, khs, vhs = [], [], []
    for sub in range(n_sub):
        k0 = sub * qs
        rowpos = i0 + k0 + lax.broadcasted_iota(I32, (qs, kwin), 0)
        keypos = i0 - blk + k0 + lax.broadcasted_iota(I32, (qs, kwin), 1)
        mask = (jnp.abs(keypos - rowpos) <= blk) & (keypos >= 0) & (keypos < t_len)
        biases.append(jnp.where(mask, 0.0, NEG))
        khs.append([kcat[k0:k0 + kwin, kv * hd:(kv + 1) * hd] for kv in range(n_kv)])
        vhs.append([vcat[k0:k0 + kwin, kv * hd:(kv + 1) * hd] for kv in range(n_kv)])
    units = [(sub, hq) for sub in range(n_sub) for hq in range(n_q)]
    for u0 in range(0, len(units), ATTN_UNITS_PER_STAGE):
        stage = units[u0:u0 + ATTN_UNITS_PER_STAGE]
        scs = [_dot_nt(q_ref[sub * qs:(sub + 1) * qs, hq * hd:(hq + 1) * hd], khs[sub][hq // grp]) + biases[sub]
               for sub, hq in stage]
        ps, dens = [], []
        for (sub, hq), sc in zip(stage, scs):
            m = jnp.max(sc, axis=-1, keepdims=True)
            if with_sink:
                sk = sink_ref[hq]
                m = jnp.maximum(m, sk)
            p = jnp.exp(sc - m)
            den = jnp.sum(p, axis=-1, keepdims=True)
            if with_sink:
                den = den + jnp.exp(sk - m)
            ps.append(p.astype(BF16))
            dens.append(den)
            if with_lse:
                lses[sub] = jnp.where(lse_lane == hq, m + jnp.log(den), lses[sub])
        for (sub, hq), p, den in zip(stage, ps, dens):
            outs[sub].append(_dot(p, vhs[sub][hq // grp]) / den)
    for sub in range(n_sub):
        o_ref[sub * qs:(sub + 1) * qs, :] = jnp.concatenate(outs[sub], axis=1).astype(o_ref.dtype)
        if with_lse:
            lse_ref[sub * qs:(sub + 1) * qs, :] = lses[sub]


LSE_COLS = LANES
ATTN_UNITS_PER_STAGE = 16


def _band_attn_call(q, k, v, *, n_kv, grp, blk, sinks, with_lse, name):
    bsz, n_res, t_len, _ = q.shape
    tq = min(512, t_len)
    qs = min(128, tq)
    nt = t_len // tq
    ratio = tq // blk
    nblk = t_len // blk
    qw = n_kv * grp * HEAD_DIM
    kw = n_kv * HEAD_DIM
    cur = lambda wd: pl.BlockSpec((None, None, tq, wd), lambda b, r, i: (b, r, i, 0))
    prev = lambda wd: pl.BlockSpec((None, None, blk, wd), lambda b, r, i: (b, r, jnp.maximum(i * ratio - 1, 0), 0))
    nxt = lambda wd: pl.BlockSpec((None, None, blk, wd),
                                  lambda b, r, i: (b, r, jnp.minimum((i + 1) * ratio, nblk - 1), 0))
    in_specs = [cur(qw), prev(kw), cur(kw), nxt(kw), prev(kw), cur(kw), nxt(kw)]
    args = [q, k, k, k, v, v, v]
    if sinks is not None:
        in_specs.append(pl.BlockSpec(memory_space=pltpu.SMEM))
        args.append(sinks)
    out_shape = [jax.ShapeDtypeStruct((bsz, n_res, t_len, qw), BF16)]
    out_specs = [cur(qw)]
    if with_lse:
        out_shape.append(jax.ShapeDtypeStruct((bsz, n_res, t_len, LSE_COLS), F32))
        out_specs.append(cur(LSE_COLS))
    return pl.pallas_call(
        functools.partial(_band_attn_kernel, n_kv=n_kv, grp=grp, blk=blk, tq=tq, qs=qs, t_len=t_len,
                          with_sink=sinks is not None, with_lse=with_lse),
        grid=(bsz, n_res, nt),
        in_specs=in_specs,
        out_specs=out_specs,
        out_shape=out_shape,
        compiler_params=_params(("parallel", "parallel", "parallel")),
        name=name,
    )(*args)


def _dilated_group(qg, kg, vg, window, dil):
    return _band_attn_call(qg, kg, vg, n_kv=DIL_HEADS_PER_GROUP, grp=1, blk=window // (2 * dil), sinks=None,
                           with_lse=True, name=f"dilated_attn_{dil}")


def _layer_norm(r, g, b):
    mu = jnp.mean(r, -1, keepdims=True)
    var = jnp.mean(jnp.square(r - mu), -1, keepdims=True)
    return (r - mu) * lax.rsqrt(var + LN_EPS) * g + b


def _post_mix(y, x_ref, mod_ref, lng_ref, lnb_ref, rwt_ref, x1_ref, aff_ref):
    m = mod_ref[...]
    x1 = _layer_norm(ALPHA * x_ref[...] + (1.0 + m[2:3]) * y, lng_ref[...], lnb_ref[...])
    x1_ref[...] = x1.reshape(x1_ref.shape)
    h2 = x1 * (1.0 + m[4:5]) + m[3:4]
    logits = _dot_nt(rwt_ref[...], h2, HI)
    e = jnp.exp(logits - jnp.max(logits, axis=0, keepdims=True))
    aff_ref[...] = e / jnp.sum(e, axis=0, keepdims=True)


def _position_major(ref, scratch):
    dil, n, wd = ref.shape
    if dil == 1:
        return ref[0].astype(F32)
    for r in range(dil):
        a = ref[r].astype(F32)
        for c in range(wd // LANES):
            scratch[c, pl.ds(r, n, stride=dil), :] = a[:, c * LANES:(c + 1) * LANES]
    return jnp.concatenate([scratch[c] for c in range(wd // LANES)], axis=1)


def _outproj0_kernel(of_ref, ob_ref, z_ref, dnn_ref, og0_ref, og1_ref, og2_ref, l0_ref, l1_ref, l2_ref,
                     w_ref, x_ref, mod_ref, lng_ref, lnb_ref, rwt_ref, x1_ref, aff_ref, regroup_ref):
    dk = DN_HEAD_DIM
    od = of_ref[...] + ob_ref[...]
    z = z_ref[...]
    parts = []
    for h in range(DN_HEADS):
        oh = od[:, h * dk:(h + 1) * dk]
        oh = oh * lax.rsqrt(jnp.mean(oh * oh, -1, keepdims=True) + NORM_EPS) * dnn_ref[...]
        parts.append(oh * _silu(z[:, h * dk:(h + 1) * dk]))
    o_dn = jnp.concatenate(parts, axis=1).astype(BF16)
    l0, l1, l2 = (_position_major(r, regroup_ref) for r in (l0_ref, l1_ref, l2_ref))
    og0, og1, og2 = (_position_major(r, regroup_ref) for r in (og0_ref, og1_ref, og2_ref))
    mx = jnp.maximum(jnp.maximum(l0, l1), l2)
    e0, e1, e2 = jnp.exp(l0 - mx), jnp.exp(l1 - mx), jnp.exp(l2 - mx)
    den = e0 + e1 + e2
    head = lax.broadcasted_iota(I32, (l0.shape[0], DIL_GROUP_WIDTH), 1) >> int(math.log2(HEAD_DIM))

    def per_lane(wt):
        out = jnp.zeros(head.shape, F32)
        for h in range(DIL_HEADS_PER_GROUP):
            out = jnp.where(head == h, wt[:, h:h + 1], out)
        return out

    o_dil = (per_lane(e0 / den) * og0 + per_lane(e1 / den) * og1 + per_lane(e2 / den) * og2).astype(BF16)
    y = _dot(o_dn, w_ref[0:DN_WIDTH, :]) + _dot(o_dil, w_ref[DN_WIDTH:DN_WIDTH + DIL_GROUP_WIDTH, :])
    _post_mix(y, x_ref, mod_ref, lng_ref, lnb_ref, rwt_ref, x1_ref, aff_ref)


def _outproj1_kernel(o_ref, w_ref, x_ref, mod_ref, lng_ref, lnb_ref, rwt_ref, x1_ref, aff_ref):
    y = _dot(o_ref[...], w_ref[...])
    _post_mix(y, x_ref, mod_ref, lng_ref, lnb_ref, rwt_ref, x1_ref, aff_ref)


def _tail_specs(bsz, s, d, tm, n_e):
    row = lambda wd: pl.BlockSpec((None, tm, wd), lambda b, i: (b, i, 0))
    const = lambda shp: pl.BlockSpec(shp, lambda b, i: tuple(0 for _ in shp))
    in_specs = [row(d), pl.BlockSpec((None, 6, d), lambda b, i: (b, 0, 0)), const((1, d)), const((1, d)),
                const((n_e, d))]
    out_specs = [pl.BlockSpec((None, tm, d // LANES, LANES), lambda b, i: (b, i, 0, 0)),
                 pl.BlockSpec((None, n_e, tm), lambda b, i: (b, 0, i))]
    out_shape = [jax.ShapeDtypeStruct((bsz, s, d // LANES, LANES), F32), jax.ShapeDtypeStruct((bsz, n_e, s), F32)]
    return in_specs, out_specs, out_shape


def _outproj0_call(o_scan, z, dn_norm, ogs, lses, w_out, x, mod, ln_g, ln_b, router_w, tm):
    bsz, s, d = x.shape
    n_e = router_w.shape[1]
    tail_in, out_specs, out_shape = _tail_specs(bsz, s, d, tm, n_e)
    row = lambda wd: pl.BlockSpec((None, tm, wd), lambda b, i: (b, i, 0))
    dirspec = lambda dd: pl.BlockSpec((None, None, tm, DN_WIDTH), lambda b, i: (dd, b, i, 0))
    gw = DIL_GROUP_WIDTH
    resid = lambda a: pl.BlockSpec((None, a.shape[1], tm // a.shape[1], a.shape[3]), lambda b, i: (b, 0, i, 0))
    in_specs = [dirspec(0), dirspec(1), row(DN_WIDTH), pl.BlockSpec((1, DN_HEAD_DIM), lambda b, i: (0, 0))]
    in_specs += [resid(a) for a in ogs] + [resid(a) for a in lses]
    in_specs += [pl.BlockSpec(w_out.shape, lambda b, i: (0, 0))] + tail_in
    return pl.pallas_call(
        _outproj0_kernel, grid=(bsz, s // tm), in_specs=in_specs, out_specs=out_specs, out_shape=out_shape,
        scratch_shapes=[pltpu.VMEM((gw // LANES, tm, LANES), F32)],
        compiler_params=_params(("parallel", "parallel")), name="outproj_deltanet_dilated",
    )(o_scan, o_scan, z, dn_norm.reshape(1, -1), *ogs, *lses, w_out, x, mod, ln_g.reshape(1, d), ln_b.reshape(1, d),
      router_w.T)


def _outproj1_call(o, w_out, x, mod, ln_g, ln_b, router_w, tm):
    bsz, s, d = x.shape
    n_e = router_w.shape[1]
    tail_in, out_specs, out_shape = _tail_specs(bsz, s, d, tm, n_e)
    in_specs = [pl.BlockSpec((None, tm, o.shape[-1]), lambda b, i: (b, i, 0)),
                pl.BlockSpec(w_out.shape, lambda b, i: (0, 0))] + tail_in
    return pl.pallas_call(
        _outproj1_kernel, grid=(bsz, s // tm), in_specs=in_specs, out_specs=out_specs, out_shape=out_shape,
        compiler_params=_params(("parallel", "parallel")), name="outproj_swa",
    )(o, w_out, x, mod, ln_g.reshape(1, d), ln_b.reshape(1, d), router_w.T)


def _topk_kernel(a_ref, idx_ref, *, cap, jb):
    a = a_ref[...]
    n_e, rows, _ = a.shape
    bits = pltpu.bitcast(a, I32)
    thr = jnp.zeros((n_e, 1, 1), I32)
    for bit in range(30, -1, -1):
        cand = thr | (1 << bit)
        cnt = jnp.sum(jnp.where(bits >= cand, 1, 0), axis=(1, 2), keepdims=True)
        thr = jnp.where(cnt >= cap, cand, thr)
    gt = jnp.where(bits > thr, 1.0, 0.0).astype(F32)
    eq = jnp.where(bits == thr, 1.0, 0.0).astype(F32)
    need = cap - jnp.sum(gt, axis=(1, 2), keepdims=True)
    ru = lax.broadcasted_iota(I32, (LANES, LANES), 0)
    cu = lax.broadcasted_iota(I32, (LANES, LANES), 1)
    upper = jnp.where(ru <= cu, 1.0, 0.0).astype(BF16)
    rl = lax.broadcasted_iota(I32, (rows, rows), 0)
    cl = lax.broadcasted_iota(I32, (rows, rows), 1)
    lstrict = jnp.where(cl < rl, 1.0, 0.0).astype(BF16)

    def fold_cumsum(x01):
        within = _dot(x01.astype(BF16), upper)
        rowtot = jnp.broadcast_to(within[:, LANES - 1:LANES], within.shape)
        before = _dot(lstrict, rowtot.astype(BF16))
        return within + before, before + rowtot

    rowid = lax.broadcasted_iota(I32, (rows, jb), 0).astype(F32)
    for e in range(n_e):
        eq_before = fold_cumsum(eq[e])[0] - eq[e]
        sel = jnp.maximum(gt[e], jnp.where(eq_before < need[e], eq[e], 0.0))
        count, count_end = fold_cumsum(sel)
        for j0 in range(0, cap, jb):
            slot = (j0 + lax.broadcasted_iota(I32, (1, jb), 1)).astype(F32)
            row = jnp.sum(jnp.where(count_end[:, 0:1] <= slot, 1.0, 0.0), axis=0, keepdims=True)
            onehot = jnp.where(rowid == row, 1.0, 0.0)
            count_row = _dot_tn(count, onehot, HI)
            lane = jnp.sum(jnp.where(count_row <= slot, 1.0, 0.0), axis=0, keepdims=True)
            idx_ref[e:e + 1, j0:j0 + jb] = (row * LANES + lane).astype(I32)


def _topk_call(aff, cap):
    bsz, n_e, s = aff.shape
    rows = s // LANES
    return pl.pallas_call(
        functools.partial(_topk_kernel, cap=cap, jb=min(512, cap)),
        grid=(bsz,),
        in_specs=[pl.BlockSpec((None, n_e, rows, LANES), lambda b: (b, 0, 0, 0))],
        out_specs=pl.BlockSpec((None, n_e, cap), lambda b: (b, 0, 0)),
        out_shape=jax.ShapeDtypeStruct((bsz, n_e, cap), I32),
        compiler_params=_params(("parallel",)), name="topk_route",
    )(aff.reshape(bsz, n_e, rows, LANES))


SUBLANES = 8


def _moe_kernel(idx_hbm, aff_hbm, x_hbm, mod_ref, wg_ref, wu_ref, wd_ref, out_hbm,
                idx_s, aff_s, xg0, xg1, hid_s, y, acc, sem_i, sem_a, sem_g, sem_o, *, cap, n_f, y_rows):
    b = pl.program_id(0)
    e = pl.program_id(1)
    f = pl.program_id(2)
    n_e = pl.num_programs(1)
    per_f = cap // n_f

    def row_copy(tok, j, dst, sem):
        return pltpu.make_async_copy(x_hbm.at[b, tok], dst.at[j], sem)

    def wait_rows(dst, sem):
        pltpu.make_async_copy(x_hbm.at[b, pl.ds(0, cap)], dst, sem).wait()

    @pl.when((e == 0) & (f == 0))
    def _():
        acc[...] = jnp.zeros(acc.shape, acc.dtype)
        ci = pltpu.make_async_copy(idx_hbm.at[b], idx_s, sem_i)
        ci.start()
        ci.wait()

        def gather(jo, carry):
            for r in range(SUBLANES):
                j = jo * SUBLANES + r
                row_copy(idx_s[j], j, xg0, sem_g.at[0]).start(priority=r % 2)
            return carry

        lax.fori_loop(0, cap // SUBLANES, gather, 0)
        wait_rows(xg0, sem_g.at[0])

    aff_copy = pltpu.make_async_copy(aff_hbm.at[b, e], aff_s, sem_a)

    @pl.when(f == 0)
    def _():
        aff_copy.start()

    def expert_step(xg_cur, xg_next, sem_next):
        next_base = jnp.minimum(e + 1, n_e - 1) * cap + f * per_f
        for r in range(per_f):
            row_copy(idx_s[next_base + r], f * per_f + r, xg_next, sem_next).start(priority=r % 2)
        m = mod_ref[...]
        xv = (xg_cur[...].reshape(cap, m.shape[-1]) * (1.0 + m[4:5]) + m[3:4]).astype(BF16)
        hid_s[f] = (_silu(_dot(xv, wg_ref[...])) * _dot(xv, wu_ref[...])).astype(BF16)

        @pl.when(f == n_f - 1)
        def _():
            aff_copy.wait()
            for r0 in range(0, cap, y_rows):
                hid = jnp.concatenate([hid_s[i, r0:r0 + y_rows, :] for i in range(n_f)], axis=1)
                y[...] = _dot(hid, wd_ref[...])

                def scatter(jo, carry):
                    j0 = pl.multiple_of(jo * SUBLANES, SUBLANES)
                    yc = y[pl.ds(j0, SUBLANES), :].reshape((SUBLANES,) + acc.shape[1:])
                    toks = [idx_s[e * cap + r0 + j0 + r] for r in range(SUBLANES)]
                    olds = [acc[toks[r]] for r in range(SUBLANES)]
                    for r in range(SUBLANES):
                        acc[toks[r]] = olds[r] + aff_s[toks[r]] * yc[r]
                    return carry

                lax.fori_loop(0, y_rows // SUBLANES, scatter, 0)
            wait_rows(xg_next, sem_next)

            @pl.when(e == n_e - 1)
            def _():
                co = pltpu.make_async_copy(acc, out_hbm.at[b], sem_o)
                co.start()
                co.wait()

    @pl.when(e % 2 == 0)
    def _():
        expert_step(xg0, xg1, sem_g.at[1])

    @pl.when(e % 2 == 1)
    def _():
        expert_step(xg1, xg0, sem_g.at[0])


def _moe_call(x1, mod, idx, aff, wg, wu, wd, fcw):
    bsz, s, nt, _ = x1.shape
    d = nt * LANES
    n_e, _, ff = wg.shape
    cap = idx.shape[-1]
    fcw = min(fcw, ff)
    n_f = ff // fcw
    y_rows = min(512, cap)
    assert n_e % 2 == 0 and cap % n_f == 0 and cap % y_rows == 0 and d // LANES == SUBLANES
    anyspec = pl.BlockSpec(memory_space=pl.ANY)
    return pl.pallas_call(
        functools.partial(_moe_kernel, cap=cap, n_f=n_f, y_rows=y_rows),
        grid=(bsz, n_e, n_f),
        in_specs=[anyspec, anyspec, anyspec,
                  pl.BlockSpec((None, 6, d), lambda b, e, f: (b, 0, 0)),
                  pl.BlockSpec((None, d, fcw), lambda b, e, f: (e, 0, f)),
                  pl.BlockSpec((None, d, fcw), lambda b, e, f: (e, 0, f)),
                  pl.BlockSpec((None, ff, d), lambda b, e, f: (e, 0, 0))],
        out_specs=anyspec,
        out_shape=jax.ShapeDtypeStruct((bsz, s, d // LANES, LANES), F32),
        scratch_shapes=[pltpu.SMEM((n_e * cap,), I32), pltpu.SMEM((s,), F32),
                        pltpu.VMEM((cap, nt, LANES), F32), pltpu.VMEM((cap, nt, LANES), F32),
                        pltpu.VMEM((n_f, cap, fcw), BF16), pltpu.VMEM((y_rows, d), F32),
                        pltpu.VMEM((s, nt, LANES), F32),
                        pltpu.SemaphoreType.DMA(()), pltpu.SemaphoreType.DMA(()),
                        pltpu.SemaphoreType.DMA((2,)), pltpu.SemaphoreType.DMA(())],
        compiler_params=_params(("arbitrary", "arbitrary", "arbitrary")),
        name="moe_experts",
    )(idx.reshape(bsz, n_e * cap), aff, x1, mod, wg, wu, wd)


def _ln2_kernel(x_ref, y_ref, g2_ref, g_ref, b_ref, o_ref):
    r = ALPHA * x_ref[...] + (1.0 + g2_ref[...]) * y_ref[...]
    o_ref[...] = _layer_norm(r.reshape(o_ref.shape), g_ref[...], b_ref[...])


def _ln2_call(x1, moe, mod, g, b, tm):
    bsz, s, nt, _ = x1.shape
    d = nt * LANES
    row = pl.BlockSpec((None, tm, d), lambda bb, i: (bb, i, 0))
    tiles = pl.BlockSpec((None, tm, nt, LANES), lambda bb, i: (bb, i, 0, 0))
    vec = pl.BlockSpec((1, d), lambda bb, i: (0, 0))
    return pl.pallas_call(
        _ln2_kernel, grid=(bsz, s // tm),
        in_specs=[tiles, tiles, pl.BlockSpec((None, 1, nt, LANES), lambda bb, i: (bb, 0, 0, 0)), vec, vec],
        out_specs=row, out_shape=jax.ShapeDtypeStruct((bsz, s, d), F32),
        compiler_params=_params(("parallel", "parallel")), name="ffn_postnorm",
    )(x1, moe, mod[:, 5].reshape(bsz, 1, nt, LANES), g.reshape(1, d), b.reshape(1, d))


def _ffn_block(x1, aff, mod, wg, wu, wd, ln_g, ln_b, tm):
    s = x1.shape[1]
    cap = (EC_FACTOR * s) // N_EXPERTS
    idx = _topk_call(aff, cap)
    moe = _moe_call(x1, mod, idx, aff, wg.astype(BF16), wu.astype(BF16), wd.astype(BF16), fcw=512)
    return _ln2_call(x1, moe, mod, ln_g, ln_b, tm)


def kernel(x, c, positions, ada_w, ada_b, ab_w_in, ab_conv_w, ab_a_log, ab_dt_bias, ab_dn_norm, ab_w_out, swa_w_in,
           swa_sinks, swa_w_out, ln_mix_g, ln_mix_b, router_w, moe_w_gate, moe_w_up, moe_w_down, ln_ffn_g, ln_ffn_b):
    bsz, s, d = x.shape
    tm = min(512, s)
    mod = _mod_call(c, ada_w, ada_b).reshape(DEPTH, bsz, 6, d)
    rope = _rope_tables(positions)

    w_in = ab_w_in[0]
    n_a = 4 * DN_WIDTH
    n_g = 4 * DN_HEADS
    gw = DIL_GROUP_WIDTH
    q_scale = HEAD_DIM ** -0.5
    cols = [w_in[:, :n_a], jnp.pad(w_in[:, n_a:n_a + n_g], ((0, 0), (0, LANES - n_g)))]
    groups = [(0, 3 * DN_WIDTH, 0, False), (3 * DN_WIDTH, DN_WIDTH, 1, False), (n_a, LANES, 2, False)]
    for gi in range(len(DIL_PAIRS)):
        for part in range(3):
            c0 = n_a + n_g + part * DIL_WIDTH + gi * gw
            cols.append(w_in[:, c0:c0 + gw] * (q_scale if part == 0 else 1.0))
            groups.append((n_a + LANES + (3 * gi + part) * gw, gw, 3 + 3 * gi + part, part < 2))
    w0 = jnp.concatenate(cols, axis=1).astype(BF16)
    dils = tuple(dil for _, dil in DIL_PAIRS for _ in range(3))
    outs0 = _inproj_call(
        x, mod[0], w0, rope, _chunk_plan(groups), (3 * DN_WIDTH, DN_WIDTH, LANES) + (gw,) * 9,
        (F32, F32, F32) + (BF16,) * 9, (1, 1, 1) + dils, tm, "inproj_deltanet_dilated")
    qkv_a, z, gates = outs0[:3]
    u, w, qd, kd, at, egl = _dn_prep_call(qkv_a, gates, ab_conv_w[0], ab_a_log[0], ab_dt_bias[0], tm)
    o_scan = _dn_scan_call(u, w, qd, kd, at, egl, min(256, s))
    ogs, lses = [], []
    for gi, (window, dil) in enumerate(DIL_PAIRS):
        qkv_g = [a.reshape(bsz, dil, s // dil, gw) for a in outs0[3 + 3 * gi:6 + 3 * gi]]
        o_g, lse_g = _dilated_group(*qkv_g, window, dil)
        ogs.append(o_g)
        lses.append(lse_g)
    x1, aff = _outproj0_call(o_scan, z, ab_dn_norm[0], ogs, lses, ab_w_out[0].astype(BF16), x, mod[0],
                             ln_mix_g[0], ln_mix_b[0], router_w[0], tm)
    x = _ffn_block(x1, aff, mod[0], moe_w_gate[0], moe_w_up[0], moe_w_down[0], ln_ffn_g[0], ln_ffn_b[0], tm)

    qw = SWA_Q_HEADS * HEAD_DIM
    kw = SWA_KV_HEADS * HEAD_DIM
    plan1 = _chunk_plan([(0, qw, 0, True), (qw, kw, 1, True), (qw + kw, kw, 2, False)])
    w1 = jnp.concatenate([swa_w_in[0][:, :qw] * q_scale, swa_w_in[0][:, qw:]], axis=1).astype(BF16)
    q1, k1, v1 = _inproj_call(x, mod[1], w1, rope, plan1, (qw, kw, kw), (BF16, BF16, BF16), (1, 1, 1), tm,
                              "inproj_swa")
    (o1,) = _band_attn_call(q1[:, None], k1[:, None], v1[:, None], n_kv=SWA_KV_HEADS,
                            grp=SWA_Q_HEADS // SWA_KV_HEADS, blk=SWA_WINDOW, sinks=swa_sinks[0], with_lse=False,
                            name="swa_attn")
    x1, aff = _outproj1_call(o1[:, 0], swa_w_out[0].astype(BF16), x, mod[1], ln_mix_g[1], ln_mix_b[1], router_w[1],
                             tm)
    x = _ffn_block(x1, aff, mod[1], moe_w_gate[1], moe_w_up[1], moe_w_down[1], ln_ffn_g[1], ln_ffn_b[1], tm)
    return x
```

```python
import functools
import math

import jax
import jax.numpy as jnp
from jax import lax
from jax.experimental import pallas as pl
from jax.experimental.pallas import tpu as pltpu

F32 = jnp.float32
BF16 = jnp.bfloat16
I32 = jnp.int32
HI = lax.Precision.HIGHEST

DEPTH = 2
HEAD_DIM = 64
ROT_DIM = HEAD_DIM // 4
ROPE_THETA = 500000.0
DN_HEADS = 4
DN_HEAD_DIM = 128
DN_CHUNK = 64
DN_CONV = 5
DN_WIDTH = DN_HEADS * DN_HEAD_DIM
DIL_PAIRS = ((128, 1), (512, 4), (2048, 16))
DIL_HEADS_PER_GROUP = 4
DIL_GROUP_WIDTH = DIL_HEADS_PER_GROUP * HEAD_DIM
DIL_WIDTH = DIL_GROUP_WIDTH * len(DIL_PAIRS)
SWA_Q_HEADS = 16
SWA_KV_HEADS = 4
SWA_WINDOW = 128
N_EXPERTS = 16
EC_FACTOR = 2
ALPHA = (2.0 * DEPTH) ** 0.25
LN_EPS = 1e-5
NORM_EPS = 1e-6
NEG = -1e30
LANES = 128
HALO = 8
VMEM_LIMIT = 56 * 1024 * 1024
MOE_VMEM_LIMIT = 60 * 1024 * 1024


def _dot(a, b, prec=None):
    return jnp.dot(a, b, preferred_element_type=F32, precision=prec)


def _dot_nt(a, b, prec=None):
    return lax.dot_general(a, b, (((1,), (1,)), ((), ())), preferred_element_type=F32, precision=prec)


def _dot_tn(a, b, prec=None):
    return lax.dot_general(a, b, (((0,), (0,)), ((), ())), preferred_element_type=F32, precision=prec)


def _silu(x):
    return x * jax.nn.sigmoid(x)


def _params(sem, vmem_limit=VMEM_LIMIT):
    return pltpu.CompilerParams(dimension_semantics=sem, vmem_limit_bytes=vmem_limit)


def _mod_kernel(c_ref, w_ref, b_ref, o_ref):
    o_ref[...] = _dot(_silu(c_ref[...]), w_ref[...], HI) + b_ref[...]


def _mod_call(c, ada_w, ada_b):
    depth, d, n6 = ada_w.shape
    bsz = c.shape[0]
    tn = n6 // 4
    return pl.pallas_call(
        _mod_kernel,
        grid=(depth, n6 // tn),
        in_specs=[pl.BlockSpec((bsz, d), lambda i, j: (0, 0)),
                  pl.BlockSpec((None, d, tn), lambda i, j: (i, 0, j)),
                  pl.BlockSpec((None, 1, tn), lambda i, j: (i, 0, j))],
        out_specs=pl.BlockSpec((None, bsz, tn), lambda i, j: (i, 0, j)),
        out_shape=jax.ShapeDtypeStruct((depth, bsz, n6), F32),
        compiler_params=_params(("arbitrary", "arbitrary")),
        name="adaln_mod",
    )(c, ada_w, ada_b.reshape(depth, 1, n6))


def _inproj_kernel(x_ref, mod_ref, w_ref, rope_ref, *refs, plan):
    *out_refs, regroup_ref = refs
    m = mod_ref[...]
    h = (x_ref[...] * (1.0 + m[1:2]) + m[0:1]).astype(BF16)
    half = ROT_DIM // 2
    if any(p[4] for p in plan):
        src = lax.broadcasted_iota(I32, (ROT_DIM, LANES), 0)
        lane = lax.broadcasted_iota(I32, (ROT_DIM, LANES), 1) & (HEAD_DIM - 1)
        pick_cos = jnp.where(lane < ROT_DIM, jnp.where((lane & (half - 1)) == src, 1.0, 0.0), 0.0)
        pick_sin = jnp.where(lane < half, jnp.where(src == lane + half, -1.0, 0.0),
                             jnp.where(lane < ROT_DIM, jnp.where(src == lane, 1.0, 0.0), 0.0))
        cs = rope_ref[...]
        lane1 = lax.broadcasted_iota(I32, (1, LANES), 1) & (HEAD_DIM - 1)
        cos_t = _dot(cs, pick_cos, HI) + jnp.where(lane1 < ROT_DIM, 0.0, 1.0)
        sin_t = _dot(cs, pick_sin, HI)
        sin_a = jnp.where(lane1 < half, sin_t, 0.0)
        sin_b = sin_t - sin_a
    for c0, width, oi, o0, rope in plan:
        acc = _dot(h, w_ref[:, c0:c0 + width])
        if rope:
            reps = width // LANES
            tile = lambda a: jnp.concatenate([a] * reps, axis=1)
            acc = (acc * tile(cos_t) + pltpu.roll(acc, width - half, 1) * tile(sin_a)
                   + pltpu.roll(acc, half, 1) * tile(sin_b))
        out = out_refs[oi]
        if len(out.shape) == 2:
            out[:, o0:o0 + width] = acc.astype(out.dtype)
        else:
            dil, n = out.shape[0], out.shape[1]
            for c in range(width // LANES):
                regroup_ref[c] = acc[:, c * LANES:(c + 1) * LANES]
            for r in range(dil):
                out[r] = jnp.concatenate([regroup_ref[c, pl.ds(r, n, stride=dil), :] for c in range(width // LANES)],
                                         axis=1).astype(out.dtype)


def _inproj_call(x, mod, w, rope, plan, out_widths, out_dtypes, out_dils, tm, name):
    bsz, s, d = x.shape
    n = w.shape[1]
    out_shape, out_specs = [], []
    for ow, od, dil in zip(out_widths, out_dtypes, out_dils):
        if dil == 1:
            out_shape.append(jax.ShapeDtypeStruct((bsz, s, ow), od))
            out_specs.append(pl.BlockSpec((None, tm, ow), lambda b, i: (b, i, 0)))
        else:
            out_shape.append(jax.ShapeDtypeStruct((bsz, dil, s // dil, ow), od))
            out_specs.append(pl.BlockSpec((None, dil, tm // dil, ow), lambda b, i: (b, 0, i, 0)))
    chunk_w = max(p[1] for p in plan)
    return pl.pallas_call(
        functools.partial(_inproj_kernel, plan=plan),
        grid=(bsz, s // tm),
        in_specs=[pl.BlockSpec((None, tm, d), lambda b, i: (b, i, 0)),
                  pl.BlockSpec((None, 6, d), lambda b, i: (b, 0, 0)),
                  pl.BlockSpec((d, n), lambda b, i: (0, 0)),
                  pl.BlockSpec((None, tm, ROT_DIM), lambda b, i: (b, i, 0))],
        out_specs=out_specs,
        out_shape=out_shape,
        scratch_shapes=[pltpu.VMEM((chunk_w // LANES, tm, LANES), F32)],
        compiler_params=_params(("parallel", "parallel")),
        name=name,
    )(x, mod, w, rope)


def _chunk_plan(groups, chunk=256):
    plan = []
    for c0, width, oi, rope in groups:
        off = 0
        while off < width:
            wd = min(chunk, width - off)
            plan.append((c0 + off, wd, oi, off, rope))
            off += wd
    return tuple(plan)


def _rope_tables(positions):
    inv_freq = jnp.power(ROPE_THETA, -jnp.arange(0, ROT_DIM, 2, dtype=F32) / ROT_DIM)
    ang = positions.astype(F32)[..., None] * inv_freq
    return jnp.concatenate([jnp.cos(ang), jnp.sin(ang)], -1)


def _dn_prep_kernel(xa_ref, top_ref, bot_ref, gt_ref, cw_ref, alog_ref, dtb_ref,
                    u_ref, w_ref, qd_ref, kd_ref, at_ref, egl_ref,
                    q_s, k_s, v_s, gc_s, *, t):
    ch = DN_CHUNK
    dk = DN_HEAD_DIM
    pad = (DN_CONV - 1) // 2
    for grp, dst in enumerate((q_s, k_s, v_s)):
        cols = slice(grp * DN_WIDTH, (grp + 1) * DN_WIDTH)
        xe = jnp.concatenate([top_ref[:, cols], xa_ref[:, cols], bot_ref[:, cols]], axis=0)
        y = jnp.zeros((t, DN_WIDTH), F32)
        for k in range(DN_CONV):
            y = y + xe[HALO - pad + k:HALO - pad + k + t, :] * cw_ref[k:k + 1, cols]
        y = _silu(y)
        if grp < 2:
            scale = dk ** -0.5 if grp == 0 else 1.0
            parts = []
            for h in range(DN_HEADS):
                yh = y[:, h * dk:(h + 1) * dk]
                parts.append(yh * lax.rsqrt(jnp.sum(yh * yh, -1, keepdims=True) + NORM_EPS) * scale)
            y = jnp.concatenate(parts, axis=1)
        dst[...] = y

    g = gt_ref[...]
    lane = lax.broadcasted_iota(I32, (t, LANES), 1)
    z = g + dtb_ref[...]
    softplus = jnp.maximum(z, 0.0) + jnp.log1p(jnp.exp(-jnp.abs(z)))
    dec = -jnp.exp(alog_ref[...]) * softplus
    gv = jnp.where(lane < 2 * DN_HEADS, dec, jnp.where(lane < 4 * DN_HEADS, jax.nn.sigmoid(g), 0.0))
    ri = lax.broadcasted_iota(I32, (t, t), 0)
    ci = lax.broadcasted_iota(I32, (t, t), 1)
    shift = int(math.log2(ch))
    same = (ri >> shift) == (ci >> shift)
    pre = jnp.where(same & (ci <= ri), 1.0, 0.0).astype(F32)
    suf = jnp.where(same & (ci >= ri), 1.0, 0.0).astype(F32)
    gcf = _dot(pre, gv, HI)
    gcb = _dot(suf, gv, HI)
    gc_s[...] = jnp.where(lane < DN_HEADS, gcf, jnp.where(lane < 2 * DN_HEADS, gcb, gv))

    rr = lax.broadcasted_iota(I32, (ch, 2 * ch), 0)
    cc = lax.broadcasted_iota(I32, (ch, 2 * ch), 1)
    fwd = cc < ch
    cj = jnp.where(fwd, cc, cc - ch)
    ahead = jnp.where(fwd, cj - rr, rr - cj)
    incl = ahead <= 0
    strict = ahead < 0
    eye2 = jnp.where(cj == rr, 1.0, 0.0).astype(F32)

    def blockdiag(p):
        return jnp.concatenate([jnp.where(fwd, p, 0.0), jnp.where(fwd, 0.0, p)], axis=0).astype(BF16)

    n_sq = int(math.log2(ch)) - 1
    per_iter = math.gcd(t // ch, 8)

    def chunk_body(ci, carry):
        units = []
        for sub in range(per_iter):
            c = ci * per_iter + sub
            rows = pl.ds(pl.multiple_of(c * ch, ch), ch)
            gcc = gc_s[rows, :]
            gct = gcc.T
            for h in range(DN_HEADS):
                units.append((c, rows, gcc, gct, h))

        st, pws = [], []
        for c, rows, gcc, gct, h in units:
            bcast = lambda col, gcc=gcc: jnp.broadcast_to(gcc[:, col:col + 1], (ch, LANES))
            gf, gb = bcast(h), bcast(DN_HEADS + h)
            bf, bb = bcast(2 * DN_HEADS + h), bcast(3 * DN_HEADS + h)
            grow = jnp.concatenate([gct[h:h + 1, :], gct[DN_HEADS + h:DN_HEADS + h + 1, :]], axis=1)
            diff = jnp.where(fwd, gf, gb) - grow
            decay = jnp.where(incl, jnp.exp(jnp.where(incl, diff, 0.0)), 0.0)
            hs = slice(h * dk, (h + 1) * dk)
            k16 = k_s[rows, hs].astype(BF16)
            kk = jnp.concatenate([k16, k16], axis=0)
            lower = jnp.where(strict, jnp.where(fwd, bf, bb) * _dot_nt(k16, kk) * decay, 0.0)
            intra = (_dot_nt(q_s[rows, hs].astype(BF16), kk) * decay).astype(BF16)
            at_ref[0, rows, h * ch:(h + 1) * ch] = intra[:, :ch]
            at_ref[1, rows, h * ch:(h + 1) * ch] = intra[:, ch:]
            st.append((gf, gb, bf, bb))
            pws.append(-lower)
        ainvs = [eye2 + p for p in pws]
        pbds = [blockdiag(p) for p in pws]
        for _ in range(n_sq):
            pws = [_dot(p.astype(BF16), bd) for p, bd in zip(pws, pbds)]
            pbds = [blockdiag(p) for p in pws]
            ainvs = [a + _dot(a.astype(BF16), bd) for a, bd in zip(ainvs, pbds)]

        egl_f, egl_b = [], []
        for (c, rows, _, _, h), (gf, gb, bf, bb), ainv in zip(units, st, ainvs):
            hs = slice(h * dk, (h + 1) * dk)
            q = q_s[rows, hs]
            k = k_s[rows, hs]
            v = v_s[rows, hs]
            egf, egb = jnp.exp(gf), jnp.exp(gb)
            rhs = jnp.concatenate([jnp.concatenate([v * bf, k * bf * egf], axis=1),
                                   jnp.concatenate([v * bb, k * bb * egb], axis=1)], axis=0).astype(BF16)
            uw_f = _dot(jnp.where(fwd, ainv, 0.0).astype(BF16), rhs)
            uw_b = _dot(jnp.where(fwd, 0.0, ainv).astype(BF16), rhs)
            glf = jnp.broadcast_to(gf[ch - 1:ch, :], (ch, LANES))
            glb = jnp.broadcast_to(gb[0:1, :], (ch, LANES))
            u_ref[0, rows, hs] = uw_f[:, :dk]
            u_ref[1, rows, hs] = uw_b[:, :dk]
            w_ref[0, rows, hs] = uw_f[:, dk:].astype(BF16)
            w_ref[1, rows, hs] = uw_b[:, dk:].astype(BF16)
            qd_ref[0, rows, hs] = (q * egf).astype(BF16)
            qd_ref[1, rows, hs] = (q * egb).astype(BF16)
            kd_ref[0, rows, hs] = (k * jnp.exp(glf - gf)).astype(BF16)
            kd_ref[1, rows, hs] = (k * jnp.exp(glb - gb)).astype(BF16)
            egl_f.append(jnp.exp(glf[0:1, :]))
            egl_b.append(jnp.exp(glb[0:1, :]))
            if h == DN_HEADS - 1:
                fill = [jnp.zeros((8 - DN_HEADS, LANES), F32)]
                egl_ref[0, c] = jnp.concatenate(egl_f + fill, axis=0)
                egl_ref[1, c] = jnp.concatenate(egl_b + fill, axis=0)
                egl_f, egl_b = [], []
        return carry

    lax.fori_loop(0, t // (ch * per_iter), chunk_body, 0)


def _dn_prep_call(qkv_a, gates, conv_w, a_log, dt_bias, t):
    bsz, s, cw = qkv_a.shape
    nt = s // t
    r = qkv_a.reshape(bsz, nt, t, cw)
    zero = jnp.zeros((bsz, 1, HALO, cw), F32)
    top = jnp.concatenate([zero, r[:, :-1, t - HALO:]], axis=1)
    bot = jnp.concatenate([r[:, 1:, :HALO], zero], axis=1)
    cwp = jnp.zeros((8, cw), F32).at[:DN_CONV].set(conv_w)
    alog = jnp.zeros((1, LANES), F32).at[0, :2 * DN_HEADS].set(a_log.reshape(-1))
    dtb = jnp.zeros((1, LANES), F32).at[0, :2 * DN_HEADS].set(dt_bias.reshape(-1))
    nch = s // DN_CHUNK
    wide = lambda dt, wd: jax.ShapeDtypeStruct((2, bsz, s, wd), dt)
    spec = lambda wd: pl.BlockSpec((2, None, t, wd), lambda b, i: (0, b, i, 0))
    return pl.pallas_call(
        functools.partial(_dn_prep_kernel, t=t),
        grid=(bsz, nt),
        in_specs=[pl.BlockSpec((None, t, cw), lambda b, i: (b, i, 0)),
                  pl.BlockSpec((None, None, HALO, cw), lambda b, i: (b, i, 0, 0)),
                  pl.BlockSpec((None, None, HALO, cw), lambda b, i: (b, i, 0, 0)),
                  pl.BlockSpec((None, t, LANES), lambda b, i: (b, i, 0)),
                  pl.BlockSpec((8, cw), lambda b, i: (0, 0)),
                  pl.BlockSpec((1, LANES), lambda b, i: (0, 0)),
                  pl.BlockSpec((1, LANES), lambda b, i: (0, 0))],
        out_specs=[spec(DN_WIDTH), spec(DN_WIDTH), spec(DN_WIDTH), spec(DN_WIDTH), spec(DN_HEADS * DN_CHUNK),
                   pl.BlockSpec((2, None, t // DN_CHUNK, 8, LANES), lambda b, i: (0, b, i, 0, 0))],
        out_shape=[wide(F32, DN_WIDTH), wide(BF16, DN_WIDTH), wide(BF16, DN_WIDTH), wide(BF16, DN_WIDTH),
                   wide(BF16, DN_HEADS * DN_CHUNK),
                   jax.ShapeDtypeStruct((2, bsz, nch, 8, LANES), F32)],
        scratch_shapes=[pltpu.VMEM((t, DN_WIDTH), F32), pltpu.VMEM((t, DN_WIDTH), F32),
                        pltpu.VMEM((t, DN_WIDTH), F32), pltpu.VMEM((t, LANES), F32)],
        compiler_params=_params(("parallel", "parallel")),
        name="deltanet_prep",
    )(qkv_a, top, bot, gates, cwp, alog, dtb)


def _dn_scan_kernel(u_ref, w_ref, qd_ref, kd_ref, at_ref, egl_ref, o_ref, st_ref, *, nc):
    ch = DN_CHUNK
    dk = DN_HEAD_DIM
    d = pl.program_id(0)

    @pl.when(pl.program_id(2) == 0)
    def _():
        st_ref[...] = jnp.zeros(st_ref.shape, st_ref.dtype)

    n_seq = u_ref.shape[0]
    units = [(sq, h) for sq in range(n_seq) for h in range(DN_HEADS)]
    hsl = [slice(h * dk, (h + 1) * dk) for h in range(DN_HEADS)]
    sts = [st_ref[sq * DN_HEADS + h] for sq, h in units]
    for j in range(nc):
        cc = j + d * (nc - 1 - 2 * j)
        rows = pl.ds(pl.multiple_of(cc * ch, ch), ch)
        egls = [egl_ref[sq, cc] for sq in range(n_seq)]
        sbs = [st.astype(BF16) for st in sts]
        vbs = [(u_ref[sq, rows, hsl[h]] - _dot(w_ref[sq, rows, hsl[h]], sb)).astype(BF16)
               for (sq, h), sb in zip(units, sbs)]
        qss = [_dot(qd_ref[sq, rows, hsl[h]], sb) for (sq, h), sb in zip(units, sbs)]
        sts = [st * egls[sq][h:h + 1, :] + _dot_tn(kd_ref[sq, rows, hsl[h]], vb)
               for (sq, h), st, vb in zip(units, sts, vbs)]
        for (sq, h), qs, vb in zip(units, qss, vbs):
            o_ref[sq, rows, hsl[h]] = qs + _dot(at_ref[sq, rows, h * ch:(h + 1) * ch], vb)
    for (sq, h), st in zip(units, sts):
        st_ref[sq * DN_HEADS + h] = st


def _dn_scan_call(u, w, qd, kd, at, egl, tc):
    _, bsz, s, wd = u.shape
    nb = s // tc
    nc = tc // DN_CHUNK
    n_seq = math.gcd(bsz, 2)

    def blk(d, n):
        return n + d * (nb - 1 - 2 * n)

    spec = lambda width: pl.BlockSpec((None, n_seq, tc, width), lambda d, b, n: (d, b, blk(d, n), 0))
    return pl.pallas_call(
        functools.partial(_dn_scan_kernel, nc=nc),
        grid=(2, bsz // n_seq, nb),
        in_specs=[spec(wd), spec(wd), spec(wd), spec(wd), spec(DN_HEADS * DN_CHUNK),
                  pl.BlockSpec((None, n_seq, nc, 8, LANES), lambda d, b, n: (d, b, blk(d, n), 0, 0))],
        out_specs=spec(wd),
        out_shape=jax.ShapeDtypeStruct((2, bsz, s, wd), F32),
        scratch_shapes=[pltpu.VMEM((n_seq * DN_HEADS, DN_HEAD_DIM, DN_HEAD_DIM), F32)],
        compiler_params=_params(("parallel", "parallel", "arbitrary")),
        name="deltanet_scan",
    )(u, w, qd, kd, at, egl)


def _band_attn_kernel(*refs, n_kv, grp, blk, tq, qs, t_len, with_sink, with_lse):
    q_ref, kp_ref, kc_ref, kn_ref, vp_ref, vc_ref, vn_ref = refs[:7]
    pos = 7
    sink_ref = None
    if with_sink:
        sink_ref = refs[pos]
        pos += 1
    o_ref = refs[pos]
    lse_ref = refs[pos + 1] if with_lse else None
    hd = HEAD_DIM
    i0 = pl.program_id(2) * tq
    kcat = jnp.concatenate([kp_ref[...], kc_ref[...], kn_ref[...]], axis=0)
    vcat = jnp.concatenate([vp_ref[...], vc_ref[...], vn_ref[...]], axis=0)
    kwin = qs + 2 * blk
    n_q = n_kv * grp
    n_sub = tq // qs
    lse_lane = lax.broadcasted_iota(I32, (qs, LSE_COLS), 1)
    ones_k = jnp.ones((kwin, LSE_COLS), BF16)
    lses = [jnp.zeros((qs, LSE_COLS), F32) for _ in range(n_sub)]
    outs = [[] for _ in range(n_sub)]
    biases, khs, vhs = [], [], []
    for sub in range(n_sub):
        k0 = sub * qs
        rowpos = i0 + k0 + lax.broadcasted_iota(I32, (qs, kwin), 0)
        keypos = i0 - blk + k0 + lax.broadcasted_iota(I32, (qs, kwin), 1)
        mask = (jnp.abs(keypos - rowpos) <= blk) & (keypos >= 0) & (keypos < t_len)
        biases.append(jnp.where(mask, 0.0, NEG))
        khs.append([kcat[k0:k0 + kwin, kv * hd:(kv + 1) * hd] for kv in range(n_kv)])
        vhs.append([vcat[k0:k0 + kwin, kv * hd:(kv + 1) * hd] for kv in range(n_kv)])
    units = [(sub, hq) for sub in range(n_sub) for hq in range(n_q)]
    for u0 in range(0, len(units), ATTN_UNITS_PER_STAGE):
        stage = units[u0:u0 + ATTN_UNITS_PER_STAGE]
        scs = [_dot_nt(q_ref[sub * qs:(sub + 1) * qs, hq * hd:(hq + 1) * hd], khs[sub][hq // grp]) + biases[sub]
               for sub, hq in stage]
        ps, ms = [], []
        for (sub, hq), sc in zip(stage, scs):
            m = jnp.max(sc, axis=-1, keepdims=True)
            if with_sink:
                m = jnp.maximum(m, sink_ref[hq])
            ps.append(jnp.exp((sc - m).astype(BF16)))
            ms.append(m)
        dens = [_dot(p, ones_k) for p in ps]
        for (sub, hq), p, m, den in zip(stage, ps, ms, dens):
            if with_sink:
                den = den + jnp.exp(sink_ref[hq] - m)
            if with_lse:
                lses[sub] = jnp.where(lse_lane == hq, m + jnp.log(den), lses[sub])
            outs[sub].append(_dot(p, vhs[sub][hq // grp]) / den[:, :hd])
    for sub in range(n_sub):
        o_ref[sub * qs:(sub + 1) * qs, :] = jnp.concatenate(outs[sub], axis=1).astype(o_ref.dtype)
        if with_lse:
            lse_ref[sub * qs:(sub + 1) * qs, :] = lses[sub]


LSE_COLS = LANES
ATTN_UNITS_PER_STAGE = 16


def _band_attn_call(q, k, v, *, n_kv, grp, blk, sinks, with_lse, name):
    bsz, n_res, t_len, _ = q.shape
    tq = min(512, t_len)
    qs = min(128, tq)
    nt = t_len // tq
    ratio = tq // blk
    nblk = t_len // blk
    qw = n_kv * grp * HEAD_DIM
    kw = n_kv * HEAD_DIM
    cur = lambda wd: pl.BlockSpec((None, None, tq, wd), lambda b, r, i: (b, r, i, 0))
    prev = lambda wd: pl.BlockSpec((None, None, blk, wd), lambda b, r, i: (b, r, jnp.maximum(i * ratio - 1, 0), 0))
    nxt = lambda wd: pl.BlockSpec((None, None, blk, wd),
                                  lambda b, r, i: (b, r, jnp.minimum((i + 1) * ratio, nblk - 1), 0))
    in_specs = [cur(qw), prev(kw), cur(kw), nxt(kw), prev(kw), cur(kw), nxt(kw)]
    args = [q, k, k, k, v, v, v]
    if sinks is not None:
        in_specs.append(pl.BlockSpec(memory_space=pltpu.SMEM))
        args.append(sinks)
    out_shape = [jax.ShapeDtypeStruct((bsz, n_res, t_len, qw), BF16)]
    out_specs = [cur(qw)]
    if with_lse:
        out_shape.append(jax.ShapeDtypeStruct((bsz, n_res, t_len, LSE_COLS), F32))
        out_specs.append(cur(LSE_COLS))
    return pl.pallas_call(
        functools.partial(_band_attn_kernel, n_kv=n_kv, grp=grp, blk=blk, tq=tq, qs=qs, t_len=t_len,
                          with_sink=sinks is not None, with_lse=with_lse),
        grid=(bsz, n_res, nt),
        in_specs=in_specs,
        out_specs=out_specs,
        out_shape=out_shape,
        compiler_params=_params(("parallel", "parallel", "parallel")),
        name=name,
    )(*args)


def _dilated_group(qg, kg, vg, window, dil):
    return _band_attn_call(qg, kg, vg, n_kv=DIL_HEADS_PER_GROUP, grp=1, blk=window // (2 * dil), sinks=None,
                           with_lse=True, name=f"dilated_attn_{dil}")


def _layer_norm(r, g, b):
    mu = jnp.mean(r, -1, keepdims=True)
    var = jnp.mean(jnp.square(r - mu), -1, keepdims=True)
    return (r - mu) * lax.rsqrt(var + LN_EPS) * g + b


def _post_mix(y, x_ref, mod_ref, lng_ref, lnb_ref, rwt_ref, x1_ref, aff_ref):
    m = mod_ref[...]
    x1 = _layer_norm(ALPHA * x_ref[...] + (1.0 + m[2:3]) * y, lng_ref[...], lnb_ref[...])
    x1_ref[...] = x1.reshape(x1_ref.shape)
    h2 = x1 * (1.0 + m[4:5]) + m[3:4]
    logits = _dot_nt(rwt_ref[...], h2, HI)
    e = jnp.exp(logits - jnp.max(logits, axis=0, keepdims=True))
    aff_ref[...] = e / jnp.sum(e, axis=0, keepdims=True)


def _position_major(ref, scratch):
    dil, n, wd = ref.shape
    if dil == 1:
        return ref[0].astype(F32)
    for r in range(dil):
        a = ref[r].astype(F32)
        for c in range(wd // LANES):
            scratch[c, pl.ds(r, n, stride=dil), :] = a[:, c * LANES:(c + 1) * LANES]
    return jnp.concatenate([scratch[c] for c in range(wd // LANES)], axis=1)


def _outproj0_kernel(of_ref, ob_ref, z_ref, dnn_ref, og0_ref, og1_ref, og2_ref, l0_ref, l1_ref, l2_ref,
                     w_ref, x_ref, mod_ref, lng_ref, lnb_ref, rwt_ref, x1_ref, aff_ref, regroup_ref):
    dk = DN_HEAD_DIM
    od = of_ref[...] + ob_ref[...]
    z = z_ref[...]
    parts = []
    for h in range(DN_HEADS):
        oh = od[:, h * dk:(h + 1) * dk]
        oh = oh * lax.rsqrt(jnp.mean(oh * oh, -1, keepdims=True) + NORM_EPS) * dnn_ref[...]
        parts.append(oh * _silu(z[:, h * dk:(h + 1) * dk]))
    o_dn = jnp.concatenate(parts, axis=1).astype(BF16)
    l0, l1, l2 = (_position_major(r, regroup_ref) for r in (l0_ref, l1_ref, l2_ref))
    og0, og1, og2 = (_position_major(r, regroup_ref) for r in (og0_ref, og1_ref, og2_ref))
    mx = jnp.maximum(jnp.maximum(l0, l1), l2)
    e0, e1, e2 = jnp.exp(l0 - mx), jnp.exp(l1 - mx), jnp.exp(l2 - mx)
    den = e0 + e1 + e2
    head = lax.broadcasted_iota(I32, (l0.shape[0], DIL_GROUP_WIDTH), 1) >> int(math.log2(HEAD_DIM))

    def per_lane(wt):
        out = jnp.zeros(head.shape, F32)
        for h in range(DIL_HEADS_PER_GROUP):
            out = jnp.where(head == h, wt[:, h:h + 1], out)
        return out

    o_dil = (per_lane(e0 / den) * og0 + per_lane(e1 / den) * og1 + per_lane(e2 / den) * og2).astype(BF16)
    y = _dot(o_dn, w_ref[0:DN_WIDTH, :]) + _dot(o_dil, w_ref[DN_WIDTH:DN_WIDTH + DIL_GROUP_WIDTH, :])
    _post_mix(y, x_ref, mod_ref, lng_ref, lnb_ref, rwt_ref, x1_ref, aff_ref)


def _outproj1_kernel(o_ref, w_ref, x_ref, mod_ref, lng_ref, lnb_ref, rwt_ref, x1_ref, aff_ref):
    y = _dot(o_ref[...], w_ref[...])
    _post_mix(y, x_ref, mod_ref, lng_ref, lnb_ref, rwt_ref, x1_ref, aff_ref)


def _tail_specs(bsz, s, d, tm, n_e):
    row = lambda wd: pl.BlockSpec((None, tm, wd), lambda b, i: (b, i, 0))
    const = lambda shp: pl.BlockSpec(shp, lambda b, i: tuple(0 for _ in shp))
    in_specs = [row(d), pl.BlockSpec((None, 6, d), lambda b, i: (b, 0, 0)), const((1, d)), const((1, d)),
                const((n_e, d))]
    out_specs = [pl.BlockSpec((None, tm, d // LANES, LANES), lambda b, i: (b, i, 0, 0)),
                 pl.BlockSpec((None, n_e, tm), lambda b, i: (b, 0, i))]
    out_shape = [jax.ShapeDtypeStruct((bsz, s, d // LANES, LANES), F32), jax.ShapeDtypeStruct((bsz, n_e, s), F32)]
    return in_specs, out_specs, out_shape


def _outproj0_call(o_scan, z, dn_norm, ogs, lses, w_out, x, mod, ln_g, ln_b, router_w, tm):
    bsz, s, d = x.shape
    n_e = router_w.shape[1]
    tail_in, out_specs, out_shape = _tail_specs(bsz, s, d, tm, n_e)
    row = lambda wd: pl.BlockSpec((None, tm, wd), lambda b, i: (b, i, 0))
    dirspec = lambda dd: pl.BlockSpec((None, None, tm, DN_WIDTH), lambda b, i: (dd, b, i, 0))
    gw = DIL_GROUP_WIDTH
    resid = lambda a: pl.BlockSpec((None, a.shape[1], tm // a.shape[1], a.shape[3]), lambda b, i: (b, 0, i, 0))
    in_specs = [dirspec(0), dirspec(1), row(DN_WIDTH), pl.BlockSpec((1, DN_HEAD_DIM), lambda b, i: (0, 0))]
    in_specs += [resid(a) for a in ogs] + [resid(a) for a in lses]
    in_specs += [pl.BlockSpec(w_out.shape, lambda b, i: (0, 0))] + tail_in
    return pl.pallas_call(
        _outproj0_kernel, grid=(bsz, s // tm), in_specs=in_specs, out_specs=out_specs, out_shape=out_shape,
        scratch_shapes=[pltpu.VMEM((gw // LANES, tm, LANES), F32)],
        compiler_params=_params(("parallel", "parallel")), name="outproj_deltanet_dilated",
    )(o_scan, o_scan, z, dn_norm.reshape(1, -1), *ogs, *lses, w_out, x, mod, ln_g.reshape(1, d), ln_b.reshape(1, d),
      router_w.T)


def _outproj1_call(o, w_out, x, mod, ln_g, ln_b, router_w, tm):
    bsz, s, d = x.shape
    n_e = router_w.shape[1]
    tail_in, out_specs, out_shape = _tail_specs(bsz, s, d, tm, n_e)
    in_specs = [pl.BlockSpec((None, tm, o.shape[-1]), lambda b, i: (b, i, 0)),
                pl.BlockSpec(w_out.shape, lambda b, i: (0, 0))] + tail_in
    return pl.pallas_call(
        _outproj1_kernel, grid=(bsz, s // tm), in_specs=in_specs, out_specs=out_specs, out_shape=out_shape,
        compiler_params=_params(("parallel", "parallel")), name="outproj_swa",
    )(o, w_out, x, mod, ln_g.reshape(1, d), ln_b.reshape(1, d), router_w.T)


def _topk_kernel(a_ref, idx_ref, *, cap, jb):
    a = a_ref[...]
    n_e, rows, _ = a.shape
    bits = pltpu.bitcast(a, I32)
    thr = jnp.zeros((n_e, 1, 1), I32)
    for bit in range(30, -1, -1):
        cand = thr | (1 << bit)
        cnt = jnp.sum(jnp.where(bits >= cand, 1, 0), axis=(1, 2), keepdims=True)
        thr = jnp.where(cnt >= cap, cand, thr)
    gt = jnp.where(bits > thr, 1.0, 0.0).astype(F32)
    eq = jnp.where(bits == thr, 1.0, 0.0).astype(F32)
    need = cap - jnp.sum(gt, axis=(1, 2), keepdims=True)
    ru = lax.broadcasted_iota(I32, (LANES, LANES), 0)
    cu = lax.broadcasted_iota(I32, (LANES, LANES), 1)
    upper = jnp.where(ru <= cu, 1.0, 0.0).astype(BF16)
    rl = lax.broadcasted_iota(I32, (rows, rows), 0)
    cl = lax.broadcasted_iota(I32, (rows, rows), 1)
    lstrict = jnp.where(cl < rl, 1.0, 0.0).astype(BF16)

    def fold_cumsum(x01):
        within = _dot(x01.astype(BF16), upper)
        rowtot = jnp.broadcast_to(within[:, LANES - 1:LANES], within.shape)
        before = _dot(lstrict, rowtot.astype(BF16))
        return within + before, before + rowtot

    rowid = lax.broadcasted_iota(I32, (rows, jb), 0).astype(F32)
    for e in range(n_e):
        eq_before = fold_cumsum(eq[e])[0] - eq[e]
        sel = jnp.maximum(gt[e], jnp.where(eq_before < need[e], eq[e], 0.0))
        count, count_end = fold_cumsum(sel)
        for j0 in range(0, cap, jb):
            slot = (j0 + lax.broadcasted_iota(I32, (1, jb), 1)).astype(F32)
            row = jnp.sum(jnp.where(count_end[:, 0:1] <= slot, 1.0, 0.0), axis=0, keepdims=True)
            onehot = jnp.where(rowid == row, 1.0, 0.0)
            count_row = _dot_tn(count, onehot, HI)
            lane = jnp.sum(jnp.where(count_row <= slot, 1.0, 0.0), axis=0, keepdims=True)
            idx_ref[e:e + 1, j0:j0 + jb] = (row * LANES + lane).astype(I32)


def _topk_call(aff, cap):
    bsz, n_e, s = aff.shape
    rows = s // LANES
    return pl.pallas_call(
        functools.partial(_topk_kernel, cap=cap, jb=min(512, cap)),
        grid=(bsz,),
        in_specs=[pl.BlockSpec((None, n_e, rows, LANES), lambda b: (b, 0, 0, 0))],
        out_specs=pl.BlockSpec((None, n_e, cap), lambda b: (b, 0, 0)),
        out_shape=jax.ShapeDtypeStruct((bsz, n_e, cap), I32),
        compiler_params=_params(("parallel",)), name="topk_route",
    )(aff.reshape(bsz, n_e, rows, LANES))


SUBLANES = 8


def _moe_kernel(idx_hbm, aff_hbm, x_hbm, mod_ref, wg_ref, wu_ref, wd_ref, out_hbm,
                idx_s, aff_s, xg0, xg1, hid_s, y, acc, sem_i, sem_a, sem_g, sem_o, *, cap, n_f, y_rows):
    b = pl.program_id(0)
    e = pl.program_id(1)
    f = pl.program_id(2)
    n_e = pl.num_programs(1)
    per_f = cap // n_f

    def row_copy(tok, j, dst, sem):
        return pltpu.make_async_copy(x_hbm.at[b, tok], dst.at[j], sem)

    def wait_rows(dst, sem):
        pltpu.make_async_copy(x_hbm.at[b, pl.ds(0, cap)], dst, sem).wait()

    @pl.when((e == 0) & (f == 0))
    def _():
        acc[...] = jnp.zeros(acc.shape, acc.dtype)
        ci = pltpu.make_async_copy(idx_hbm.at[b], idx_s, sem_i)
        ci.start()
        ci.wait()

        def gather(jo, carry):
            for r in range(SUBLANES):
                j = jo * SUBLANES + r
                row_copy(idx_s[j], j, xg0, sem_g.at[0]).start(priority=r % 2)
            return carry

        lax.fori_loop(0, cap // SUBLANES, gather, 0)
        wait_rows(xg0, sem_g.at[0])

    aff_copy = pltpu.make_async_copy(aff_hbm.at[b, e], aff_s, sem_a)

    @pl.when(f == 0)
    def _():
        aff_copy.start()

    def expert_step(xg_cur, xg_next, sem_next):
        next_base = jnp.minimum(e + 1, n_e - 1) * cap + f * per_f
        for r in range(per_f):
            row_copy(idx_s[next_base + r], f * per_f + r, xg_next, sem_next).start(priority=r % 2)
        m = mod_ref[...]
        xv = (xg_cur[...].reshape(cap, m.shape[-1]) * (1.0 + m[4:5]) + m[3:4]).astype(BF16)
        hid_s[f] = (_silu(_dot(xv, wg_ref[...])) * _dot(xv, wu_ref[...])).astype(BF16)

        @pl.when(f == n_f - 1)
        def _():
            aff_copy.wait()

            def scatter_rows(p, j0):
                yc = y[p, pl.ds(j0, SUBLANES), :].reshape((SUBLANES,) + acc.shape[1:])
                toks = [idx_s[e * cap + p * y_rows + j0 + r] for r in range(SUBLANES)]
                olds = [acc[toks[r]] for r in range(SUBLANES)]
                for r in range(SUBLANES):
                    acc[toks[r]] = olds[r] + aff_s[toks[r]] * yc[r]

            n_pass = cap // y_rows
            for p in range(n_pass):
                hid = jnp.concatenate([hid_s[i, p * y_rows:(p + 1) * y_rows, :] for i in range(n_f)], axis=1)
                y[p] = _dot(hid, wd_ref[...])
                if p > 0:
                    for j0 in range(0, y_rows, SUBLANES):
                        scatter_rows(p - 1, j0)

            def scatter_last(jo, carry):
                scatter_rows(n_pass - 1, pl.multiple_of(jo * SUBLANES, SUBLANES))
                return carry

            lax.fori_loop(0, y_rows // SUBLANES, scatter_last, 0)
            wait_rows(xg_next, sem_next)

            @pl.when(e == n_e - 1)
            def _():
                co = pltpu.make_async_copy(acc, out_hbm.at[b], sem_o)
                co.start()
                co.wait()

    @pl.when(e % 2 == 0)
    def _():
        expert_step(xg0, xg1, sem_g.at[1])

    @pl.when(e % 2 == 1)
    def _():
        expert_step(xg1, xg0, sem_g.at[0])


def _moe_call(x1, mod, idx, aff, wg, wu, wd, fcw):
    bsz, s, nt, _ = x1.shape
    d = nt * LANES
    n_e, _, ff = wg.shape
    cap = idx.shape[-1]
    fcw = min(fcw, ff)
    n_f = ff // fcw
    y_rows = min(512, cap)
    assert n_e % 2 == 0 and cap % n_f == 0 and cap % y_rows == 0 and d // LANES == SUBLANES
    anyspec = pl.BlockSpec(memory_space=pl.ANY)
    return pl.pallas_call(
        functools.partial(_moe_kernel, cap=cap, n_f=n_f, y_rows=y_rows),
        grid=(bsz, n_e, n_f),
        in_specs=[anyspec, anyspec, anyspec,
                  pl.BlockSpec((None, 6, d), lambda b, e, f: (b, 0, 0)),
                  pl.BlockSpec((None, d, fcw), lambda b, e, f: (e, 0, f)),
                  pl.BlockSpec((None, d, fcw), lambda b, e, f: (e, 0, f)),
                  pl.BlockSpec((None, ff, d), lambda b, e, f: (e, 0, 0))],
        out_specs=anyspec,
        out_shape=jax.ShapeDtypeStruct((bsz, s, d // LANES, LANES), F32),
        scratch_shapes=[pltpu.SMEM((n_e * cap,), I32), pltpu.SMEM((s,), F32),
                        pltpu.VMEM((cap, nt, LANES), F32), pltpu.VMEM((cap, nt, LANES), F32),
                        pltpu.VMEM((n_f, cap, fcw), BF16), pltpu.VMEM((cap // y_rows, y_rows, d), F32),
                        pltpu.VMEM((s, nt, LANES), F32),
                        pltpu.SemaphoreType.DMA(()), pltpu.SemaphoreType.DMA(()),
                        pltpu.SemaphoreType.DMA((2,)), pltpu.SemaphoreType.DMA(())],
        compiler_params=_params(("arbitrary", "arbitrary", "arbitrary"), MOE_VMEM_LIMIT),
        name="moe_experts",
    )(idx.reshape(bsz, n_e * cap), aff, x1, mod, wg, wu, wd)


def _ln2_kernel(x_ref, y_ref, g2_ref, g_ref, b_ref, o_ref):
    r = ALPHA * x_ref[...] + (1.0 + g2_ref[...]) * y_ref[...]
    o_ref[...] = _layer_norm(r.reshape(o_ref.shape), g_ref[...], b_ref[...])


def _ln2_call(x1, moe, mod, g, b, tm):
    bsz, s, nt, _ = x1.shape
    d = nt * LANES
    row = pl.BlockSpec((None, tm, d), lambda bb, i: (bb, i, 0))
    tiles = pl.BlockSpec((None, tm, nt, LANES), lambda bb, i: (bb, i, 0, 0))
    vec = pl.BlockSpec((1, d), lambda bb, i: (0, 0))
    return pl.pallas_call(
        _ln2_kernel, grid=(bsz, s // tm),
        in_specs=[tiles, tiles, pl.BlockSpec((None, 1, nt, LANES), lambda bb, i: (bb, 0, 0, 0)), vec, vec],
        out_specs=row, out_shape=jax.ShapeDtypeStruct((bsz, s, d), F32),
        compiler_params=_params(("parallel", "parallel")), name="ffn_postnorm",
    )(x1, moe, mod[:, 5].reshape(bsz, 1, nt, LANES), g.reshape(1, d), b.reshape(1, d))


def _ffn_block(x1, aff, mod, wg, wu, wd, ln_g, ln_b, tm):
    s = x1.shape[1]
    cap = (EC_FACTOR * s) // N_EXPERTS
    idx = _topk_call(aff, cap)
    moe = _moe_call(x1, mod, idx, aff, wg.astype(BF16), wu.astype(BF16), wd.astype(BF16), fcw=512)
    return _ln2_call(x1, moe, mod, ln_g, ln_b, tm)


def kernel(x, c, positions, ada_w, ada_b, ab_w_in, ab_conv_w, ab_a_log, ab_dt_bias, ab_dn_norm, ab_w_out, swa_w_in,
           swa_sinks, swa_w_out, ln_mix_g, ln_mix_b, router_w, moe_w_gate, moe_w_up, moe_w_down, ln_ffn_g, ln_ffn_b):
    bsz, s, d = x.shape
    tm = min(512, s)
    mod = _mod_call(c, ada_w, ada_b).reshape(DEPTH, bsz, 6, d)
    rope = _rope_tables(positions)

    w_in = ab_w_in[0]
    n_a = 4 * DN_WIDTH
    n_g = 4 * DN_HEADS
    gw = DIL_GROUP_WIDTH
    q_scale = HEAD_DIM ** -0.5
    cols = [w_in[:, :n_a], jnp.pad(w_in[:, n_a:n_a + n_g], ((0, 0), (0, LANES - n_g)))]
    groups = [(0, 3 * DN_WIDTH, 0, False), (3 * DN_WIDTH, DN_WIDTH, 1, False), (n_a, LANES, 2, False)]
    for gi in range(len(DIL_PAIRS)):
        for part in range(3):
            c0 = n_a + n_g + part * DIL_WIDTH + gi * gw
            cols.append(w_in[:, c0:c0 + gw] * (q_scale if part == 0 else 1.0))
            groups.append((n_a + LANES + (3 * gi + part) * gw, gw, 3 + 3 * gi + part, part < 2))
    w0 = jnp.concatenate(cols, axis=1).astype(BF16)
    dils = tuple(dil for _, dil in DIL_PAIRS for _ in range(3))
    outs0 = _inproj_call(
        x, mod[0], w0, rope, _chunk_plan(groups), (3 * DN_WIDTH, DN_WIDTH, LANES) + (gw,) * 9,
        (F32, F32, F32) + (BF16,) * 9, (1, 1, 1) + dils, tm, "inproj_deltanet_dilated")
    qkv_a, z, gates = outs0[:3]
    u, w, qd, kd, at, egl = _dn_prep_call(qkv_a, gates, ab_conv_w[0], ab_a_log[0], ab_dt_bias[0], tm)
    o_scan = _dn_scan_call(u, w, qd, kd, at, egl, min(256, s))
    ogs, lses = [], []
    for gi, (window, dil) in enumerate(DIL_PAIRS):
        qkv_g = [a.reshape(bsz, dil, s // dil, gw) for a in outs0[3 + 3 * gi:6 + 3 * gi]]
        o_g, lse_g = _dilated_group(*qkv_g, window, dil)
        ogs.append(o_g)
        lses.append(lse_g)
    x1, aff = _outproj0_call(o_scan, z, ab_dn_norm[0], ogs, lses, ab_w_out[0].astype(BF16), x, mod[0],
                             ln_mix_g[0], ln_mix_b[0], router_w[0], tm)
    x = _ffn_block(x1, aff, mod[0], moe_w_gate[0], moe_w_up[0], moe_w_down[0], ln_ffn_g[0], ln_ffn_b[0], tm)

    qw = SWA_Q_HEADS * HEAD_DIM
    kw = SWA_KV_HEADS * HEAD_DIM
    plan1 = _chunk_plan([(0, qw, 0, True), (qw, kw, 1, True), (qw + kw, kw, 2, False)])
    w1 = jnp.concatenate([swa_w_in[0][:, :qw] * q_scale, swa_w_in[0][:, qw:]], axis=1).astype(BF16)
    q1, k1, v1 = _inproj_call(x, mod[1], w1, rope, plan1, (qw, kw, kw), (BF16, BF16, BF16), (1, 1, 1), tm,
                              "inproj_swa")
    (o1,) = _band_attn_call(q1[:, None], k1[:, None], v1[:, None], n_kv=SWA_KV_HEADS,
                            grp=SWA_Q_HEADS // SWA_KV_HEADS, blk=SWA_WINDOW, sinks=swa_sinks[0], with_lse=False,
                            name="swa_attn")
    x1, aff = _outproj1_call(o1[:, 0], swa_w_out[0].astype(BF16), x, mod[1], ln_mix_g[1], ln_mix_b[1], router_w[1],
                             tm)
    x = _ffn_block(x1, aff, mod[1], moe_w_gate[1], moe_w_up[1], moe_w_down[1], ln_ffn_g[1], ln_ffn_b[1], tm)
    return x
```

```python
import functools
import math

import jax
import jax.numpy as jnp
from jax import lax
from jax.experimental import pallas as pl
from jax.experimental.pallas import tpu as pltpu

F32 = jnp.float32
BF16 = jnp.bfloat16
I32 = jnp.int32
HI = lax.Precision.HIGHEST

DEPTH = 2
HEAD_DIM = 64
ROT_DIM = HEAD_DIM // 4
ROPE_THETA = 500000.0
DN_HEADS = 4
DN_HEAD_DIM = 128
DN_CHUNK = 64
DN_CONV = 5
DN_WIDTH = DN_HEADS * DN_HEAD_DIM
DIL_PAIRS = ((128, 1), (512, 4), (2048, 16))
DIL_HEADS_PER_GROUP = 4
DIL_GROUP_WIDTH = DIL_HEADS_PER_GROUP * HEAD_DIM
DIL_WIDTH = DIL_GROUP_WIDTH * len(DIL_PAIRS)
SWA_Q_HEADS = 16
SWA_KV_HEADS = 4
SWA_WINDOW = 128
N_EXPERTS = 16
EC_FACTOR = 2
ALPHA = (2.0 * DEPTH) ** 0.25
LN_EPS = 1e-5
NORM_EPS = 1e-6
NEG = -1e30
LANES = 128
HALO = 8
VMEM_LIMIT = 56 * 1024 * 1024
MOE_VMEM_LIMIT = 60 * 1024 * 1024


def _dot(a, b, prec=None):
    return jnp.dot(a, b, preferred_element_type=F32, precision=prec)


def _dot_nt(a, b, prec=None):
    return lax.dot_general(a, b, (((1,), (1,)), ((), ())), preferred_element_type=F32, precision=prec)


def _dot_tn(a, b, prec=None):
    return lax.dot_general(a, b, (((0,), (0,)), ((), ())), preferred_element_type=F32, precision=prec)


def _silu(x):
    return x * jax.nn.sigmoid(x)


def _params(sem, vmem_limit=VMEM_LIMIT):
    return pltpu.CompilerParams(dimension_semantics=sem, vmem_limit_bytes=vmem_limit)


def _mod_kernel(c_ref, w_ref, b_ref, o_ref):
    o_ref[...] = _dot(_silu(c_ref[...]), w_ref[...], HI) + b_ref[...]


def _mod_call(c, ada_w, ada_b):
    depth, d, n6 = ada_w.shape
    bsz = c.shape[0]
    tn = n6 // 4
    return pl.pallas_call(
        _mod_kernel,
        grid=(depth, n6 // tn),
        in_specs=[pl.BlockSpec((bsz, d), lambda i, j: (0, 0)),
                  pl.BlockSpec((None, d, tn), lambda i, j: (i, 0, j)),
                  pl.BlockSpec((None, 1, tn), lambda i, j: (i, 0, j))],
        out_specs=pl.BlockSpec((None, bsz, tn), lambda i, j: (i, 0, j)),
        out_shape=jax.ShapeDtypeStruct((depth, bsz, n6), F32),
        compiler_params=_params(("arbitrary", "arbitrary")),
        name="adaln_mod",
    )(c, ada_w, ada_b.reshape(depth, 1, n6))


def _inproj_kernel(x_ref, mod_ref, w_ref, rope_ref, *refs, plan):
    *out_refs, regroup_ref = refs
    m = mod_ref[...]
    h = (x_ref[...] * (1.0 + m[1:2]) + m[0:1]).astype(BF16)
    half = ROT_DIM // 2
    if any(p[4] for p in plan):
        src = lax.broadcasted_iota(I32, (ROT_DIM, LANES), 0)
        lane = lax.broadcasted_iota(I32, (ROT_DIM, LANES), 1) & (HEAD_DIM - 1)
        pick_cos = jnp.where(lane < ROT_DIM, jnp.where((lane & (half - 1)) == src, 1.0, 0.0), 0.0)
        pick_sin = jnp.where(lane < half, jnp.where(src == lane + half, -1.0, 0.0),
                             jnp.where(lane < ROT_DIM, jnp.where(src == lane, 1.0, 0.0), 0.0))
        cs = rope_ref[...]
        lane1 = lax.broadcasted_iota(I32, (1, LANES), 1) & (HEAD_DIM - 1)
        cos_t = _dot(cs, pick_cos, HI) + jnp.where(lane1 < ROT_DIM, 0.0, 1.0)
        sin_t = _dot(cs, pick_sin, HI)
        sin_a = jnp.where(lane1 < half, sin_t, 0.0)
        sin_b = sin_t - sin_a
    for c0, width, oi, o0, rope in plan:
        acc = _dot(h, w_ref[:, c0:c0 + width])
        if rope:
            reps = width // LANES
            tile = lambda a: jnp.concatenate([a] * reps, axis=1)
            acc = (acc * tile(cos_t) + pltpu.roll(acc, width - half, 1) * tile(sin_a)
                   + pltpu.roll(acc, half, 1) * tile(sin_b))
        out = out_refs[oi]
        if len(out.shape) == 2:
            out[:, o0:o0 + width] = acc.astype(out.dtype)
        else:
            dil, n = out.shape[0], out.shape[1]
            for c in range(width // LANES):
                regroup_ref[c] = acc[:, c * LANES:(c + 1) * LANES]
            for r in range(dil):
                out[r] = jnp.concatenate([regroup_ref[c, pl.ds(r, n, stride=dil), :] for c in range(width // LANES)],
                                         axis=1).astype(out.dtype)


def _inproj_call(x, mod, w, rope, plan, out_widths, out_dtypes, out_dils, tm, name):
    bsz, s, d = x.shape
    n = w.shape[1]
    out_shape, out_specs = [], []
    for ow, od, dil in zip(out_widths, out_dtypes, out_dils):
        if dil == 1:
            out_shape.append(jax.ShapeDtypeStruct((bsz, s, ow), od))
            out_specs.append(pl.BlockSpec((None, tm, ow), lambda b, i: (b, i, 0)))
        else:
            out_shape.append(jax.ShapeDtypeStruct((bsz, dil, s // dil, ow), od))
            out_specs.append(pl.BlockSpec((None, dil, tm // dil, ow), lambda b, i: (b, 0, i, 0)))
    chunk_w = max(p[1] for p in plan)
    return pl.pallas_call(
        functools.partial(_inproj_kernel, plan=plan),
        grid=(bsz, s // tm),
        in_specs=[pl.BlockSpec((None, tm, d), lambda b, i: (b, i, 0)),
                  pl.BlockSpec((None, 6, d), lambda b, i: (b, 0, 0)),
                  pl.BlockSpec((d, n), lambda b, i: (0, 0)),
                  pl.BlockSpec((None, tm, ROT_DIM), lambda b, i: (b, i, 0))],
        out_specs=out_specs,
        out_shape=out_shape,
        scratch_shapes=[pltpu.VMEM((chunk_w // LANES, tm, LANES), F32)],
        compiler_params=_params(("parallel", "parallel")),
        name=name,
    )(x, mod, w, rope)


def _chunk_plan(groups, chunk=256):
    plan = []
    for c0, width, oi, rope in groups:
        off = 0
        while off < width:
            wd = min(chunk, width - off)
            plan.append((c0 + off, wd, oi, off, rope))
            off += wd
    return tuple(plan)


def _rope_tables(positions):
    inv_freq = jnp.power(ROPE_THETA, -jnp.arange(0, ROT_DIM, 2, dtype=F32) / ROT_DIM)
    ang = positions.astype(F32)[..., None] * inv_freq
    return jnp.concatenate([jnp.cos(ang), jnp.sin(ang)], -1)


def _dn_prep_kernel(xa_ref, top_ref, bot_ref, gt_ref, cw_ref, alog_ref, dtb_ref,
                    u_ref, w_ref, qd_ref, kd_ref, at_ref, egl_ref,
                    q_s, k_s, v_s, gc_s, *, t):
    ch = DN_CHUNK
    dk = DN_HEAD_DIM
    pad = (DN_CONV - 1) // 2
    for grp, dst in enumerate((q_s, k_s, v_s)):
        cols = slice(grp * DN_WIDTH, (grp + 1) * DN_WIDTH)
        xe = jnp.concatenate([top_ref[:, cols], xa_ref[:, cols], bot_ref[:, cols]], axis=0)
        y = jnp.zeros((t, DN_WIDTH), F32)
        for k in range(DN_CONV):
            y = y + xe[HALO - pad + k:HALO - pad + k + t, :] * cw_ref[k:k + 1, cols]
        y = _silu(y)
        if grp < 2:
            scale = dk ** -0.5 if grp == 0 else 1.0
            parts = []
            for h in range(DN_HEADS):
                yh = y[:, h * dk:(h + 1) * dk]
                parts.append(yh * lax.rsqrt(jnp.sum(yh * yh, -1, keepdims=True) + NORM_EPS) * scale)
            y = jnp.concatenate(parts, axis=1)
        dst[...] = y

    g = gt_ref[...]
    lane = lax.broadcasted_iota(I32, (t, LANES), 1)
    z = g + dtb_ref[...]
    softplus = jnp.maximum(z, 0.0) + jnp.log1p(jnp.exp(-jnp.abs(z)))
    dec = -jnp.exp(alog_ref[...]) * softplus
    gv = jnp.where(lane < 2 * DN_HEADS, dec, jnp.where(lane < 4 * DN_HEADS, jax.nn.sigmoid(g), 0.0))
    ri = lax.broadcasted_iota(I32, (t, t), 0)
    ci = lax.broadcasted_iota(I32, (t, t), 1)
    shift = int(math.log2(ch))
    same = (ri >> shift) == (ci >> shift)
    pre = jnp.where(same & (ci <= ri), 1.0, 0.0).astype(F32)
    suf = jnp.where(same & (ci >= ri), 1.0, 0.0).astype(F32)
    gcf = _dot(pre, gv, HI)
    gcb = _dot(suf, gv, HI)
    gc_s[...] = jnp.where(lane < DN_HEADS, gcf, jnp.where(lane < 2 * DN_HEADS, gcb, gv))

    rr = lax.broadcasted_iota(I32, (ch, 2 * ch), 0)
    cc = lax.broadcasted_iota(I32, (ch, 2 * ch), 1)
    fwd = cc < ch
    cj = jnp.where(fwd, cc, cc - ch)
    ahead = jnp.where(fwd, cj - rr, rr - cj)
    incl = ahead <= 0
    strict = ahead < 0
    eye2 = jnp.where(cj == rr, 1.0, 0.0).astype(F32)

    def blockdiag(p):
        return jnp.concatenate([jnp.where(fwd, p, 0.0), jnp.where(fwd, 0.0, p)], axis=0).astype(BF16)

    n_sq = int(math.log2(ch)) - 1
    per_iter = math.gcd(t // ch, 8)

    def chunk_body(ci, carry):
        units = []
        for sub in range(per_iter):
            c = ci * per_iter + sub
            rows = pl.ds(pl.multiple_of(c * ch, ch), ch)
            gcc = gc_s[rows, :]
            gct = gcc.T
            for h in range(DN_HEADS):
                units.append((c, rows, gcc, gct, h))

        st, pws = [], []
        for c, rows, gcc, gct, h in units:
            bcast = lambda col, gcc=gcc: jnp.broadcast_to(gcc[:, col:col + 1], (ch, LANES))
            gf, gb = bcast(h), bcast(DN_HEADS + h)
            bf, bb = bcast(2 * DN_HEADS + h), bcast(3 * DN_HEADS + h)
            grow = jnp.concatenate([gct[h:h + 1, :], gct[DN_HEADS + h:DN_HEADS + h + 1, :]], axis=1)
            diff = jnp.where(fwd, gf, gb) - grow
            decay = jnp.where(incl, jnp.exp(jnp.where(incl, diff, 0.0)), 0.0)
            hs = slice(h * dk, (h + 1) * dk)
            k16 = k_s[rows, hs].astype(BF16)
            kk = jnp.concatenate([k16, k16], axis=0)
            lower = jnp.where(strict, jnp.where(fwd, bf, bb) * _dot_nt(k16, kk) * decay, 0.0)
            intra = (_dot_nt(q_s[rows, hs].astype(BF16), kk) * decay).astype(BF16)
            at_ref[0, rows, h * ch:(h + 1) * ch] = intra[:, :ch]
            at_ref[1, rows, h * ch:(h + 1) * ch] = intra[:, ch:]
            st.append((gf, gb, bf, bb))
            pws.append(-lower)
        ainvs = [eye2 + p for p in pws]
        pbds = [blockdiag(p) for p in pws]
        for _ in range(n_sq):
            pws = [_dot(p.astype(BF16), bd) for p, bd in zip(pws, pbds)]
            pbds = [blockdiag(p) for p in pws]
            ainvs = [a + _dot(a.astype(BF16), bd) for a, bd in zip(ainvs, pbds)]

        egl_f, egl_b = [], []
        for (c, rows, _, _, h), (gf, gb, bf, bb), ainv in zip(units, st, ainvs):
            hs = slice(h * dk, (h + 1) * dk)
            q = q_s[rows, hs]
            k = k_s[rows, hs]
            v = v_s[rows, hs]
            egf, egb = jnp.exp(gf), jnp.exp(gb)
            rhs = jnp.concatenate([jnp.concatenate([v * bf, k * bf * egf], axis=1),
                                   jnp.concatenate([v * bb, k * bb * egb], axis=1)], axis=0).astype(BF16)
            uw_f = _dot(jnp.where(fwd, ainv, 0.0).astype(BF16), rhs)
            uw_b = _dot(jnp.where(fwd, 0.0, ainv).astype(BF16), rhs)
            glf = jnp.broadcast_to(gf[ch - 1:ch, :], (ch, LANES))
            glb = jnp.broadcast_to(gb[0:1, :], (ch, LANES))
            u_ref[0, rows, hs] = uw_f[:, :dk]
            u_ref[1, rows, hs] = uw_b[:, :dk]
            w_ref[0, rows, hs] = uw_f[:, dk:].astype(BF16)
            w_ref[1, rows, hs] = uw_b[:, dk:].astype(BF16)
            qd_ref[0, rows, hs] = (q * egf).astype(BF16)
            qd_ref[1, rows, hs] = (q * egb).astype(BF16)
            kd_ref[0, rows, hs] = (k * jnp.exp(glf - gf)).astype(BF16)
            kd_ref[1, rows, hs] = (k * jnp.exp(glb - gb)).astype(BF16)
            egl_f.append(jnp.exp(glf[0:1, :]))
            egl_b.append(jnp.exp(glb[0:1, :]))
            if h == DN_HEADS - 1:
                fill = [jnp.zeros((8 - DN_HEADS, LANES), F32)]
                egl_ref[0, c] = jnp.concatenate(egl_f + fill, axis=0)
                egl_ref[1, c] = jnp.concatenate(egl_b + fill, axis=0)
                egl_f, egl_b = [], []
        return carry

    lax.fori_loop(0, t // (ch * per_iter), chunk_body, 0)


def _dn_prep_call(qkv_a, gates, conv_w, a_log, dt_bias, t):
    bsz, s, cw = qkv_a.shape
    nt = s // t
    r = qkv_a.reshape(bsz, nt, t, cw)
    zero = jnp.zeros((bsz, 1, HALO, cw), F32)
    top = jnp.concatenate([zero, r[:, :-1, t - HALO:]], axis=1)
    bot = jnp.concatenate([r[:, 1:, :HALO], zero], axis=1)
    cwp = jnp.zeros((8, cw), F32).at[:DN_CONV].set(conv_w)
    alog = jnp.zeros((1, LANES), F32).at[0, :2 * DN_HEADS].set(a_log.reshape(-1))
    dtb = jnp.zeros((1, LANES), F32).at[0, :2 * DN_HEADS].set(dt_bias.reshape(-1))
    nch = s // DN_CHUNK
    wide = lambda dt, wd: jax.ShapeDtypeStruct((2, bsz, s, wd), dt)
    spec = lambda wd: pl.BlockSpec((2, None, t, wd), lambda b, i: (0, b, i, 0))
    return pl.pallas_call(
        functools.partial(_dn_prep_kernel, t=t),
        grid=(bsz, nt),
        in_specs=[pl.BlockSpec((None, t, cw), lambda b, i: (b, i, 0)),
                  pl.BlockSpec((None, None, HALO, cw), lambda b, i: (b, i, 0, 0)),
                  pl.BlockSpec((None, None, HALO, cw), lambda b, i: (b, i, 0, 0)),
                  pl.BlockSpec((None, t, LANES), lambda b, i: (b, i, 0)),
                  pl.BlockSpec((8, cw), lambda b, i: (0, 0)),
                  pl.BlockSpec((1, LANES), lambda b, i: (0, 0)),
                  pl.BlockSpec((1, LANES), lambda b, i: (0, 0))],
        out_specs=[spec(DN_WIDTH), spec(DN_WIDTH), spec(DN_WIDTH), spec(DN_WIDTH), spec(DN_HEADS * DN_CHUNK),
                   pl.BlockSpec((2, None, t // DN_CHUNK, 8, LANES), lambda b, i: (0, b, i, 0, 0))],
        out_shape=[wide(F32, DN_WIDTH), wide(BF16, DN_WIDTH), wide(BF16, DN_WIDTH), wide(BF16, DN_WIDTH),
                   wide(BF16, DN_HEADS * DN_CHUNK),
                   jax.ShapeDtypeStruct((2, bsz, nch, 8, LANES), F32)],
        scratch_shapes=[pltpu.VMEM((t, DN_WIDTH), F32), pltpu.VMEM((t, DN_WIDTH), F32),
                        pltpu.VMEM((t, DN_WIDTH), F32), pltpu.VMEM((t, LANES), F32)],
        compiler_params=_params(("parallel", "parallel")),
        name="deltanet_prep",
    )(qkv_a, top, bot, gates, cwp, alog, dtb)


def _dn_scan_kernel(u_ref, w_ref, qd_ref, kd_ref, at_ref, egl_ref, o_ref, st_ref, *, nc):
    ch = DN_CHUNK
    dk = DN_HEAD_DIM
    d = pl.program_id(0)

    @pl.when(pl.program_id(2) == 0)
    def _():
        st_ref[...] = jnp.zeros(st_ref.shape, st_ref.dtype)

    n_seq = u_ref.shape[0]
    units = [(sq, h) for sq in range(n_seq) for h in range(DN_HEADS)]
    hsl = [slice(h * dk, (h + 1) * dk) for h in range(DN_HEADS)]
    sts = [st_ref[sq * DN_HEADS + h] for sq, h in units]
    for j in range(nc):
        cc = j + d * (nc - 1 - 2 * j)
        rows = pl.ds(pl.multiple_of(cc * ch, ch), ch)
        egls = [egl_ref[sq, cc] for sq in range(n_seq)]
        sbs = [st.astype(BF16) for st in sts]
        vbs = [(u_ref[sq, rows, hsl[h]] - _dot(w_ref[sq, rows, hsl[h]], sb)).astype(BF16)
               for (sq, h), sb in zip(units, sbs)]
        qss = [_dot(qd_ref[sq, rows, hsl[h]], sb) for (sq, h), sb in zip(units, sbs)]
        sts = [st * egls[sq][h:h + 1, :] + _dot_tn(kd_ref[sq, rows, hsl[h]], vb)
               for (sq, h), st, vb in zip(units, sts, vbs)]
        for (sq, h), qs, vb in zip(units, qss, vbs):
            o_ref[sq, rows, hsl[h]] = qs + _dot(at_ref[sq, rows, h * ch:(h + 1) * ch], vb)
    for (sq, h), st in zip(units, sts):
        st_ref[sq * DN_HEADS + h] = st


def _dn_scan_call(u, w, qd, kd, at, egl, tc):
    _, bsz, s, wd = u.shape
    nb = s // tc
    nc = tc // DN_CHUNK
    n_seq = math.gcd(bsz, 2)

    def blk(d, n):
        return n + d * (nb - 1 - 2 * n)

    spec = lambda width: pl.BlockSpec((None, n_seq, tc, width), lambda d, b, n: (d, b, blk(d, n), 0))
    return pl.pallas_call(
        functools.partial(_dn_scan_kernel, nc=nc),
        grid=(2, bsz // n_seq, nb),
        in_specs=[spec(wd), spec(wd), spec(wd), spec(wd), spec(DN_HEADS * DN_CHUNK),
                  pl.BlockSpec((None, n_seq, nc, 8, LANES), lambda d, b, n: (d, b, blk(d, n), 0, 0))],
        out_specs=spec(wd),
        out_shape=jax.ShapeDtypeStruct((2, bsz, s, wd), F32),
        scratch_shapes=[pltpu.VMEM((n_seq * DN_HEADS, DN_HEAD_DIM, DN_HEAD_DIM), F32)],
        compiler_params=_params(("parallel", "parallel", "arbitrary")),
        name="deltanet_scan",
    )(u, w, qd, kd, at, egl)


def _band_attn_kernel(*refs, n_kv, grp, blk, tq, qs, t_len, with_sink, with_lse):
    q_ref, kp_ref, kc_ref, kn_ref, vp_ref, vc_ref, vn_ref = refs[:7]
    pos = 7
    sink_ref = None
    if with_sink:
        sink_ref = refs[pos]
        pos += 1
    o_ref = refs[pos]
    lse_ref = refs[pos + 1] if with_lse else None
    hd = HEAD_DIM
    i0 = pl.program_id(2) * tq
    kcat = jnp.concatenate([kp_ref[...], kc_ref[...], kn_ref[...]], axis=0)
    vcat = jnp.concatenate([vp_ref[...], vc_ref[...], vn_ref[...]], axis=0)
    kwin = qs + 2 * blk
    n_q = n_kv * grp
    n_sub = tq // qs
    lse_lane = lax.broadcasted_iota(I32, (qs, LSE_COLS), 1)
    ones_k = jnp.ones((kwin, LSE_COLS), BF16)
    lses = [jnp.zeros((qs, LSE_COLS), F32) for _ in range(n_sub)]
    outs = [[] for _ in range(n_sub)]
    biases, khs, vhs = [], [], []
    for sub in range(n_sub):
        k0 = sub * qs
        rowpos = i0 + k0 + lax.broadcasted_iota(I32, (qs, kwin), 0)
        keypos = i0 - blk + k0 + lax.broadcasted_iota(I32, (qs, kwin), 1)
        mask = (jnp.abs(keypos - rowpos) <= blk) & (keypos >= 0) & (keypos < t_len)
        biases.append(jnp.where(mask, 0.0, NEG))
        khs.append([kcat[k0:k0 + kwin, kv * hd:(kv + 1) * hd] for kv in range(n_kv)])
        vhs.append([vcat[k0:k0 + kwin, kv * hd:(kv + 1) * hd] for kv in range(n_kv)])
    units = [(sub, hq) for sub in range(n_sub) for hq in range(n_q)]
    for u0 in range(0, len(units), ATTN_UNITS_PER_STAGE):
        stage = units[u0:u0 + ATTN_UNITS_PER_STAGE]
        scs = [_dot_nt(q_ref[sub * qs:(sub + 1) * qs, hq * hd:(hq + 1) * hd], khs[sub][hq // grp]) + biases[sub]
               for sub, hq in stage]
        ps, ms = [], []
        for (sub, hq), sc in zip(stage, scs):
            m = jnp.max(sc, axis=-1, keepdims=True)
            if with_sink:
                m = jnp.maximum(m, sink_ref[hq])
            ps.append(jnp.exp((sc - m).astype(BF16)))
            ms.append(m)
        dens = [_dot(p, ones_k) for p in ps]
        for (sub, hq), p, m, den in zip(stage, ps, ms, dens):
            if with_sink:
                den = den + jnp.exp(sink_ref[hq] - m)
            if with_lse:
                lses[sub] = jnp.where(lse_lane == hq, m + jnp.log(den), lses[sub])
            outs[sub].append(_dot(p, vhs[sub][hq // grp]) / den[:, :hd])
    for sub in range(n_sub):
        o_ref[sub * qs:(sub + 1) * qs, :] = jnp.concatenate(outs[sub], axis=1).astype(o_ref.dtype)
        if with_lse:
            lse_ref[sub * qs:(sub + 1) * qs, :] = lses[sub]


LSE_COLS = LANES
ATTN_UNITS_PER_STAGE = 16


def _band_attn_call(q, k, v, *, n_kv, grp, blk, sinks, with_lse, name):
    bsz, n_res, t_len, _ = q.shape
    tq = min(512, t_len)
    qs = min(128, tq)
    nt = t_len // tq
    ratio = tq // blk
    nblk = t_len // blk
    qw = n_kv * grp * HEAD_DIM
    kw = n_kv * HEAD_DIM
    cur = lambda wd: pl.BlockSpec((None, None, tq, wd), lambda b, r, i: (b, r, i, 0))
    prev = lambda wd: pl.BlockSpec((None, None, blk, wd), lambda b, r, i: (b, r, jnp.maximum(i * ratio - 1, 0), 0))
    nxt = lambda wd: pl.BlockSpec((None, None, blk, wd),
                                  lambda b, r, i: (b, r, jnp.minimum((i + 1) * ratio, nblk - 1), 0))
    in_specs = [cur(qw), prev(kw), cur(kw), nxt(kw), prev(kw), cur(kw), nxt(kw)]
    args = [q, k, k, k, v, v, v]
    if sinks is not None:
        in_specs.append(pl.BlockSpec(memory_space=pltpu.SMEM))
        args.append(sinks)
    out_shape = [jax.ShapeDtypeStruct((bsz, n_res, t_len, qw), BF16)]
    out_specs = [cur(qw)]
    if with_lse:
        out_shape.append(jax.ShapeDtypeStruct((bsz, n_res, t_len, LSE_COLS), F32))
        out_specs.append(cur(LSE_COLS))
    return pl.pallas_call(
        functools.partial(_band_attn_kernel, n_kv=n_kv, grp=grp, blk=blk, tq=tq, qs=qs, t_len=t_len,
                          with_sink=sinks is not None, with_lse=with_lse),
        grid=(bsz, n_res, nt),
        in_specs=in_specs,
        out_specs=out_specs,
        out_shape=out_shape,
        compiler_params=_params(("parallel", "parallel", "parallel")),
        name=name,
    )(*args)


def _dilated_group(qg, kg, vg, window, dil):
    return _band_attn_call(qg, kg, vg, n_kv=DIL_HEADS_PER_GROUP, grp=1, blk=window // (2 * dil), sinks=None,
                           with_lse=True, name=f"dilated_attn_{dil}")


def _layer_norm(r, g, b):
    mu = jnp.mean(r, -1, keepdims=True)
    var = jnp.mean(jnp.square(r - mu), -1, keepdims=True)
    return (r - mu) * lax.rsqrt(var + LN_EPS) * g + b


def _post_mix(y, x_ref, mod_ref, lng_ref, lnb_ref, rwt_ref, x1_ref, aff_ref):
    m = mod_ref[...]
    x1 = _layer_norm(ALPHA * x_ref[...] + (1.0 + m[2:3]) * y, lng_ref[...], lnb_ref[...])
    x1_ref[...] = x1.reshape(x1_ref.shape)
    h2 = x1 * (1.0 + m[4:5]) + m[3:4]
    split = lambda a: (a.astype(BF16), (a - a.astype(BF16).astype(F32)).astype(BF16))
    (w_hi, w_lo), (h_hi, h_lo) = split(rwt_ref[...]), split(h2)
    logits = _dot_nt(w_hi, h_hi) + (_dot_nt(w_hi, h_lo) + _dot_nt(w_lo, h_hi))
    e = jnp.exp(logits - jnp.max(logits, axis=0, keepdims=True))
    aff_ref[...] = e / jnp.sum(e, axis=0, keepdims=True)


def _position_major(ref, scratch):
    dil, n, wd = ref.shape
    if dil == 1:
        return ref[0].astype(F32)
    for r in range(dil):
        a = ref[r].astype(F32)
        for c in range(wd // LANES):
            scratch[c, pl.ds(r, n, stride=dil), :] = a[:, c * LANES:(c + 1) * LANES]
    return jnp.concatenate([scratch[c] for c in range(wd // LANES)], axis=1)


def _outproj0_kernel(of_ref, ob_ref, z_ref, dnn_ref, og0_ref, og1_ref, og2_ref, l0_ref, l1_ref, l2_ref,
                     w_ref, x_ref, mod_ref, lng_ref, lnb_ref, rwt_ref, x1_ref, aff_ref, regroup_ref):
    dk = DN_HEAD_DIM
    od = of_ref[...] + ob_ref[...]
    z = z_ref[...]
    parts = []
    for h in range(DN_HEADS):
        oh = od[:, h * dk:(h + 1) * dk]
        oh = oh * lax.rsqrt(jnp.mean(oh * oh, -1, keepdims=True) + NORM_EPS) * dnn_ref[...]
        parts.append(oh * _silu(z[:, h * dk:(h + 1) * dk]))
    o_dn = jnp.concatenate(parts, axis=1).astype(BF16)
    l0, l1, l2 = (_position_major(r, regroup_ref) for r in (l0_ref, l1_ref, l2_ref))
    og0, og1, og2 = (_position_major(r, regroup_ref) for r in (og0_ref, og1_ref, og2_ref))
    mx = jnp.maximum(jnp.maximum(l0, l1), l2)
    e0, e1, e2 = jnp.exp(l0 - mx), jnp.exp(l1 - mx), jnp.exp(l2 - mx)
    den = e0 + e1 + e2
    head = lax.broadcasted_iota(I32, (l0.shape[0], DIL_GROUP_WIDTH), 1) >> int(math.log2(HEAD_DIM))

    def per_lane(wt):
        out = jnp.zeros(head.shape, F32)
        for h in range(DIL_HEADS_PER_GROUP):
            out = jnp.where(head == h, wt[:, h:h + 1], out)
        return out

    o_dil = (per_lane(e0 / den) * og0 + per_lane(e1 / den) * og1 + per_lane(e2 / den) * og2).astype(BF16)
    y = _dot(o_dn, w_ref[0:DN_WIDTH, :]) + _dot(o_dil, w_ref[DN_WIDTH:DN_WIDTH + DIL_GROUP_WIDTH, :])
    _post_mix(y, x_ref, mod_ref, lng_ref, lnb_ref, rwt_ref, x1_ref, aff_ref)


def _outproj1_kernel(o_ref, w_ref, x_ref, mod_ref, lng_ref, lnb_ref, rwt_ref, x1_ref, aff_ref):
    y = _dot(o_ref[...], w_ref[...])
    _post_mix(y, x_ref, mod_ref, lng_ref, lnb_ref, rwt_ref, x1_ref, aff_ref)


def _tail_specs(bsz, s, d, tm, n_e):
    row = lambda wd: pl.BlockSpec((None, tm, wd), lambda b, i: (b, i, 0))
    const = lambda shp: pl.BlockSpec(shp, lambda b, i: tuple(0 for _ in shp))
    in_specs = [row(d), pl.BlockSpec((None, 6, d), lambda b, i: (b, 0, 0)), const((1, d)), const((1, d)),
                const((n_e, d))]
    out_specs = [pl.BlockSpec((None, tm, d // LANES, LANES), lambda b, i: (b, i, 0, 0)),
                 pl.BlockSpec((None, n_e, tm), lambda b, i: (b, 0, i))]
    out_shape = [jax.ShapeDtypeStruct((bsz, s, d // LANES, LANES), F32), jax.ShapeDtypeStruct((bsz, n_e, s), F32)]
    return in_specs, out_specs, out_shape


def _outproj0_call(o_scan, z, dn_norm, ogs, lses, w_out, x, mod, ln_g, ln_b, router_w, tm):
    bsz, s, d = x.shape
    n_e = router_w.shape[1]
    tail_in, out_specs, out_shape = _tail_specs(bsz, s, d, tm, n_e)
    row = lambda wd: pl.BlockSpec((None, tm, wd), lambda b, i: (b, i, 0))
    dirspec = lambda dd: pl.BlockSpec((None, None, tm, DN_WIDTH), lambda b, i: (dd, b, i, 0))
    gw = DIL_GROUP_WIDTH
    resid = lambda a: pl.BlockSpec((None, a.shape[1], tm // a.shape[1], a.shape[3]), lambda b, i: (b, 0, i, 0))
    in_specs = [dirspec(0), dirspec(1), row(DN_WIDTH), pl.BlockSpec((1, DN_HEAD_DIM), lambda b, i: (0, 0))]
    in_specs += [resid(a) for a in ogs] + [resid(a) for a in lses]
    in_specs += [pl.BlockSpec(w_out.shape, lambda b, i: (0, 0))] + tail_in
    return pl.pallas_call(
        _outproj0_kernel, grid=(bsz, s // tm), in_specs=in_specs, out_specs=out_specs, out_shape=out_shape,
        scratch_shapes=[pltpu.VMEM((gw // LANES, tm, LANES), F32)],
        compiler_params=_params(("parallel", "parallel")), name="outproj_deltanet_dilated",
    )(o_scan, o_scan, z, dn_norm.reshape(1, -1), *ogs, *lses, w_out, x, mod, ln_g.reshape(1, d), ln_b.reshape(1, d),
      router_w.T)


def _outproj1_call(o, w_out, x, mod, ln_g, ln_b, router_w, tm):
    bsz, s, d = x.shape
    n_e = router_w.shape[1]
    tail_in, out_specs, out_shape = _tail_specs(bsz, s, d, tm, n_e)
    in_specs = [pl.BlockSpec((None, tm, o.shape[-1]), lambda b, i: (b, i, 0)),
                pl.BlockSpec(w_out.shape, lambda b, i: (0, 0))] + tail_in
    return pl.pallas_call(
        _outproj1_kernel, grid=(bsz, s // tm), in_specs=in_specs, out_specs=out_specs, out_shape=out_shape,
        compiler_params=_params(("parallel", "parallel")), name="outproj_swa",
    )(o, w_out, x, mod, ln_g.reshape(1, d), ln_b.reshape(1, d), router_w.T)


def _topk_kernel(a_ref, idx_ref, *, cap, jb):
    a = a_ref[...]
    n_e, rows, _ = a.shape
    bits = pltpu.bitcast(a, I32)
    thr = jnp.zeros((n_e, 1, 1), I32)
    for bit in range(30, -1, -1):
        cand = thr | (1 << bit)
        cnt = jnp.sum(jnp.where(bits >= cand, 1, 0), axis=(1, 2), keepdims=True)
        thr = jnp.where(cnt >= cap, cand, thr)
    gt = jnp.where(bits > thr, 1.0, 0.0).astype(F32)
    eq = jnp.where(bits == thr, 1.0, 0.0).astype(F32)
    need = cap - jnp.sum(gt, axis=(1, 2), keepdims=True)
    ru = lax.broadcasted_iota(I32, (LANES, LANES), 0)
    cu = lax.broadcasted_iota(I32, (LANES, LANES), 1)
    upper = jnp.where(ru <= cu, 1.0, 0.0).astype(BF16)
    rl = lax.broadcasted_iota(I32, (rows, rows), 0)
    cl = lax.broadcasted_iota(I32, (rows, rows), 1)
    lstrict = jnp.where(cl < rl, 1.0, 0.0).astype(BF16)

    def fold_cumsum(xs):
        withins = [_dot(x.astype(BF16), upper) for x in xs]
        rowtots = [jnp.broadcast_to(w[:, LANES - 1:LANES], w.shape) for w in withins]
        befores = [_dot(lstrict, r.astype(BF16)) for r in rowtots]
        return [w + b for w, b in zip(withins, befores)], [b + r for b, r in zip(befores, rowtots)]

    experts = range(n_e)
    eq_counts, _ = fold_cumsum([eq[e] for e in experts])
    sels = [jnp.maximum(gt[e], jnp.where(eq_counts[e] - eq[e] < need[e], eq[e], 0.0)) for e in experts]
    counts, count_ends = fold_cumsum(sels)
    rowid = lax.broadcasted_iota(I32, (rows, jb), 0).astype(F32)
    for j0 in range(0, cap, jb):
        slot = (j0 + lax.broadcasted_iota(I32, (1, jb), 1)).astype(F32)
        row = [jnp.sum(jnp.where(count_ends[e][:, 0:1] <= slot, 1.0, 0.0), axis=0, keepdims=True) for e in experts]
        onehot = [jnp.where(rowid == row[e], 1.0, 0.0) for e in experts]
        count_row = [_dot_tn(counts[e], onehot[e], HI) for e in experts]
        for e in experts:
            lane = jnp.sum(jnp.where(count_row[e] <= slot, 1.0, 0.0), axis=0, keepdims=True)
            idx_ref[e:e + 1, j0:j0 + jb] = (row[e] * LANES + lane).astype(I32)


def _topk_call(aff, cap):
    bsz, n_e, s = aff.shape
    rows = s // LANES
    return pl.pallas_call(
        functools.partial(_topk_kernel, cap=cap, jb=min(512, cap)),
        grid=(bsz,),
        in_specs=[pl.BlockSpec((None, n_e, rows, LANES), lambda b: (b, 0, 0, 0))],
        out_specs=pl.BlockSpec((None, n_e, cap), lambda b: (b, 0, 0)),
        out_shape=jax.ShapeDtypeStruct((bsz, n_e, cap), I32),
        compiler_params=_params(("parallel",)), name="topk_route",
    )(aff.reshape(bsz, n_e, rows, LANES))


SUBLANES = 8


def _moe_kernel(idx_hbm, aff_hbm, x_hbm, mod_ref, wg_ref, wu_ref, wd_ref, out_hbm,
                idx_s, aff_s, xg0, xg1, hid_s, y, acc, sem_i, sem_a, sem_g, sem_o, *, cap, n_f, y_bounds):
    b = pl.program_id(0)
    e = pl.program_id(1)
    f = pl.program_id(2)
    n_e = pl.num_programs(1)
    per_f = cap // n_f

    def row_copy(tok, j, dst, sem):
        return pltpu.make_async_copy(x_hbm.at[b, tok], dst.at[j], sem)

    def wait_rows(dst, sem):
        pltpu.make_async_copy(x_hbm.at[b, pl.ds(0, cap)], dst, sem).wait()

    @pl.when((e == 0) & (f == 0))
    def _():
        acc[...] = jnp.zeros(acc.shape, acc.dtype)
        ci = pltpu.make_async_copy(idx_hbm.at[b], idx_s, sem_i)
        ci.start()
        ci.wait()

        def gather(jo, carry):
            for r in range(SUBLANES):
                j = jo * SUBLANES + r
                row_copy(idx_s[j], j, xg0, sem_g.at[0]).start(priority=r % 2)
            return carry

        lax.fori_loop(0, cap // SUBLANES, gather, 0)
        wait_rows(xg0, sem_g.at[0])

    aff_copy = pltpu.make_async_copy(aff_hbm.at[b, e], aff_s, sem_a)

    @pl.when(f == 0)
    def _():
        aff_copy.start()

    def expert_step(xg_cur, xg_next, sem_next):
        next_base = jnp.minimum(e + 1, n_e - 1) * cap + f * per_f
        for r in range(per_f):
            row_copy(idx_s[next_base + r], f * per_f + r, xg_next, sem_next).start(priority=r % 2)
        m = mod_ref[...]
        xv = (xg_cur[...].reshape(cap, m.shape[-1]) * (1.0 + m[4:5]) + m[3:4]).astype(BF16)
        hid_s[f] = (_silu(_dot(xv, wg_ref[...])) * _dot(xv, wu_ref[...])).astype(BF16)

        @pl.when(f == n_f - 1)
        def _():
            aff_copy.wait()

            def scatter_rows(j0):
                yc = y[pl.ds(j0, SUBLANES), :].reshape((SUBLANES,) + acc.shape[1:])
                toks = [idx_s[e * cap + j0 + r] for r in range(SUBLANES)]
                olds = [acc[toks[r]] for r in range(SUBLANES)]
                for r in range(SUBLANES):
                    acc[toks[r]] = olds[r] + aff_s[toks[r]] * yc[r]

            for p, (r0, r1) in enumerate(zip(y_bounds[:-1], y_bounds[1:])):
                hid = jnp.concatenate([hid_s[i, r0:r1, :] for i in range(n_f)], axis=1)
                y[r0:r1, :] = _dot(hid, wd_ref[...])
                if p > 0:
                    for j0 in range(y_bounds[p - 1], r0, SUBLANES):
                        scatter_rows(j0)

            def scatter_last(jo, carry):
                scatter_rows(pl.multiple_of(y_bounds[-2] + jo * SUBLANES, SUBLANES))
                return carry

            lax.fori_loop(0, (y_bounds[-1] - y_bounds[-2]) // SUBLANES, scatter_last, 0)
            wait_rows(xg_next, sem_next)

            @pl.when(e == n_e - 1)
            def _():
                co = pltpu.make_async_copy(acc, out_hbm.at[b], sem_o)
                co.start()
                co.wait()

    @pl.when(e % 2 == 0)
    def _():
        expert_step(xg0, xg1, sem_g.at[1])

    @pl.when(e % 2 == 1)
    def _():
        expert_step(xg1, xg0, sem_g.at[0])


def _moe_call(x1, mod, idx, aff, wg, wu, wd, fcw):
    bsz, s, nt, _ = x1.shape
    d = nt * LANES
    n_e, _, ff = wg.shape
    cap = idx.shape[-1]
    fcw = min(fcw, ff)
    n_f = ff // fcw
    y_bounds = (0, cap // 2, cap // 2 + 3 * cap // 8, cap)
    assert n_e % 2 == 0 and cap % n_f == 0 and cap % (8 * SUBLANES) == 0 and d // LANES == SUBLANES
    anyspec = pl.BlockSpec(memory_space=pl.ANY)
    return pl.pallas_call(
        functools.partial(_moe_kernel, cap=cap, n_f=n_f, y_bounds=y_bounds),
        grid=(bsz, n_e, n_f),
        in_specs=[anyspec, anyspec, anyspec,
                  pl.BlockSpec((None, 6, d), lambda b, e, f: (b, 0, 0)),
                  pl.BlockSpec((None, d, fcw), lambda b, e, f: (e, 0, f)),
                  pl.BlockSpec((None, d, fcw), lambda b, e, f: (e, 0, f)),
                  pl.BlockSpec((None, ff, d), lambda b, e, f: (e, 0, 0))],
        out_specs=anyspec,
        out_shape=jax.ShapeDtypeStruct((bsz, s, d // LANES, LANES), F32),
        scratch_shapes=[pltpu.SMEM((n_e * cap,), I32), pltpu.SMEM((s,), F32),
                        pltpu.VMEM((cap, nt, LANES), F32), pltpu.VMEM((cap, nt, LANES), F32),
                        pltpu.VMEM((n_f, cap, fcw), BF16), pltpu.VMEM((cap, d), F32),
                        pltpu.VMEM((s, nt, LANES), F32),
                        pltpu.SemaphoreType.DMA(()), pltpu.SemaphoreType.DMA(()),
                        pltpu.SemaphoreType.DMA((2,)), pltpu.SemaphoreType.DMA(())],
        compiler_params=_params(("arbitrary", "arbitrary", "arbitrary"), MOE_VMEM_LIMIT),
        name="moe_experts",
    )(idx.reshape(bsz, n_e * cap), aff, x1, mod, wg, wu, wd)


def _ln2_kernel(x_ref, y_ref, g2_ref, g_ref, b_ref, o_ref):
    r = ALPHA * x_ref[...] + (1.0 + g2_ref[...]) * y_ref[...]
    o_ref[...] = _layer_norm(r.reshape(o_ref.shape), g_ref[...], b_ref[...])


def _ln2_call(x1, moe, mod, g, b, tm):
    bsz, s, nt, _ = x1.shape
    d = nt * LANES
    row = pl.BlockSpec((None, tm, d), lambda bb, i: (bb, i, 0))
    tiles = pl.BlockSpec((None, tm, nt, LANES), lambda bb, i: (bb, i, 0, 0))
    vec = pl.BlockSpec((1, d), lambda bb, i: (0, 0))
    return pl.pallas_call(
        _ln2_kernel, grid=(bsz, s // tm),
        in_specs=[tiles, tiles, pl.BlockSpec((None, 1, nt, LANES), lambda bb, i: (bb, 0, 0, 0)), vec, vec],
        out_specs=row, out_shape=jax.ShapeDtypeStruct((bsz, s, d), F32),
        compiler_params=_params(("parallel", "parallel")), name="ffn_postnorm",
    )(x1, moe, mod[:, 5].reshape(bsz, 1, nt, LANES), g.reshape(1, d), b.reshape(1, d))


def _ffn_block(x1, aff, mod, wg, wu, wd, ln_g, ln_b, tm):
    s = x1.shape[1]
    cap = (EC_FACTOR * s) // N_EXPERTS
    idx = _topk_call(aff, cap)
    moe = _moe_call(x1, mod, idx, aff, wg.astype(BF16), wu.astype(BF16), wd.astype(BF16), fcw=512)
    return _ln2_call(x1, moe, mod, ln_g, ln_b, tm)


def kernel(x, c, positions, ada_w, ada_b, ab_w_in, ab_conv_w, ab_a_log, ab_dt_bias, ab_dn_norm, ab_w_out, swa_w_in,
           swa_sinks, swa_w_out, ln_mix_g, ln_mix_b, router_w, moe_w_gate, moe_w_up, moe_w_down, ln_ffn_g, ln_ffn_b):
    bsz, s, d = x.shape
    tm = min(512, s)
    mod = _mod_call(c, ada_w, ada_b).reshape(DEPTH, bsz, 6, d)
    rope = _rope_tables(positions)

    w_in = ab_w_in[0]
    n_a = 4 * DN_WIDTH
    n_g = 4 * DN_HEADS
    gw = DIL_GROUP_WIDTH
    q_scale = HEAD_DIM ** -0.5
    cols = [w_in[:, :n_a], jnp.pad(w_in[:, n_a:n_a + n_g], ((0, 0), (0, LANES - n_g)))]
    groups = [(0, 3 * DN_WIDTH, 0, False), (3 * DN_WIDTH, DN_WIDTH, 1, False), (n_a, LANES, 2, False)]
    for gi in range(len(DIL_PAIRS)):
        for part in range(3):
            c0 = n_a + n_g + part * DIL_WIDTH + gi * gw
            cols.append(w_in[:, c0:c0 + gw] * (q_scale if part == 0 else 1.0))
            groups.append((n_a + LANES + (3 * gi + part) * gw, gw, 3 + 3 * gi + part, part < 2))
    w0 = jnp.concatenate(cols, axis=1).astype(BF16)
    dils = tuple(dil for _, dil in DIL_PAIRS for _ in range(3))
    outs0 = _inproj_call(
        x, mod[0], w0, rope, _chunk_plan(groups), (3 * DN_WIDTH, DN_WIDTH, LANES) + (gw,) * 9,
        (F32, F32, F32) + (BF16,) * 9, (1, 1, 1) + dils, tm, "inproj_deltanet_dilated")
    qkv_a, z, gates = outs0[:3]
    u, w, qd, kd, at, egl = _dn_prep_call(qkv_a, gates, ab_conv_w[0], ab_a_log[0], ab_dt_bias[0], tm)
    o_scan = _dn_scan_call(u, w, qd, kd, at, egl, min(256, s))
    ogs, lses = [], []
    for gi, (window, dil) in enumerate(DIL_PAIRS):
        qkv_g = [a.reshape(bsz, dil, s // dil, gw) for a in outs0[3 + 3 * gi:6 + 3 * gi]]
        o_g, lse_g = _dilated_group(*qkv_g, window, dil)
        ogs.append(o_g)
        lses.append(lse_g)
    x1, aff = _outproj0_call(o_scan, z, ab_dn_norm[0], ogs, lses, ab_w_out[0].astype(BF16), x, mod[0],
                             ln_mix_g[0], ln_mix_b[0], router_w[0], tm)
    x = _ffn_block(x1, aff, mod[0], moe_w_gate[0], moe_w_up[0], moe_w_down[0], ln_ffn_g[0], ln_ffn_b[0], tm)

    qw = SWA_Q_HEADS * HEAD_DIM
    kw = SWA_KV_HEADS * HEAD_DIM
    plan1 = _chunk_plan([(0, qw, 0, True), (qw, kw, 1, True), (qw + kw, kw, 2, False)])
    w1 = jnp.concatenate([swa_w_in[0][:, :qw] * q_scale, swa_w_in[0][:, qw:]], axis=1).astype(BF16)
    q1, k1, v1 = _inproj_call(x, mod[1], w1, rope, plan1, (qw, kw, kw), (BF16, BF16, BF16), (1, 1, 1), tm,
                              "inproj_swa")
    (o1,) = _band_attn_call(q1[:, None], k1[:, None], v1[:, None], n_kv=SWA_KV_HEADS,
                            grp=SWA_Q_HEADS // SWA_KV_HEADS, blk=SWA_WINDOW, sinks=swa_sinks[0], with_lse=False,
                            name="swa_attn")
    x1, aff = _outproj1_call(o1[:, 0], swa_w_out[0].astype(BF16), x, mod[1], ln_mix_g[1], ln_mix_b[1], router_w[1],
                             tm)
    x = _ffn_block(x1, aff, mod[1], moe_w_gate[1], moe_w_up[1], moe_w_down[1], ln_ffn_g[1], ln_ffn_b[1], tm)
    return x
```

```python
import functools
import math

import jax
import jax.numpy as jnp
from jax import lax
from jax.experimental import pallas as pl
from jax.experimental.pallas import tpu as pltpu

F32 = jnp.float32
BF16 = jnp.bfloat16
I32 = jnp.int32
HI = lax.Precision.HIGHEST

DEPTH = 2
HEAD_DIM = 64
ROT_DIM = HEAD_DIM // 4
ROPE_THETA = 500000.0
DN_HEADS = 4
DN_HEAD_DIM = 128
DN_CHUNK = 64
DN_CONV = 5
DN_WIDTH = DN_HEADS * DN_HEAD_DIM
DIL_PAIRS = ((128, 1), (512, 4), (2048, 16))
DIL_HEADS_PER_GROUP = 4
DIL_GROUP_WIDTH = DIL_HEADS_PER_GROUP * HEAD_DIM
DIL_WIDTH = DIL_GROUP_WIDTH * len(DIL_PAIRS)
SWA_Q_HEADS = 16
SWA_KV_HEADS = 4
SWA_WINDOW = 128
N_EXPERTS = 16
EC_FACTOR = 2
ALPHA = (2.0 * DEPTH) ** 0.25
LN_EPS = 1e-5
NORM_EPS = 1e-6
NEG = -1e30
LANES = 128
HALO = 8
VMEM_LIMIT = 56 * 1024 * 1024
MOE_VMEM_LIMIT = 60 * 1024 * 1024


def _dot(a, b, prec=None):
    return jnp.dot(a, b, preferred_element_type=F32, precision=prec)


def _dot_nt(a, b, prec=None):
    return lax.dot_general(a, b, (((1,), (1,)), ((), ())), preferred_element_type=F32, precision=prec)


def _dot_tn(a, b, prec=None):
    return lax.dot_general(a, b, (((0,), (0,)), ((), ())), preferred_element_type=F32, precision=prec)


def _silu(x):
    return x * jax.nn.sigmoid(x)


def _params(sem, vmem_limit=VMEM_LIMIT):
    return pltpu.CompilerParams(dimension_semantics=sem, vmem_limit_bytes=vmem_limit)


def _mod_kernel(c_ref, w_ref, b_ref, o_ref):
    o_ref[...] = _dot(_silu(c_ref[...]), w_ref[...], HI) + b_ref[...]


def _mod_call(c, ada_w, ada_b):
    depth, d, n6 = ada_w.shape
    bsz = c.shape[0]
    tn = n6 // 4
    return pl.pallas_call(
        _mod_kernel,
        grid=(depth, n6 // tn),
        in_specs=[pl.BlockSpec((bsz, d), lambda i, j: (0, 0)),
                  pl.BlockSpec((None, d, tn), lambda i, j: (i, 0, j)),
                  pl.BlockSpec((None, 1, tn), lambda i, j: (i, 0, j))],
        out_specs=pl.BlockSpec((None, bsz, tn), lambda i, j: (i, 0, j)),
        out_shape=jax.ShapeDtypeStruct((depth, bsz, n6), F32),
        compiler_params=_params(("arbitrary", "arbitrary")),
        name="adaln_mod",
    )(c, ada_w, ada_b.reshape(depth, 1, n6))


def _inproj_kernel(x_ref, mod_ref, w_ref, rope_ref, *refs, plan):
    *out_refs, regroup_ref = refs
    m = mod_ref[...]
    h = (x_ref[...] * (1.0 + m[1:2]) + m[0:1]).astype(BF16)
    half = ROT_DIM // 2
    if any(p[4] for p in plan):
        src = lax.broadcasted_iota(I32, (ROT_DIM, LANES), 0)
        lane = lax.broadcasted_iota(I32, (ROT_DIM, LANES), 1) & (HEAD_DIM - 1)
        pick_cos = jnp.where(lane < ROT_DIM, jnp.where((lane & (half - 1)) == src, 1.0, 0.0), 0.0)
        pick_sin = jnp.where(lane < half, jnp.where(src == lane + half, -1.0, 0.0),
                             jnp.where(lane < ROT_DIM, jnp.where(src == lane, 1.0, 0.0), 0.0))
        cs = rope_ref[...]
        lane1 = lax.broadcasted_iota(I32, (1, LANES), 1) & (HEAD_DIM - 1)
        cos_t = _dot(cs, pick_cos, HI) + jnp.where(lane1 < ROT_DIM, 0.0, 1.0)
        sin_t = _dot(cs, pick_sin, HI)
        sin_a = jnp.where(lane1 < half, sin_t, 0.0)
        sin_b = sin_t - sin_a
    for c0, width, oi, o0, rope in plan:
        acc = _dot(h, w_ref[:, c0:c0 + width])
        if rope:
            reps = width // LANES
            tile = lambda a: jnp.concatenate([a] * reps, axis=1)
            acc = (acc * tile(cos_t) + pltpu.roll(acc, width - half, 1) * tile(sin_a)
                   + pltpu.roll(acc, half, 1) * tile(sin_b))
        out = out_refs[oi]
        if len(out.shape) == 2:
            out[:, o0:o0 + width] = acc.astype(out.dtype)
        else:
            dil, n = out.shape[0], out.shape[1]
            for c in range(width // LANES):
                regroup_ref[c] = acc[:, c * LANES:(c + 1) * LANES]
            for r in range(dil):
                out[r] = jnp.concatenate([regroup_ref[c, pl.ds(r, n, stride=dil), :] for c in range(width // LANES)],
                                         axis=1).astype(out.dtype)


def _inproj_call(x, mod, w, rope, plan, out_widths, out_dtypes, out_dils, tm, name):
    bsz, s, d = x.shape
    n = w.shape[1]
    out_shape, out_specs = [], []
    for ow, od, dil in zip(out_widths, out_dtypes, out_dils):
        if dil == 1:
            out_shape.append(jax.ShapeDtypeStruct((bsz, s, ow), od))
            out_specs.append(pl.BlockSpec((None, tm, ow), lambda b, i: (b, i, 0)))
        else:
            out_shape.append(jax.ShapeDtypeStruct((bsz, dil, s // dil, ow), od))
            out_specs.append(pl.BlockSpec((None, dil, tm // dil, ow), lambda b, i: (b, 0, i, 0)))
    chunk_w = max(p[1] for p in plan)
    return pl.pallas_call(
        functools.partial(_inproj_kernel, plan=plan),
        grid=(bsz, s // tm),
        in_specs=[pl.BlockSpec((None, tm, d), lambda b, i: (b, i, 0)),
                  pl.BlockSpec((None, 6, d), lambda b, i: (b, 0, 0)),
                  pl.BlockSpec((d, n), lambda b, i: (0, 0)),
                  pl.BlockSpec((None, tm, ROT_DIM), lambda b, i: (b, i, 0))],
        out_specs=out_specs,
        out_shape=out_shape,
        scratch_shapes=[pltpu.VMEM((chunk_w // LANES, tm, LANES), F32)],
        compiler_params=_params(("parallel", "parallel")),
        name=name,
    )(x, mod, w, rope)


def _chunk_plan(groups, chunk=256):
    plan = []
    for c0, width, oi, rope in groups:
        off = 0
        while off < width:
            wd = min(chunk, width - off)
            plan.append((c0 + off, wd, oi, off, rope))
            off += wd
    return tuple(plan)


def _rope_tables(positions):
    inv_freq = jnp.power(ROPE_THETA, -jnp.arange(0, ROT_DIM, 2, dtype=F32) / ROT_DIM)
    ang = positions.astype(F32)[..., None] * inv_freq
    return jnp.concatenate([jnp.cos(ang), jnp.sin(ang)], -1)


def _dn_prep_kernel(xa_ref, top_ref, bot_ref, gt_ref, cw_ref, alog_ref, dtb_ref,
                    u_ref, w_ref, qd_ref, kd_ref, at_ref, egl_ref,
                    q_s, k_s, v_s, gc_s, *, t):
    ch = DN_CHUNK
    dk = DN_HEAD_DIM
    pad = (DN_CONV - 1) // 2
    has_top = pl.program_id(1) > 0
    has_bot = pl.program_id(1) < pl.num_programs(1) - 1
    for grp, dst in enumerate((q_s, k_s, v_s)):
        cols = slice(grp * DN_WIDTH, (grp + 1) * DN_WIDTH)
        xe = jnp.concatenate([jnp.where(has_top, top_ref[:, cols], 0.0), xa_ref[:, cols],
                              jnp.where(has_bot, bot_ref[:, cols], 0.0)], axis=0)
        y = jnp.zeros((t, DN_WIDTH), F32)
        for k in range(DN_CONV):
            y = y + xe[HALO - pad + k:HALO - pad + k + t, :] * cw_ref[k:k + 1, cols]
        y = _silu(y)
        if grp < 2:
            scale = dk ** -0.5 if grp == 0 else 1.0
            parts = []
            for h in range(DN_HEADS):
                yh = y[:, h * dk:(h + 1) * dk]
                parts.append(yh * lax.rsqrt(jnp.sum(yh * yh, -1, keepdims=True) + NORM_EPS) * scale)
            y = jnp.concatenate(parts, axis=1)
        dst[...] = y

    g = gt_ref[...]
    lane = lax.broadcasted_iota(I32, (t, LANES), 1)
    z = g + dtb_ref[...]
    softplus = jnp.maximum(z, 0.0) + jnp.log1p(jnp.exp(-jnp.abs(z)))
    dec = -jnp.exp(alog_ref[...]) * softplus
    gv = jnp.where(lane < 2 * DN_HEADS, dec, jnp.where(lane < 4 * DN_HEADS, jax.nn.sigmoid(g), 0.0))
    ri = lax.broadcasted_iota(I32, (t, t), 0)
    ci = lax.broadcasted_iota(I32, (t, t), 1)
    shift = int(math.log2(ch))
    same = (ri >> shift) == (ci >> shift)
    pre = jnp.where(same & (ci <= ri), 1.0, 0.0).astype(F32)
    suf = jnp.where(same & (ci >= ri), 1.0, 0.0).astype(F32)
    gcf = _dot(pre, gv, HI)
    gcb = _dot(suf, gv, HI)
    gc_s[...] = jnp.where(lane < DN_HEADS, gcf, jnp.where(lane < 2 * DN_HEADS, gcb, gv))

    rr = lax.broadcasted_iota(I32, (ch, 2 * ch), 0)
    cc = lax.broadcasted_iota(I32, (ch, 2 * ch), 1)
    fwd = cc < ch
    cj = jnp.where(fwd, cc, cc - ch)
    ahead = jnp.where(fwd, cj - rr, rr - cj)
    incl = ahead <= 0
    strict = ahead < 0
    eye2 = jnp.where(cj == rr, 1.0, 0.0).astype(F32)

    def blockdiag(p):
        return jnp.concatenate([jnp.where(fwd, p, 0.0), jnp.where(fwd, 0.0, p)], axis=0).astype(BF16)

    n_sq = int(math.log2(ch)) - 1
    per_iter = math.gcd(t // ch, 8)

    def chunk_body(ci, carry):
        units = []
        for sub in range(per_iter):
            c = ci * per_iter + sub
            rows = pl.ds(pl.multiple_of(c * ch, ch), ch)
            gcc = gc_s[rows, :]
            gct = gcc.T
            for h in range(DN_HEADS):
                units.append((c, rows, gcc, gct, h))

        st, pws = [], []
        for c, rows, gcc, gct, h in units:
            bcast = lambda col, gcc=gcc: jnp.broadcast_to(gcc[:, col:col + 1], (ch, LANES))
            gf, gb = bcast(h), bcast(DN_HEADS + h)
            bf, bb = bcast(2 * DN_HEADS + h), bcast(3 * DN_HEADS + h)
            grow = jnp.concatenate([gct[h:h + 1, :], gct[DN_HEADS + h:DN_HEADS + h + 1, :]], axis=1)
            diff = jnp.where(fwd, gf, gb) - grow
            decay = jnp.where(incl, jnp.exp(jnp.where(incl, diff, 0.0)), 0.0)
            hs = slice(h * dk, (h + 1) * dk)
            k16 = k_s[rows, hs].astype(BF16)
            kk = jnp.concatenate([k16, k16], axis=0)
            lower = jnp.where(strict, jnp.where(fwd, bf, bb) * _dot_nt(k16, kk) * decay, 0.0)
            intra = (_dot_nt(q_s[rows, hs].astype(BF16), kk) * decay).astype(BF16)
            at_ref[0, rows, h * ch:(h + 1) * ch] = intra[:, :ch]
            at_ref[1, rows, h * ch:(h + 1) * ch] = intra[:, ch:]
            st.append((gf, gb, bf, bb))
            pws.append(-lower)
        ainvs = [eye2 + p for p in pws]
        pbds = [blockdiag(p) for p in pws]
        for _ in range(n_sq):
            pws = [_dot(p.astype(BF16), bd) for p, bd in zip(pws, pbds)]
            pbds = [blockdiag(p) for p in pws]
            ainvs = [a + _dot(a.astype(BF16), bd) for a, bd in zip(ainvs, pbds)]

        egl_f, egl_b = [], []
        for (c, rows, _, _, h), (gf, gb, bf, bb), ainv in zip(units, st, ainvs):
            hs = slice(h * dk, (h + 1) * dk)
            q = q_s[rows, hs]
            k = k_s[rows, hs]
            v = v_s[rows, hs]
            egf, egb = jnp.exp(gf), jnp.exp(gb)
            rhs = jnp.concatenate([jnp.concatenate([v * bf, k * bf * egf], axis=1),
                                   jnp.concatenate([v * bb, k * bb * egb], axis=1)], axis=0).astype(BF16)
            uw_f = _dot(jnp.where(fwd, ainv, 0.0).astype(BF16), rhs)
            uw_b = _dot(jnp.where(fwd, 0.0, ainv).astype(BF16), rhs)
            glf = jnp.broadcast_to(gf[ch - 1:ch, :], (ch, LANES))
            glb = jnp.broadcast_to(gb[0:1, :], (ch, LANES))
            u_ref[0, rows, hs] = uw_f[:, :dk]
            u_ref[1, rows, hs] = uw_b[:, :dk]
            w_ref[0, rows, hs] = uw_f[:, dk:].astype(BF16)
            w_ref[1, rows, hs] = uw_b[:, dk:].astype(BF16)
            qd_ref[0, rows, hs] = (q * egf).astype(BF16)
            qd_ref[1, rows, hs] = (q * egb).astype(BF16)
            kd_ref[0, rows, hs] = (k * jnp.exp(glf - gf)).astype(BF16)
            kd_ref[1, rows, hs] = (k * jnp.exp(glb - gb)).astype(BF16)
            egl_f.append(jnp.exp(glf[0:1, :]))
            egl_b.append(jnp.exp(glb[0:1, :]))
            if h == DN_HEADS - 1:
                fill = [jnp.zeros((8 - DN_HEADS, LANES), F32)]
                egl_ref[0, c] = jnp.concatenate(egl_f + fill, axis=0)
                egl_ref[1, c] = jnp.concatenate(egl_b + fill, axis=0)
                egl_f, egl_b = [], []
        return carry

    lax.fori_loop(0, t // (ch * per_iter), chunk_body, 0)


def _dn_prep_call(qkv_a, gates, conv_w, a_log, dt_bias, t):
    bsz, s, cw = qkv_a.shape
    nt = s // t
    per_tile = t // HALO
    n_halo = s // HALO
    cwp = jnp.zeros((8, cw), F32).at[:DN_CONV].set(conv_w)
    alog = jnp.zeros((1, LANES), F32).at[0, :2 * DN_HEADS].set(a_log.reshape(-1))
    dtb = jnp.zeros((1, LANES), F32).at[0, :2 * DN_HEADS].set(dt_bias.reshape(-1))
    nch = s // DN_CHUNK
    wide = lambda dt, wd: jax.ShapeDtypeStruct((2, bsz, s, wd), dt)
    spec = lambda wd: pl.BlockSpec((2, None, t, wd), lambda b, i: (0, b, i, 0))
    return pl.pallas_call(
        functools.partial(_dn_prep_kernel, t=t),
        grid=(bsz, nt),
        in_specs=[pl.BlockSpec((None, t, cw), lambda b, i: (b, i, 0)),
                  pl.BlockSpec((None, HALO, cw), lambda b, i: (b, jnp.maximum(i * per_tile - 1, 0), 0)),
                  pl.BlockSpec((None, HALO, cw), lambda b, i: (b, jnp.minimum((i + 1) * per_tile, n_halo - 1), 0)),
                  pl.BlockSpec((None, t, LANES), lambda b, i: (b, i, 0)),
                  pl.BlockSpec((8, cw), lambda b, i: (0, 0)),
                  pl.BlockSpec((1, LANES), lambda b, i: (0, 0)),
                  pl.BlockSpec((1, LANES), lambda b, i: (0, 0))],
        out_specs=[spec(DN_WIDTH), spec(DN_WIDTH), spec(DN_WIDTH), spec(DN_WIDTH), spec(DN_HEADS * DN_CHUNK),
                   pl.BlockSpec((2, None, t // DN_CHUNK, 8, LANES), lambda b, i: (0, b, i, 0, 0))],
        out_shape=[wide(F32, DN_WIDTH), wide(BF16, DN_WIDTH), wide(BF16, DN_WIDTH), wide(BF16, DN_WIDTH),
                   wide(BF16, DN_HEADS * DN_CHUNK),
                   jax.ShapeDtypeStruct((2, bsz, nch, 8, LANES), F32)],
        scratch_shapes=[pltpu.VMEM((t, DN_WIDTH), F32), pltpu.VMEM((t, DN_WIDTH), F32),
                        pltpu.VMEM((t, DN_WIDTH), F32), pltpu.VMEM((t, LANES), F32)],
        compiler_params=_params(("parallel", "parallel")),
        name="deltanet_prep",
    )(qkv_a, qkv_a, qkv_a, gates, cwp, alog, dtb)


def _dn_scan_kernel(u_ref, w_ref, qd_ref, kd_ref, at_ref, egl_ref, o_ref, st_ref, *, nc):
    ch = DN_CHUNK
    dk = DN_HEAD_DIM
    d = pl.program_id(0)

    @pl.when(pl.program_id(2) == 0)
    def _():
        st_ref[...] = jnp.zeros(st_ref.shape, st_ref.dtype)

    n_seq = u_ref.shape[0]
    units = [(sq, h) for sq in range(n_seq) for h in range(DN_HEADS)]
    hsl = [slice(h * dk, (h + 1) * dk) for h in range(DN_HEADS)]
    sts = [st_ref[sq * DN_HEADS + h] for sq, h in units]
    for j in range(nc):
        cc = j + d * (nc - 1 - 2 * j)
        rows = pl.ds(pl.multiple_of(cc * ch, ch), ch)
        egls = [egl_ref[sq, cc] for sq in range(n_seq)]
        sbs = [st.astype(BF16) for st in sts]
        vbs = [(u_ref[sq, rows, hsl[h]] - _dot(w_ref[sq, rows, hsl[h]], sb)).astype(BF16)
               for (sq, h), sb in zip(units, sbs)]
        qss = [_dot(qd_ref[sq, rows, hsl[h]], sb) for (sq, h), sb in zip(units, sbs)]
        sts = [st * egls[sq][h:h + 1, :] + _dot_tn(kd_ref[sq, rows, hsl[h]], vb)
               for (sq, h), st, vb in zip(units, sts, vbs)]
        for (sq, h), qs, vb in zip(units, qss, vbs):
            o_ref[sq, rows, hsl[h]] = qs + _dot(at_ref[sq, rows, h * ch:(h + 1) * ch], vb)
    for (sq, h), st in zip(units, sts):
        st_ref[sq * DN_HEADS + h] = st


def _dn_scan_call(u, w, qd, kd, at, egl, tc):
    _, bsz, s, wd = u.shape
    nb = s // tc
    nc = tc // DN_CHUNK
    n_seq = math.gcd(bsz, 4)

    def blk(d, n):
        return n + d * (nb - 1 - 2 * n)

    spec = lambda width: pl.BlockSpec((None, n_seq, tc, width), lambda d, b, n: (d, b, blk(d, n), 0))
    return pl.pallas_call(
        functools.partial(_dn_scan_kernel, nc=nc),
        grid=(2, bsz // n_seq, nb),
        in_specs=[spec(wd), spec(wd), spec(wd), spec(wd), spec(DN_HEADS * DN_CHUNK),
                  pl.BlockSpec((None, n_seq, nc, 8, LANES), lambda d, b, n: (d, b, blk(d, n), 0, 0))],
        out_specs=spec(wd),
        out_shape=jax.ShapeDtypeStruct((2, bsz, s, wd), F32),
        scratch_shapes=[pltpu.VMEM((n_seq * DN_HEADS, DN_HEAD_DIM, DN_HEAD_DIM), F32)],
        compiler_params=_params(("parallel", "parallel", "arbitrary")),
        name="deltanet_scan",
    )(u, w, qd, kd, at, egl)


def _band_attn_kernel(*refs, n_kv, grp, blk, tq, qs, t_len, with_sink, with_lse):
    q_ref, kp_ref, kc_ref, kn_ref, vp_ref, vc_ref, vn_ref = refs[:7]
    pos = 7
    sink_ref = None
    if with_sink:
        sink_ref = refs[pos]
        pos += 1
    o_ref = refs[pos]
    lse_ref = refs[pos + 1] if with_lse else None
    hd = HEAD_DIM
    i0 = pl.program_id(2) * tq
    kcat = jnp.concatenate([kp_ref[...], kc_ref[...], kn_ref[...]], axis=0)
    vcat = jnp.concatenate([vp_ref[...], vc_ref[...], vn_ref[...]], axis=0)
    kwin = qs + 2 * blk
    n_q = n_kv * grp
    n_sub = tq // qs
    lse_lane = lax.broadcasted_iota(I32, (qs, LSE_COLS), 1)
    ones_k = jnp.ones((kwin, LSE_COLS), BF16)
    lses = [jnp.zeros((qs, LSE_COLS), F32) for _ in range(n_sub)]
    outs = [[] for _ in range(n_sub)]
    biases, khs, vhs = [], [], []
    for sub in range(n_sub):
        k0 = sub * qs
        rowpos = i0 + k0 + lax.broadcasted_iota(I32, (qs, kwin), 0)
        keypos = i0 - blk + k0 + lax.broadcasted_iota(I32, (qs, kwin), 1)
        mask = (jnp.abs(keypos - rowpos) <= blk) & (keypos >= 0) & (keypos < t_len)
        biases.append(jnp.where(mask, 0.0, NEG))
        khs.append([kcat[k0:k0 + kwin, kv * hd:(kv + 1) * hd] for kv in range(n_kv)])
        vhs.append([vcat[k0:k0 + kwin, kv * hd:(kv + 1) * hd] for kv in range(n_kv)])
    units = [(sub, hq) for sub in range(n_sub) for hq in range(n_q)]
    for u0 in range(0, len(units), ATTN_UNITS_PER_STAGE):
        stage = units[u0:u0 + ATTN_UNITS_PER_STAGE]
        scs = [_dot_nt(q_ref[sub * qs:(sub + 1) * qs, hq * hd:(hq + 1) * hd], khs[sub][hq // grp]) + biases[sub]
               for sub, hq in stage]
        ps, ms = [], []
        for (sub, hq), sc in zip(stage, scs):
            m = jnp.max(sc, axis=-1, keepdims=True)
            if with_sink:
                m = jnp.maximum(m, sink_ref[hq])
            ps.append(jnp.exp((sc - m).astype(BF16)))
            ms.append(m)
        dens = [_dot(p, ones_k) for p in ps]
        for (sub, hq), p, m, den in zip(stage, ps, ms, dens):
            if with_sink:
                den = den + jnp.exp(sink_ref[hq] - m)
            if with_lse:
                lses[sub] = jnp.where(lse_lane == hq, m + jnp.log(den), lses[sub])
            outs[sub].append(_dot(p, vhs[sub][hq // grp]) / den[:, :hd])
    for sub in range(n_sub):
        o_ref[sub * qs:(sub + 1) * qs, :] = jnp.concatenate(outs[sub], axis=1).astype(o_ref.dtype)
        if with_lse:
            lse_ref[sub * qs:(sub + 1) * qs, :] = lses[sub]


LSE_COLS = LANES
ATTN_UNITS_PER_STAGE = 8


def _band_attn_call(q, k, v, *, n_kv, grp, blk, sinks, with_lse, name):
    bsz, n_res, t_len, _ = q.shape
    tq = min(512, t_len)
    qs = min(128, tq)
    nt = t_len // tq
    ratio = tq // blk
    nblk = t_len // blk
    qw = n_kv * grp * HEAD_DIM
    kw = n_kv * HEAD_DIM
    cur = lambda wd: pl.BlockSpec((None, None, tq, wd), lambda b, r, i: (b, r, i, 0))
    prev = lambda wd: pl.BlockSpec((None, None, blk, wd), lambda b, r, i: (b, r, jnp.maximum(i * ratio - 1, 0), 0))
    nxt = lambda wd: pl.BlockSpec((None, None, blk, wd),
                                  lambda b, r, i: (b, r, jnp.minimum((i + 1) * ratio, nblk - 1), 0))
    in_specs = [cur(qw), prev(kw), cur(kw), nxt(kw), prev(kw), cur(kw), nxt(kw)]
    args = [q, k, k, k, v, v, v]
    if sinks is not None:
        in_specs.append(pl.BlockSpec(memory_space=pltpu.SMEM))
        args.append(sinks)
    out_shape = [jax.ShapeDtypeStruct((bsz, n_res, t_len, qw), BF16)]
    out_specs = [cur(qw)]
    if with_lse:
        out_shape.append(jax.ShapeDtypeStruct((bsz, n_res, t_len, LSE_COLS), F32))
        out_specs.append(cur(LSE_COLS))
    return pl.pallas_call(
        functools.partial(_band_attn_kernel, n_kv=n_kv, grp=grp, blk=blk, tq=tq, qs=qs, t_len=t_len,
                          with_sink=sinks is not None, with_lse=with_lse),
        grid=(bsz, n_res, nt),
        in_specs=in_specs,
        out_specs=out_specs,
        out_shape=out_shape,
        compiler_params=_params(("parallel", "parallel", "parallel")),
        name=name,
    )(*args)


def _dilated_group(qg, kg, vg, window, dil):
    return _band_attn_call(qg, kg, vg, n_kv=DIL_HEADS_PER_GROUP, grp=1, blk=window // (2 * dil), sinks=None,
                           with_lse=True, name=f"dilated_attn_{dil}")


def _layer_norm(r, g, b):
    mu = jnp.mean(r, -1, keepdims=True)
    var = jnp.mean(jnp.square(r - mu), -1, keepdims=True)
    return (r - mu) * lax.rsqrt(var + LN_EPS) * g + b


def _post_mix(y, x_ref, mod_ref, lng_ref, lnb_ref, rwt_ref, x1_ref, aff_ref):
    m = mod_ref[...]
    x1 = _layer_norm(ALPHA * x_ref[...] + (1.0 + m[2:3]) * y, lng_ref[...], lnb_ref[...])
    x1_ref[...] = x1.reshape(x1_ref.shape)
    h2 = x1 * (1.0 + m[4:5]) + m[3:4]
    split = lambda a: (a.astype(BF16), (a - a.astype(BF16).astype(F32)).astype(BF16))
    (w_hi, w_lo), (h_hi, h_lo) = split(rwt_ref[...]), split(h2)
    logits = _dot_nt(w_hi, h_hi) + (_dot_nt(w_hi, h_lo) + _dot_nt(w_lo, h_hi))
    e = jnp.exp(logits - jnp.max(logits, axis=0, keepdims=True))
    aff_ref[...] = e / jnp.sum(e, axis=0, keepdims=True)


def _position_major(ref, scratch):
    dil, n, wd = ref.shape
    if dil == 1:
        return ref[0].astype(F32)
    for r in range(dil):
        a = ref[r].astype(F32)
        for c in range(wd // LANES):
            scratch[c, pl.ds(r, n, stride=dil), :] = a[:, c * LANES:(c + 1) * LANES]
    return jnp.concatenate([scratch[c] for c in range(wd // LANES)], axis=1)


def _outproj0_kernel(of_ref, ob_ref, z_ref, dnn_ref, og0_ref, og1_ref, og2_ref, l0_ref, l1_ref, l2_ref,
                     w_ref, x_ref, mod_ref, lng_ref, lnb_ref, rwt_ref, x1_ref, aff_ref, regroup_ref):
    dk = DN_HEAD_DIM
    od = of_ref[...] + ob_ref[...]
    z = z_ref[...]
    parts = []
    for h in range(DN_HEADS):
        oh = od[:, h * dk:(h + 1) * dk]
        oh = oh * lax.rsqrt(jnp.mean(oh * oh, -1, keepdims=True) + NORM_EPS) * dnn_ref[...]
        parts.append(oh * _silu(z[:, h * dk:(h + 1) * dk]))
    o_dn = jnp.concatenate(parts, axis=1).astype(BF16)
    l0, l1, l2 = (_position_major(r, regroup_ref) for r in (l0_ref, l1_ref, l2_ref))
    og0, og1, og2 = (_position_major(r, regroup_ref) for r in (og0_ref, og1_ref, og2_ref))
    mx = jnp.maximum(jnp.maximum(l0, l1), l2)
    e0, e1, e2 = jnp.exp(l0 - mx), jnp.exp(l1 - mx), jnp.exp(l2 - mx)
    den = e0 + e1 + e2
    head = lax.broadcasted_iota(I32, (l0.shape[0], DIL_GROUP_WIDTH), 1) >> int(math.log2(HEAD_DIM))

    def per_lane(wt):
        out = jnp.zeros(head.shape, F32)
        for h in range(DIL_HEADS_PER_GROUP):
            out = jnp.where(head == h, wt[:, h:h + 1], out)
        return out

    o_dil = (per_lane(e0 / den) * og0 + per_lane(e1 / den) * og1 + per_lane(e2 / den) * og2).astype(BF16)
    y = _dot(o_dn, w_ref[0:DN_WIDTH, :]) + _dot(o_dil, w_ref[DN_WIDTH:DN_WIDTH + DIL_GROUP_WIDTH, :])
    _post_mix(y, x_ref, mod_ref, lng_ref, lnb_ref, rwt_ref, x1_ref, aff_ref)


def _outproj1_kernel(o_ref, w_ref, x_ref, mod_ref, lng_ref, lnb_ref, rwt_ref, x1_ref, aff_ref):
    y = _dot(o_ref[...], w_ref[...])
    _post_mix(y, x_ref, mod_ref, lng_ref, lnb_ref, rwt_ref, x1_ref, aff_ref)


def _tail_specs(bsz, s, d, tm, n_e):
    row = lambda wd: pl.BlockSpec((None, tm, wd), lambda b, i: (b, i, 0))
    const = lambda shp: pl.BlockSpec(shp, lambda b, i: tuple(0 for _ in shp))
    in_specs = [row(d), pl.BlockSpec((None, 6, d), lambda b, i: (b, 0, 0)), const((1, d)), const((1, d)),
                const((n_e, d))]
    out_specs = [pl.BlockSpec((None, tm, d // LANES, LANES), lambda b, i: (b, i, 0, 0)),
                 pl.BlockSpec((None, n_e, tm), lambda b, i: (b, 0, i))]
    out_shape = [jax.ShapeDtypeStruct((bsz, s, d // LANES, LANES), F32), jax.ShapeDtypeStruct((bsz, n_e, s), F32)]
    return in_specs, out_specs, out_shape


def _outproj0_call(o_scan, z, dn_norm, ogs, lses, w_out, x, mod, ln_g, ln_b, router_w, tm):
    bsz, s, d = x.shape
    n_e = router_w.shape[1]
    tail_in, out_specs, out_shape = _tail_specs(bsz, s, d, tm, n_e)
    row = lambda wd: pl.BlockSpec((None, tm, wd), lambda b, i: (b, i, 0))
    dirspec = lambda dd: pl.BlockSpec((None, None, tm, DN_WIDTH), lambda b, i: (dd, b, i, 0))
    gw = DIL_GROUP_WIDTH
    resid = lambda a: pl.BlockSpec((None, a.shape[1], tm // a.shape[1], a.shape[3]), lambda b, i: (b, 0, i, 0))
    in_specs = [dirspec(0), dirspec(1), row(DN_WIDTH), pl.BlockSpec((1, DN_HEAD_DIM), lambda b, i: (0, 0))]
    in_specs += [resid(a) for a in ogs] + [resid(a) for a in lses]
    in_specs += [pl.BlockSpec(w_out.shape, lambda b, i: (0, 0))] + tail_in
    return pl.pallas_call(
        _outproj0_kernel, grid=(bsz, s // tm), in_specs=in_specs, out_specs=out_specs, out_shape=out_shape,
        scratch_shapes=[pltpu.VMEM((gw // LANES, tm, LANES), F32)],
        compiler_params=_params(("parallel", "parallel")), name="outproj_deltanet_dilated",
    )(o_scan, o_scan, z, dn_norm.reshape(1, -1), *ogs, *lses, w_out, x, mod, ln_g.reshape(1, d), ln_b.reshape(1, d),
      router_w.T)


def _outproj1_call(o, w_out, x, mod, ln_g, ln_b, router_w, tm):
    bsz, s, d = x.shape
    n_e = router_w.shape[1]
    tail_in, out_specs, out_shape = _tail_specs(bsz, s, d, tm, n_e)
    in_specs = [pl.BlockSpec((None, tm, o.shape[-1]), lambda b, i: (b, i, 0)),
                pl.BlockSpec(w_out.shape, lambda b, i: (0, 0))] + tail_in
    return pl.pallas_call(
        _outproj1_kernel, grid=(bsz, s // tm), in_specs=in_specs, out_specs=out_specs, out_shape=out_shape,
        compiler_params=_params(("parallel", "parallel")), name="outproj_swa",
    )(o, w_out, x, mod, ln_g.reshape(1, d), ln_b.reshape(1, d), router_w.T)


def _topk_kernel(a_ref, idx_ref, *, cap, jb):
    a = a_ref[...]
    n_e, rows, _ = a.shape
    bits = pltpu.bitcast(a, I32)
    thr = jnp.zeros((n_e, 1, 1), I32)
    for bit in range(30, -1, -1):
        cand = thr | (1 << bit)
        cnt = jnp.sum(jnp.where(bits >= cand, 1, 0), axis=(1, 2), keepdims=True)
        thr = jnp.where(cnt >= cap, cand, thr)
    gt = jnp.where(bits > thr, 1.0, 0.0).astype(F32)
    eq = jnp.where(bits == thr, 1.0, 0.0).astype(F32)
    need = cap - jnp.sum(gt, axis=(1, 2), keepdims=True)
    ru = lax.broadcasted_iota(I32, (LANES, LANES), 0)
    cu = lax.broadcasted_iota(I32, (LANES, LANES), 1)
    upper = jnp.where(ru <= cu, 1.0, 0.0).astype(BF16)
    rl = lax.broadcasted_iota(I32, (rows, rows), 0)
    cl = lax.broadcasted_iota(I32, (rows, rows), 1)
    lstrict = jnp.where(cl < rl, 1.0, 0.0).astype(BF16)

    def fold_cumsum(xs):
        withins = [_dot(x.astype(BF16), upper) for x in xs]
        rowtots = [jnp.broadcast_to(w[:, LANES - 1:LANES], w.shape) for w in withins]
        befores = [_dot(lstrict, r.astype(BF16)) for r in rowtots]
        return [w + b for w, b in zip(withins, befores)], [b + r for b, r in zip(befores, rowtots)]

    experts = range(n_e)
    eq_counts, _ = fold_cumsum([eq[e] for e in experts])
    sels = [jnp.maximum(gt[e], jnp.where(eq_counts[e] - eq[e] < need[e], eq[e], 0.0)) for e in experts]
    counts, count_ends = fold_cumsum(sels)
    rowid = lax.broadcasted_iota(I32, (rows, jb), 0).astype(F32)
    for j0 in range(0, cap, jb):
        slot = (j0 + lax.broadcasted_iota(I32, (1, jb), 1)).astype(F32)
        row = [jnp.sum(jnp.where(count_ends[e][:, 0:1] <= slot, 1.0, 0.0), axis=0, keepdims=True) for e in experts]
        onehot = [jnp.where(rowid == row[e], 1.0, 0.0) for e in experts]
        count_row = [_dot_tn(counts[e], onehot[e], HI) for e in experts]
        for e in experts:
            lane = jnp.sum(jnp.where(count_row[e] <= slot, 1.0, 0.0), axis=0, keepdims=True)
            idx_ref[e:e + 1, j0:j0 + jb] = (row[e] * LANES + lane).astype(I32)


def _topk_call(aff, cap):
    bsz, n_e, s = aff.shape
    rows = s // LANES
    return pl.pallas_call(
        functools.partial(_topk_kernel, cap=cap, jb=min(512, cap)),
        grid=(bsz,),
        in_specs=[pl.BlockSpec((None, n_e, rows, LANES), lambda b: (b, 0, 0, 0))],
        out_specs=pl.BlockSpec((None, n_e, cap), lambda b: (b, 0, 0)),
        out_shape=jax.ShapeDtypeStruct((bsz, n_e, cap), I32),
        compiler_params=_params(("parallel",)), name="topk_route",
    )(aff.reshape(bsz, n_e, rows, LANES))


SUBLANES = 8


def _moe_kernel(idx_hbm, aff_hbm, x_hbm, mod_ref, wg_ref, wu_ref, wd_ref, out_hbm,
                idx_s, aff_s, xg0, xg1, hid_s, y, acc, sem_i, sem_a, sem_g, sem_o, *, cap, n_f, y_bounds):
    b = pl.program_id(0)
    e = pl.program_id(1)
    f = pl.program_id(2)
    n_b = pl.num_programs(0)
    n_e = pl.num_programs(1)
    per_f = cap // n_f
    table = idx_s.shape[0] // 2
    cur_table = (b % 2) * table
    more = b + 1 < n_b

    def row_copy(src_b, tok, j, dst, sem):
        return pltpu.make_async_copy(x_hbm.at[src_b, tok], dst.at[j], sem)

    def wait_rows(dst, sem):
        pltpu.make_async_copy(x_hbm.at[b, pl.ds(0, cap)], dst, sem).wait()

    def table_copy(src_b):
        return pltpu.make_async_copy(idx_hbm.at[src_b], idx_s.at[pl.ds((src_b % 2) * table, table)], sem_i)

    @pl.when((e == 0) & (f == 0))
    def _():
        acc[...] = jnp.zeros(acc.shape, acc.dtype)

        @pl.when(b == 0)
        def _():
            table_copy(b).start()
            table_copy(b).wait()

            def gather(jo, carry):
                for r in range(SUBLANES):
                    j = jo * SUBLANES + r
                    row_copy(b, idx_s[j], j, xg0, sem_g.at[0]).start(priority=r % 2)
                return carry

            lax.fori_loop(0, cap // SUBLANES, gather, 0)
            wait_rows(xg0, sem_g.at[0])

    @pl.when((f == 0) & more & (e == n_e - 2))
    def _():
        table_copy(b + 1).start()

    @pl.when((f == 0) & more & (e == n_e - 1))
    def _():
        table_copy(b + 1).wait()

    aff_copy = pltpu.make_async_copy(aff_hbm.at[b, e], aff_s, sem_a)

    @pl.when(f == 0)
    def _():
        aff_copy.start()

    def expert_step(xg_cur, xg_next, sem_next):
        last = e == n_e - 1
        src_b = jnp.where(last & more, b + 1, b)
        next_base = jnp.where(last, jnp.where(more, ((b + 1) % 2) * table, cur_table + e * cap),
                              cur_table + (e + 1) * cap) + f * per_f
        for r in range(per_f):
            row_copy(src_b, idx_s[next_base + r], f * per_f + r, xg_next, sem_next).start(priority=r % 2)
        m = mod_ref[...]
        xv = (xg_cur[...].reshape(cap, m.shape[-1]) * (1.0 + m[4:5]) + m[3:4]).astype(BF16)
        hid_s[f] = (_silu(_dot(xv, wg_ref[...])) * _dot(xv, wu_ref[...])).astype(BF16)

        @pl.when(f == n_f - 1)
        def _():
            aff_copy.wait()

            def scatter_rows(j0):
                yc = y[pl.ds(j0, SUBLANES), :].reshape((SUBLANES,) + acc.shape[1:])
                toks = [idx_s[cur_table + e * cap + j0 + r] for r in range(SUBLANES)]
                olds = [acc[toks[r]] for r in range(SUBLANES)]
                for r in range(SUBLANES):
                    acc[toks[r]] = olds[r] + aff_s[toks[r]] * yc[r]

            for p, (r0, r1) in enumerate(zip(y_bounds[:-1], y_bounds[1:])):
                hid = jnp.concatenate([hid_s[i, r0:r1, :] for i in range(n_f)], axis=1)
                y[r0:r1, :] = _dot(hid, wd_ref[...])
                if p > 0:
                    for j0 in range(y_bounds[p - 1], r0, SUBLANES):
                        scatter_rows(j0)

            def scatter_last(jo, carry):
                scatter_rows(pl.multiple_of(y_bounds[-2] + jo * SUBLANES, SUBLANES))
                return carry

            lax.fori_loop(0, (y_bounds[-1] - y_bounds[-2]) // SUBLANES, scatter_last, 0)
            wait_rows(xg_next, sem_next)

            @pl.when(e == n_e - 1)
            def _():
                co = pltpu.make_async_copy(acc, out_hbm.at[b], sem_o)
                co.start()
                co.wait()

    @pl.when(e % 2 == 0)
    def _():
        expert_step(xg0, xg1, sem_g.at[1])

    @pl.when(e % 2 == 1)
    def _():
        expert_step(xg1, xg0, sem_g.at[0])


def _moe_call(x1, mod, idx, aff, wg, wu, wd, fcw):
    bsz, s, nt, _ = x1.shape
    d = nt * LANES
    n_e, _, ff = wg.shape
    cap = idx.shape[-1]
    fcw = min(fcw, ff)
    n_f = ff // fcw
    y_bounds = (0, cap // 2, cap // 2 + 3 * cap // 8, cap)
    assert n_e % 2 == 0 and cap % n_f == 0 and cap % (8 * SUBLANES) == 0 and d // LANES == SUBLANES
    anyspec = pl.BlockSpec(memory_space=pl.ANY)
    return pl.pallas_call(
        functools.partial(_moe_kernel, cap=cap, n_f=n_f, y_bounds=y_bounds),
        grid=(bsz, n_e, n_f),
        in_specs=[anyspec, anyspec, anyspec,
                  pl.BlockSpec((None, 6, d), lambda b, e, f: (b, 0, 0)),
                  pl.BlockSpec((None, d, fcw), lambda b, e, f: (e, 0, f)),
                  pl.BlockSpec((None, d, fcw), lambda b, e, f: (e, 0, f)),
                  pl.BlockSpec((None, ff, d), lambda b, e, f: (e, 0, 0))],
        out_specs=anyspec,
        out_shape=jax.ShapeDtypeStruct((bsz, s, d // LANES, LANES), F32),
        scratch_shapes=[pltpu.SMEM((2 * n_e * cap,), I32), pltpu.SMEM((s,), F32),
                        pltpu.VMEM((cap, nt, LANES), F32), pltpu.VMEM((cap, nt, LANES), F32),
                        pltpu.VMEM((n_f, cap, fcw), BF16), pltpu.VMEM((cap, d), F32),
                        pltpu.VMEM((s, nt, LANES), F32),
                        pltpu.SemaphoreType.DMA(()), pltpu.SemaphoreType.DMA(()),
                        pltpu.SemaphoreType.DMA((2,)), pltpu.SemaphoreType.DMA(())],
        compiler_params=_params(("arbitrary", "arbitrary", "arbitrary"), MOE_VMEM_LIMIT),
        name="moe_experts",
    )(idx.reshape(bsz, n_e * cap), aff, x1, mod, wg, wu, wd)


def _ln2_kernel(x_ref, y_ref, g2_ref, g_ref, b_ref, o_ref):
    r = ALPHA * x_ref[...] + (1.0 + g2_ref[...]) * y_ref[...]
    o_ref[...] = _layer_norm(r.reshape(o_ref.shape), g_ref[...], b_ref[...])


def _ln2_call(x1, moe, mod, g, b, tm):
    bsz, s, nt, _ = x1.shape
    d = nt * LANES
    row = pl.BlockSpec((None, tm, d), lambda bb, i: (bb, i, 0))
    tiles = pl.BlockSpec((None, tm, nt, LANES), lambda bb, i: (bb, i, 0, 0))
    vec = pl.BlockSpec((1, d), lambda bb, i: (0, 0))
    return pl.pallas_call(
        _ln2_kernel, grid=(bsz, s // tm),
        in_specs=[tiles, tiles, pl.BlockSpec((None, 1, nt, LANES), lambda bb, i: (bb, 0, 0, 0)), vec, vec],
        out_specs=row, out_shape=jax.ShapeDtypeStruct((bsz, s, d), F32),
        compiler_params=_params(("parallel", "parallel")), name="ffn_postnorm",
    )(x1, moe, mod[:, 5].reshape(bsz, 1, nt, LANES), g.reshape(1, d), b.reshape(1, d))


def _ffn_block(x1, aff, mod, wg, wu, wd, ln_g, ln_b, tm):
    s = x1.shape[1]
    cap = (EC_FACTOR * s) // N_EXPERTS
    idx = _topk_call(aff, cap)
    moe = _moe_call(x1, mod, idx, aff, wg.astype(BF16), wu.astype(BF16), wd.astype(BF16), fcw=512)
    return _ln2_call(x1, moe, mod, ln_g, ln_b, tm)


def kernel(x, c, positions, ada_w, ada_b, ab_w_in, ab_conv_w, ab_a_log, ab_dt_bias, ab_dn_norm, ab_w_out, swa_w_in,
           swa_sinks, swa_w_out, ln_mix_g, ln_mix_b, router_w, moe_w_gate, moe_w_up, moe_w_down, ln_ffn_g, ln_ffn_b):
    bsz, s, d = x.shape
    tm = min(512, s)
    mod = _mod_call(c, ada_w, ada_b).reshape(DEPTH, bsz, 6, d)
    rope = _rope_tables(positions)

    w_in = ab_w_in[0]
    n_a = 4 * DN_WIDTH
    n_g = 4 * DN_HEADS
    gw = DIL_GROUP_WIDTH
    q_scale = HEAD_DIM ** -0.5
    cols = [w_in[:, :n_a], jnp.pad(w_in[:, n_a:n_a + n_g], ((0, 0), (0, LANES - n_g)))]
    groups = [(0, 3 * DN_WIDTH, 0, False), (3 * DN_WIDTH, DN_WIDTH, 1, False), (n_a, LANES, 2, False)]
    for gi in range(len(DIL_PAIRS)):
        for part in range(3):
            c0 = n_a + n_g + part * DIL_WIDTH + gi * gw
            cols.append(w_in[:, c0:c0 + gw] * (q_scale if part == 0 else 1.0))
            groups.append((n_a + LANES + (3 * gi + part) * gw, gw, 3 + 3 * gi + part, part < 2))
    w0 = jnp.concatenate(cols, axis=1).astype(BF16)
    dils = tuple(dil for _, dil in DIL_PAIRS for _ in range(3))
    outs0 = _inproj_call(
        x, mod[0], w0, rope, _chunk_plan(groups), (3 * DN_WIDTH, DN_WIDTH, LANES) + (gw,) * 9,
        (F32, F32, F32) + (BF16,) * 9, (1, 1, 1) + dils, tm, "inproj_deltanet_dilated")
    qkv_a, z, gates = outs0[:3]
    u, w, qd, kd, at, egl = _dn_prep_call(qkv_a, gates, ab_conv_w[0], ab_a_log[0], ab_dt_bias[0], tm)
    o_scan = _dn_scan_call(u, w, qd, kd, at, egl, min(256, s))
    ogs, lses = [], []
    for gi, (window, dil) in enumerate(DIL_PAIRS):
        qkv_g = [a.reshape(bsz, dil, s // dil, gw) for a in outs0[3 + 3 * gi:6 + 3 * gi]]
        o_g, lse_g = _dilated_group(*qkv_g, window, dil)
        ogs.append(o_g)
        lses.append(lse_g)
    x1, aff = _outproj0_call(o_scan, z, ab_dn_norm[0], ogs, lses, ab_w_out[0].astype(BF16), x, mod[0],
                             ln_mix_g[0], ln_mix_b[0], router_w[0], tm)
    x = _ffn_block(x1, aff, mod[0], moe_w_gate[0], moe_w_up[0], moe_w_down[0], ln_ffn_g[0], ln_ffn_b[0], tm)

    qw = SWA_Q_HEADS * HEAD_DIM
    kw = SWA_KV_HEADS * HEAD_DIM
    plan1 = _chunk_plan([(0, qw, 0, True), (qw, kw, 1, True), (qw + kw, kw, 2, False)])
    w1 = jnp.concatenate([swa_w_in[0][:, :qw] * q_scale, swa_w_in[0][:, qw:]], axis=1).astype(BF16)
    q1, k1, v1 = _inproj_call(x, mod[1], w1, rope, plan1, (qw, kw, kw), (BF16, BF16, BF16), (1, 1, 1), tm,
                              "inproj_swa")
    (o1,) = _band_attn_call(q1[:, None], k1[:, None], v1[:, None], n_kv=SWA_KV_HEADS,
                            grp=SWA_Q_HEADS // SWA_KV_HEADS, blk=SWA_WINDOW, sinks=swa_sinks[0], with_lse=False,
                            name="swa_attn")
    x1, aff = _outproj1_call(o1[:, 0], swa_w_out[0].astype(BF16), x, mod[1], ln_mix_g[1], ln_mix_b[1], router_w[1],
                             tm)
    x = _ffn_block(x1, aff, mod[1], moe_w_gate[1], moe_w_up[1], moe_w_down[1], ln_ffn_g[1], ln_ffn_b[1], tm)
    return x
```

```python
import functools
import math

import jax
import jax.numpy as jnp
from jax import lax
from jax.experimental import pallas as pl
from jax.experimental.pallas import tpu as pltpu

F32 = jnp.float32
BF16 = jnp.bfloat16
I32 = jnp.int32
HI = lax.Precision.HIGHEST

DEPTH = 2
HEAD_DIM = 64
ROT_DIM = HEAD_DIM // 4
ROPE_THETA = 500000.0
DN_HEADS = 4
DN_HEAD_DIM = 128
DN_CHUNK = 64
DN_CONV = 5
DN_WIDTH = DN_HEADS * DN_HEAD_DIM
DIL_PAIRS = ((128, 1), (512, 4), (2048, 16))
DIL_HEADS_PER_GROUP = 4
DIL_GROUP_WIDTH = DIL_HEADS_PER_GROUP * HEAD_DIM
DIL_WIDTH = DIL_GROUP_WIDTH * len(DIL_PAIRS)
SWA_Q_HEADS = 16
SWA_KV_HEADS = 4
SWA_WINDOW = 128
N_EXPERTS = 16
EC_FACTOR = 2
ALPHA = (2.0 * DEPTH) ** 0.25
LN_EPS = 1e-5
NORM_EPS = 1e-6
NEG = -1e30
LANES = 128
HALO = 8
VMEM_LIMIT = 56 * 1024 * 1024
MOE_VMEM_LIMIT = 60 * 1024 * 1024


def _dot(a, b, prec=None):
    return jnp.dot(a, b, preferred_element_type=F32, precision=prec)


def _dot_nt(a, b, prec=None):
    return lax.dot_general(a, b, (((1,), (1,)), ((), ())), preferred_element_type=F32, precision=prec)


def _dot_tn(a, b, prec=None):
    return lax.dot_general(a, b, (((0,), (0,)), ((), ())), preferred_element_type=F32, precision=prec)


def _silu(x):
    return x * jax.nn.sigmoid(x)


def _params(sem, vmem_limit=VMEM_LIMIT):
    return pltpu.CompilerParams(dimension_semantics=sem, vmem_limit_bytes=vmem_limit)


def _mod_kernel(c_ref, w_ref, b_ref, o_ref):
    o_ref[...] = _dot(_silu(c_ref[...]), w_ref[...], HI) + b_ref[...]


def _mod_call(c, ada_w, ada_b):
    depth, d, n6 = ada_w.shape
    bsz = c.shape[0]
    tn = n6 // 4
    return pl.pallas_call(
        _mod_kernel,
        grid=(depth, n6 // tn),
        in_specs=[pl.BlockSpec((bsz, d), lambda i, j: (0, 0)),
                  pl.BlockSpec((None, d, tn), lambda i, j: (i, 0, j)),
                  pl.BlockSpec((None, 1, tn), lambda i, j: (i, 0, j))],
        out_specs=pl.BlockSpec((None, bsz, tn), lambda i, j: (i, 0, j)),
        out_shape=jax.ShapeDtypeStruct((depth, bsz, n6), F32),
        compiler_params=_params(("arbitrary", "arbitrary")),
        name="adaln_mod",
    )(c, ada_w, ada_b.reshape(depth, 1, n6))


def _inproj_kernel(x_ref, mod_ref, w_ref, rope_ref, *refs, plan):
    *out_refs, regroup_ref = refs
    m = mod_ref[...]
    h = (x_ref[...] * (1.0 + m[1:2]) + m[0:1]).astype(BF16)
    half = ROT_DIM // 2
    if any(p[4] for p in plan):
        src = lax.broadcasted_iota(I32, (ROT_DIM, LANES), 0)
        lane = lax.broadcasted_iota(I32, (ROT_DIM, LANES), 1) & (HEAD_DIM - 1)
        pick_cos = jnp.where(lane < ROT_DIM, jnp.where((lane & (half - 1)) == src, 1.0, 0.0), 0.0)
        pick_sin = jnp.where(lane < half, jnp.where(src == lane + half, -1.0, 0.0),
                             jnp.where(lane < ROT_DIM, jnp.where(src == lane, 1.0, 0.0), 0.0))
        cs = rope_ref[...]
        lane1 = lax.broadcasted_iota(I32, (1, LANES), 1) & (HEAD_DIM - 1)
        cos_t = _dot(cs, pick_cos, HI) + jnp.where(lane1 < ROT_DIM, 0.0, 1.0)
        sin_t = _dot(cs, pick_sin, HI)
        sin_a = jnp.where(lane1 < half, sin_t, 0.0)
        sin_b = sin_t - sin_a
    for c0, width, oi, o0, rope in plan:
        acc = _dot(h, w_ref[:, c0:c0 + width])
        if rope:
            reps = width // LANES
            tile = lambda a: jnp.concatenate([a] * reps, axis=1)
            acc = (acc * tile(cos_t) + pltpu.roll(acc, width - half, 1) * tile(sin_a)
                   + pltpu.roll(acc, half, 1) * tile(sin_b))
        out = out_refs[oi]
        if len(out.shape) == 2:
            out[:, o0:o0 + width] = acc.astype(out.dtype)
        else:
            dil, n = out.shape[0], out.shape[1]
            for c in range(width // LANES):
                regroup_ref[c] = acc[:, c * LANES:(c + 1) * LANES]
            for r in range(dil):
                out[r] = jnp.concatenate([regroup_ref[c, pl.ds(r, n, stride=dil), :] for c in range(width // LANES)],
                                         axis=1).astype(out.dtype)


def _inproj_call(x, mod, w, rope, plan, out_widths, out_dtypes, out_dils, tm, name):
    bsz, s, d = x.shape
    n = w.shape[1]
    out_shape, out_specs = [], []
    for ow, od, dil in zip(out_widths, out_dtypes, out_dils):
        if dil == 1:
            out_shape.append(jax.ShapeDtypeStruct((bsz, s, ow), od))
            out_specs.append(pl.BlockSpec((None, tm, ow), lambda b, i: (b, i, 0)))
        else:
            out_shape.append(jax.ShapeDtypeStruct((bsz, dil, s // dil, ow), od))
            out_specs.append(pl.BlockSpec((None, dil, tm // dil, ow), lambda b, i: (b, 0, i, 0)))
    chunk_w = max(p[1] for p in plan)
    return pl.pallas_call(
        functools.partial(_inproj_kernel, plan=plan),
        grid=(bsz, s // tm),
        in_specs=[pl.BlockSpec((None, tm, d), lambda b, i: (b, i, 0)),
                  pl.BlockSpec((None, 6, d), lambda b, i: (b, 0, 0)),
                  pl.BlockSpec((d, n), lambda b, i: (0, 0)),
                  pl.BlockSpec((None, tm, ROT_DIM), lambda b, i: (b, i, 0))],
        out_specs=out_specs,
        out_shape=out_shape,
        scratch_shapes=[pltpu.VMEM((chunk_w // LANES, tm, LANES), F32)],
        compiler_params=_params(("parallel", "parallel")),
        name=name,
    )(x, mod, w, rope)


def _chunk_plan(groups, chunk=256):
    plan = []
    for c0, width, oi, rope in groups:
        off = 0
        while off < width:
            wd = min(chunk, width - off)
            plan.append((c0 + off, wd, oi, off, rope))
            off += wd
    return tuple(plan)


def _rope_tables(positions):
    inv_freq = jnp.power(ROPE_THETA, -jnp.arange(0, ROT_DIM, 2, dtype=F32) / ROT_DIM)
    ang = positions.astype(F32)[..., None] * inv_freq
    return jnp.concatenate([jnp.cos(ang), jnp.sin(ang)], -1)


def _dn_prep_kernel(xa_ref, top_ref, bot_ref, gt_ref, cw_ref, alog_ref, dtb_ref, pre_ref, suf_ref,
                    u_ref, w_ref, qd_ref, kd_ref, at_ref, egl_ref,
                    q_s, k_s, v_s, gc_s, *, t):
    ch = DN_CHUNK
    dk = DN_HEAD_DIM
    pad = (DN_CONV - 1) // 2
    has_top = pl.program_id(1) > 0
    has_bot = pl.program_id(1) < pl.num_programs(1) - 1
    for grp, dst in enumerate((q_s, k_s, v_s)):
        cols = slice(grp * DN_WIDTH, (grp + 1) * DN_WIDTH)
        xe = jnp.concatenate([jnp.where(has_top, top_ref[:, cols], 0.0), xa_ref[:, cols],
                              jnp.where(has_bot, bot_ref[:, cols], 0.0)], axis=0)
        y = jnp.zeros((t, DN_WIDTH), F32)
        for k in range(DN_CONV):
            y = y + xe[HALO - pad + k:HALO - pad + k + t, :] * cw_ref[k:k + 1, cols]
        y = _silu(y)
        if grp < 2:
            scale = dk ** -0.5 if grp == 0 else 1.0
            parts = []
            for h in range(DN_HEADS):
                yh = y[:, h * dk:(h + 1) * dk]
                parts.append(yh * lax.rsqrt(jnp.sum(yh * yh, -1, keepdims=True) + NORM_EPS) * scale)
            y = jnp.concatenate(parts, axis=1)
        dst[...] = y

    g = gt_ref[...]
    lane = lax.broadcasted_iota(I32, (t, LANES), 1)
    z = g + dtb_ref[...]
    softplus = jnp.maximum(z, 0.0) + jnp.log1p(jnp.exp(-jnp.abs(z)))
    dec = -jnp.exp(alog_ref[...]) * softplus
    gv = jnp.where(lane < 2 * DN_HEADS, dec, jnp.where(lane < 4 * DN_HEADS, jax.nn.sigmoid(g), 0.0))
    g_hi = gv.astype(BF16)
    rest = gv - g_hi.astype(F32)
    g_mid = rest.astype(BF16)
    g_lo = (rest - g_mid.astype(F32)).astype(BF16)
    pre, suf = pre_ref[...], suf_ref[...]
    gcf = _dot(pre, g_hi) + (_dot(pre, g_mid) + _dot(pre, g_lo))
    gcb = _dot(suf, g_hi) + (_dot(suf, g_mid) + _dot(suf, g_lo))
    gc_s[...] = jnp.where(lane < DN_HEADS, gcf, jnp.where(lane < 2 * DN_HEADS, gcb, gv))

    rr = lax.broadcasted_iota(I32, (ch, 2 * ch), 0)
    cc = lax.broadcasted_iota(I32, (ch, 2 * ch), 1)
    fwd = cc < ch
    cj = jnp.where(fwd, cc, cc - ch)
    ahead = jnp.where(fwd, cj - rr, rr - cj)
    incl = ahead <= 0
    strict = ahead < 0
    eye2 = jnp.where(cj == rr, 1.0, 0.0).astype(F32)

    def blockdiag(p):
        return jnp.concatenate([jnp.where(fwd, p, 0.0), jnp.where(fwd, 0.0, p)], axis=0).astype(BF16)

    n_sq = int(math.log2(ch)) - 1
    per_iter = math.gcd(t // ch, 8)

    def chunk_body(ci, carry):
        units = []
        for sub in range(per_iter):
            c = ci * per_iter + sub
            rows = pl.ds(pl.multiple_of(c * ch, ch), ch)
            gcc = gc_s[rows, :]
            gct = gcc.T
            for h in range(DN_HEADS):
                units.append((c, rows, gcc, gct, h))

        st, pws = [], []
        for c, rows, gcc, gct, h in units:
            bcast = lambda col, gcc=gcc: jnp.broadcast_to(gcc[:, col:col + 1], (ch, LANES))
            gf, gb = bcast(h), bcast(DN_HEADS + h)
            bf, bb = bcast(2 * DN_HEADS + h), bcast(3 * DN_HEADS + h)
            grow = jnp.concatenate([gct[h:h + 1, :], gct[DN_HEADS + h:DN_HEADS + h + 1, :]], axis=1)
            diff = jnp.where(fwd, gf, gb) - grow
            decay = jnp.where(incl, jnp.exp(jnp.where(incl, diff, 0.0)), 0.0)
            hs = slice(h * dk, (h + 1) * dk)
            k16 = k_s[rows, hs].astype(BF16)
            kk = jnp.concatenate([k16, k16], axis=0)
            lower = jnp.where(strict, jnp.where(fwd, bf, bb) * _dot_nt(k16, kk) * decay, 0.0)
            intra = (_dot_nt(q_s[rows, hs].astype(BF16), kk) * decay).astype(BF16)
            at_ref[0, rows, h * ch:(h + 1) * ch] = intra[:, :ch]
            at_ref[1, rows, h * ch:(h + 1) * ch] = intra[:, ch:]
            st.append((gf, gb, bf, bb))
            pws.append(-lower)
        ainvs = [eye2 + p for p in pws]
        pbds = [blockdiag(p) for p in pws]
        for _ in range(n_sq):
            pws = [_dot(p.astype(BF16), bd) for p, bd in zip(pws, pbds)]
            pbds = [blockdiag(p) for p in pws]
            ainvs = [a + _dot(a.astype(BF16), bd) for a, bd in zip(ainvs, pbds)]

        egl_f, egl_b = [], []
        for (c, rows, _, _, h), (gf, gb, bf, bb), ainv in zip(units, st, ainvs):
            hs = slice(h * dk, (h + 1) * dk)
            q = q_s[rows, hs]
            k = k_s[rows, hs]
            v = v_s[rows, hs]
            egf, egb = jnp.exp(gf), jnp.exp(gb)
            rhs = jnp.concatenate([jnp.concatenate([v * bf, k * bf * egf], axis=1),
                                   jnp.concatenate([v * bb, k * bb * egb], axis=1)], axis=0).astype(BF16)
            uw_f = _dot(jnp.where(fwd, ainv, 0.0).astype(BF16), rhs)
            uw_b = _dot(jnp.where(fwd, 0.0, ainv).astype(BF16), rhs)
            glf = jnp.broadcast_to(gf[ch - 1:ch, :], (ch, LANES))
            glb = jnp.broadcast_to(gb[0:1, :], (ch, LANES))
            u_ref[0, rows, hs] = uw_f[:, :dk]
            u_ref[1, rows, hs] = uw_b[:, :dk]
            w_ref[0, rows, hs] = uw_f[:, dk:].astype(BF16)
            w_ref[1, rows, hs] = uw_b[:, dk:].astype(BF16)
            qd_ref[0, rows, hs] = (q * egf).astype(BF16)
            qd_ref[1, rows, hs] = (q * egb).astype(BF16)
            kd_ref[0, rows, hs] = (k * jnp.exp(glf - gf)).astype(BF16)
            kd_ref[1, rows, hs] = (k * jnp.exp(glb - gb)).astype(BF16)
            egl_f.append(jnp.exp(glf[0:1, :]))
            egl_b.append(jnp.exp(glb[0:1, :]))
            if h == DN_HEADS - 1:
                fill = [jnp.zeros((8 - DN_HEADS, LANES), F32)]
                egl_ref[0, c] = jnp.concatenate(egl_f + fill, axis=0)
                egl_ref[1, c] = jnp.concatenate(egl_b + fill, axis=0)
                egl_f, egl_b = [], []
        return carry

    lax.fori_loop(0, t // (ch * per_iter), chunk_body, 0)


def _dn_prep_call(qkv_a, gates, conv_w, a_log, dt_bias, t):
    bsz, s, cw = qkv_a.shape
    nt = s // t
    per_tile = t // HALO
    n_halo = s // HALO
    cwp = jnp.zeros((8, cw), F32).at[:DN_CONV].set(conv_w)
    alog = jnp.zeros((1, LANES), F32).at[0, :2 * DN_HEADS].set(a_log.reshape(-1))
    dtb = jnp.zeros((1, LANES), F32).at[0, :2 * DN_HEADS].set(dt_bias.reshape(-1))
    pos = jnp.arange(t)
    same_chunk = (pos[:, None] // DN_CHUNK) == (pos[None, :] // DN_CHUNK)
    pre = (same_chunk & (pos[None, :] <= pos[:, None])).astype(BF16)
    suf = (same_chunk & (pos[None, :] >= pos[:, None])).astype(BF16)
    nch = s // DN_CHUNK
    wide = lambda dt, wd: jax.ShapeDtypeStruct((2, bsz, s, wd), dt)
    spec = lambda wd: pl.BlockSpec((2, None, t, wd), lambda b, i: (0, b, i, 0))
    return pl.pallas_call(
        functools.partial(_dn_prep_kernel, t=t),
        grid=(bsz, nt),
        in_specs=[pl.BlockSpec((None, t, cw), lambda b, i: (b, i, 0)),
                  pl.BlockSpec((None, HALO, cw), lambda b, i: (b, jnp.maximum(i * per_tile - 1, 0), 0)),
                  pl.BlockSpec((None, HALO, cw), lambda b, i: (b, jnp.minimum((i + 1) * per_tile, n_halo - 1), 0)),
                  pl.BlockSpec((None, t, LANES), lambda b, i: (b, i, 0)),
                  pl.BlockSpec((8, cw), lambda b, i: (0, 0)),
                  pl.BlockSpec((1, LANES), lambda b, i: (0, 0)),
                  pl.BlockSpec((1, LANES), lambda b, i: (0, 0)),
                  pl.BlockSpec((t, t), lambda b, i: (0, 0)),
                  pl.BlockSpec((t, t), lambda b, i: (0, 0))],
        out_specs=[spec(DN_WIDTH), spec(DN_WIDTH), spec(DN_WIDTH), spec(DN_WIDTH), spec(DN_HEADS * DN_CHUNK),
                   pl.BlockSpec((2, None, t // DN_CHUNK, 8, LANES), lambda b, i: (0, b, i, 0, 0))],
        out_shape=[wide(F32, DN_WIDTH), wide(BF16, DN_WIDTH), wide(BF16, DN_WIDTH), wide(BF16, DN_WIDTH),
                   wide(BF16, DN_HEADS * DN_CHUNK),
                   jax.ShapeDtypeStruct((2, bsz, nch, 8, LANES), F32)],
        scratch_shapes=[pltpu.VMEM((t, DN_WIDTH), F32), pltpu.VMEM((t, DN_WIDTH), F32),
                        pltpu.VMEM((t, DN_WIDTH), F32), pltpu.VMEM((t, LANES), F32)],
        compiler_params=_params(("parallel", "parallel")),
        name="deltanet_prep",
    )(qkv_a, qkv_a, qkv_a, gates, cwp, alog, dtb, pre, suf)


def _dn_scan_kernel(u_ref, w_ref, qd_ref, kd_ref, at_ref, egl_ref, o_ref, st_ref, *, nc):
    ch = DN_CHUNK
    dk = DN_HEAD_DIM
    d = pl.program_id(0)

    @pl.when(pl.program_id(2) == 0)
    def _():
        st_ref[...] = jnp.zeros(st_ref.shape, st_ref.dtype)

    n_seq = u_ref.shape[0]
    units = [(sq, h) for sq in range(n_seq) for h in range(DN_HEADS)]
    hsl = [slice(h * dk, (h + 1) * dk) for h in range(DN_HEADS)]
    sts = [st_ref[sq * DN_HEADS + h] for sq, h in units]
    for j in range(nc):
        cc = j + d * (nc - 1 - 2 * j)
        rows = pl.ds(pl.multiple_of(cc * ch, ch), ch)
        egls = [egl_ref[sq, cc] for sq in range(n_seq)]
        sbs = [st.astype(BF16) for st in sts]
        vbs = [(u_ref[sq, rows, hsl[h]] - _dot(w_ref[sq, rows, hsl[h]], sb)).astype(BF16)
               for (sq, h), sb in zip(units, sbs)]
        qss = [_dot(qd_ref[sq, rows, hsl[h]], sb) for (sq, h), sb in zip(units, sbs)]
        sts = [st * egls[sq][h:h + 1, :] + _dot_tn(kd_ref[sq, rows, hsl[h]], vb)
               for (sq, h), st, vb in zip(units, sts, vbs)]
        for (sq, h), qs, vb in zip(units, qss, vbs):
            o_ref[sq, rows, hsl[h]] = qs + _dot(at_ref[sq, rows, h * ch:(h + 1) * ch], vb)
    for (sq, h), st in zip(units, sts):
        st_ref[sq * DN_HEADS + h] = st


def _dn_scan_call(u, w, qd, kd, at, egl, tc):
    _, bsz, s, wd = u.shape
    nb = s // tc
    nc = tc // DN_CHUNK
    n_seq = math.gcd(bsz, 4)

    def blk(d, n):
        return n + d * (nb - 1 - 2 * n)

    spec = lambda width: pl.BlockSpec((None, n_seq, tc, width), lambda d, b, n: (d, b, blk(d, n), 0))
    return pl.pallas_call(
        functools.partial(_dn_scan_kernel, nc=nc),
        grid=(2, bsz // n_seq, nb),
        in_specs=[spec(wd), spec(wd), spec(wd), spec(wd), spec(DN_HEADS * DN_CHUNK),
                  pl.BlockSpec((None, n_seq, nc, 8, LANES), lambda d, b, n: (d, b, blk(d, n), 0, 0))],
        out_specs=spec(wd),
        out_shape=jax.ShapeDtypeStruct((2, bsz, s, wd), F32),
        scratch_shapes=[pltpu.VMEM((n_seq * DN_HEADS, DN_HEAD_DIM, DN_HEAD_DIM), F32)],
        compiler_params=_params(("parallel", "parallel", "arbitrary")),
        name="deltanet_scan",
    )(u, w, qd, kd, at, egl)


def _band_attn_kernel(*refs, n_kv, grp, blk, tq, qs, t_len, with_sink, with_lse):
    q_ref, kp_ref, kc_ref, kn_ref, vp_ref, vc_ref, vn_ref = refs[:7]
    pos = 7
    sink_ref = None
    if with_sink:
        sink_ref = refs[pos]
        pos += 1
    o_ref = refs[pos]
    lse_ref = refs[pos + 1] if with_lse else None
    hd = HEAD_DIM
    i0 = pl.program_id(2) * tq
    kcat = jnp.concatenate([kp_ref[...], kc_ref[...], kn_ref[...]], axis=0)
    vcat = jnp.concatenate([vp_ref[...], vc_ref[...], vn_ref[...]], axis=0)
    kwin = qs + 2 * blk
    n_q = n_kv * grp
    n_sub = tq // qs
    lse_lane = lax.broadcasted_iota(I32, (qs, LSE_COLS), 1)
    ones_k = jnp.ones((kwin, LSE_COLS), BF16)
    lses = [jnp.zeros((qs, LSE_COLS), F32) for _ in range(n_sub)]
    outs = [[] for _ in range(n_sub)]
    biases, khs, vhs = [], [], []
    for sub in range(n_sub):
        k0 = sub * qs
        rowpos = i0 + k0 + lax.broadcasted_iota(I32, (qs, kwin), 0)
        keypos = i0 - blk + k0 + lax.broadcasted_iota(I32, (qs, kwin), 1)
        mask = (jnp.abs(keypos - rowpos) <= blk) & (keypos >= 0) & (keypos < t_len)
        biases.append(jnp.where(mask, 0.0, NEG))
        khs.append([kcat[k0:k0 + kwin, kv * hd:(kv + 1) * hd] for kv in range(n_kv)])
        vhs.append([vcat[k0:k0 + kwin, kv * hd:(kv + 1) * hd] for kv in range(n_kv)])
    units = [(sub, hq) for sub in range(n_sub) for hq in range(n_q)]
    for u0 in range(0, len(units), ATTN_UNITS_PER_STAGE):
        stage = units[u0:u0 + ATTN_UNITS_PER_STAGE]
        scs = [_dot_nt(q_ref[sub * qs:(sub + 1) * qs, hq * hd:(hq + 1) * hd], khs[sub][hq // grp]) + biases[sub]
               for sub, hq in stage]
        ps, ms = [], []
        for (sub, hq), sc in zip(stage, scs):
            m = jnp.max(sc, axis=-1, keepdims=True)
            if with_sink:
                m = jnp.maximum(m, sink_ref[hq])
            ps.append(jnp.exp((sc - m).astype(BF16)))
            ms.append(m)
        dens = [_dot(p, ones_k) for p in ps]
        for (sub, hq), p, m, den in zip(stage, ps, ms, dens):
            if with_sink:
                den = den + jnp.exp(sink_ref[hq] - m)
            if with_lse:
                lses[sub] = jnp.where(lse_lane == hq, m + jnp.log(den), lses[sub])
            outs[sub].append(_dot(p, vhs[sub][hq // grp]) / den[:, :hd])
    for sub in range(n_sub):
        o_ref[sub * qs:(sub + 1) * qs, :] = jnp.concatenate(outs[sub], axis=1).astype(o_ref.dtype)
        if with_lse:
            lse_ref[sub * qs:(sub + 1) * qs, :] = lses[sub]


LSE_COLS = LANES
ATTN_UNITS_PER_STAGE = 8


def _band_attn_call(q, k, v, *, n_kv, grp, blk, sinks, with_lse, name):
    bsz, n_res, t_len, _ = q.shape
    tq = min(512, t_len)
    qs = min(128, tq)
    nt = t_len // tq
    ratio = tq // blk
    nblk = t_len // blk
    qw = n_kv * grp * HEAD_DIM
    kw = n_kv * HEAD_DIM
    cur = lambda wd: pl.BlockSpec((None, None, tq, wd), lambda b, r, i: (b, r, i, 0))
    prev = lambda wd: pl.BlockSpec((None, None, blk, wd), lambda b, r, i: (b, r, jnp.maximum(i * ratio - 1, 0), 0))
    nxt = lambda wd: pl.BlockSpec((None, None, blk, wd),
                                  lambda b, r, i: (b, r, jnp.minimum((i + 1) * ratio, nblk - 1), 0))
    in_specs = [cur(qw), prev(kw), cur(kw), nxt(kw), prev(kw), cur(kw), nxt(kw)]
    args = [q, k, k, k, v, v, v]
    if sinks is not None:
        in_specs.append(pl.BlockSpec(memory_space=pltpu.SMEM))
        args.append(sinks)
    out_shape = [jax.ShapeDtypeStruct((bsz, n_res, t_len, qw), BF16)]
    out_specs = [cur(qw)]
    if with_lse:
        out_shape.append(jax.ShapeDtypeStruct((bsz, n_res, t_len, LSE_COLS), F32))
        out_specs.append(cur(LSE_COLS))
    return pl.pallas_call(
        functools.partial(_band_attn_kernel, n_kv=n_kv, grp=grp, blk=blk, tq=tq, qs=qs, t_len=t_len,
                          with_sink=sinks is not None, with_lse=with_lse),
        grid=(bsz, n_res, nt),
        in_specs=in_specs,
        out_specs=out_specs,
        out_shape=out_shape,
        compiler_params=_params(("parallel", "parallel", "parallel")),
        name=name,
    )(*args)


def _dilated_group(qg, kg, vg, window, dil):
    return _band_attn_call(qg, kg, vg, n_kv=DIL_HEADS_PER_GROUP, grp=1, blk=window // (2 * dil), sinks=None,
                           with_lse=True, name=f"dilated_attn_{dil}")


def _layer_norm(r, g, b):
    mu = jnp.mean(r, -1, keepdims=True)
    var = jnp.mean(jnp.square(r - mu), -1, keepdims=True)
    return (r - mu) * lax.rsqrt(var + LN_EPS) * g + b


def _post_mix(y, x_ref, mod_ref, lng_ref, lnb_ref, rwt_ref, x1_ref, aff_ref):
    m = mod_ref[...]
    x1 = _layer_norm(ALPHA * x_ref[...] + (1.0 + m[2:3]) * y, lng_ref[...], lnb_ref[...])
    x1_ref[...] = x1.reshape(x1_ref.shape)
    h2 = x1 * (1.0 + m[4:5]) + m[3:4]
    split = lambda a: (a.astype(BF16), (a - a.astype(BF16).astype(F32)).astype(BF16))
    (w_hi, w_lo), (h_hi, h_lo) = split(rwt_ref[...]), split(h2)
    logits = _dot_nt(w_hi, h_hi) + (_dot_nt(w_hi, h_lo) + _dot_nt(w_lo, h_hi))
    e = jnp.exp(logits - jnp.max(logits, axis=0, keepdims=True))
    aff_ref[...] = e / jnp.sum(e, axis=0, keepdims=True)


def _position_major(ref, scratch):
    dil, n, wd = ref.shape
    if dil == 1:
        return ref[0].astype(F32)
    for r in range(dil):
        a = ref[r].astype(F32)
        for c in range(wd // LANES):
            scratch[c, pl.ds(r, n, stride=dil), :] = a[:, c * LANES:(c + 1) * LANES]
    return jnp.concatenate([scratch[c] for c in range(wd // LANES)], axis=1)


def _outproj0_kernel(of_ref, ob_ref, z_ref, dnn_ref, og0_ref, og1_ref, og2_ref, l0_ref, l1_ref, l2_ref,
                     w_ref, x_ref, mod_ref, lng_ref, lnb_ref, rwt_ref, x1_ref, aff_ref, regroup_ref):
    dk = DN_HEAD_DIM
    od = of_ref[...] + ob_ref[...]
    z = z_ref[...]
    parts = []
    for h in range(DN_HEADS):
        oh = od[:, h * dk:(h + 1) * dk]
        oh = oh * lax.rsqrt(jnp.mean(oh * oh, -1, keepdims=True) + NORM_EPS) * dnn_ref[...]
        parts.append(oh * _silu(z[:, h * dk:(h + 1) * dk]))
    o_dn = jnp.concatenate(parts, axis=1).astype(BF16)
    l0, l1, l2 = (_position_major(r, regroup_ref) for r in (l0_ref, l1_ref, l2_ref))
    og0, og1, og2 = (_position_major(r, regroup_ref) for r in (og0_ref, og1_ref, og2_ref))
    mx = jnp.maximum(jnp.maximum(l0, l1), l2)
    e0, e1, e2 = jnp.exp(l0 - mx), jnp.exp(l1 - mx), jnp.exp(l2 - mx)
    den = e0 + e1 + e2
    head = lax.broadcasted_iota(I32, (l0.shape[0], DIL_GROUP_WIDTH), 1) >> int(math.log2(HEAD_DIM))

    def per_lane(wt):
        out = jnp.zeros(head.shape, F32)
        for h in range(DIL_HEADS_PER_GROUP):
            out = jnp.where(head == h, wt[:, h:h + 1], out)
        return out

    o_dil = (per_lane(e0 / den) * og0 + per_lane(e1 / den) * og1 + per_lane(e2 / den) * og2).astype(BF16)
    y = _dot(o_dn, w_ref[0:DN_WIDTH, :]) + _dot(o_dil, w_ref[DN_WIDTH:DN_WIDTH + DIL_GROUP_WIDTH, :])
    _post_mix(y, x_ref, mod_ref, lng_ref, lnb_ref, rwt_ref, x1_ref, aff_ref)


def _outproj1_kernel(o_ref, w_ref, x_ref, mod_ref, lng_ref, lnb_ref, rwt_ref, x1_ref, aff_ref):
    y = _dot(o_ref[...], w_ref[...])
    _post_mix(y, x_ref, mod_ref, lng_ref, lnb_ref, rwt_ref, x1_ref, aff_ref)


def _tail_specs(bsz, s, d, tm, n_e):
    row = lambda wd: pl.BlockSpec((None, tm, wd), lambda b, i: (b, i, 0))
    const = lambda shp: pl.BlockSpec(shp, lambda b, i: tuple(0 for _ in shp))
    in_specs = [row(d), pl.BlockSpec((None, 6, d), lambda b, i: (b, 0, 0)), const((1, d)), const((1, d)),
                const((n_e, d))]
    out_specs = [pl.BlockSpec((None, tm, d // LANES, LANES), lambda b, i: (b, i, 0, 0)),
                 pl.BlockSpec((None, n_e, tm), lambda b, i: (b, 0, i))]
    out_shape = [jax.ShapeDtypeStruct((bsz, s, d // LANES, LANES), F32), jax.ShapeDtypeStruct((bsz, n_e, s), F32)]
    return in_specs, out_specs, out_shape


def _outproj0_call(o_scan, z, dn_norm, ogs, lses, w_out, x, mod, ln_g, ln_b, router_w, tm):
    bsz, s, d = x.shape
    n_e = router_w.shape[1]
    tail_in, out_specs, out_shape = _tail_specs(bsz, s, d, tm, n_e)
    row = lambda wd: pl.BlockSpec((None, tm, wd), lambda b, i: (b, i, 0))
    dirspec = lambda dd: pl.BlockSpec((None, None, tm, DN_WIDTH), lambda b, i: (dd, b, i, 0))
    gw = DIL_GROUP_WIDTH
    resid = lambda a: pl.BlockSpec((None, a.shape[1], tm // a.shape[1], a.shape[3]), lambda b, i: (b, 0, i, 0))
    in_specs = [dirspec(0), dirspec(1), row(DN_WIDTH), pl.BlockSpec((1, DN_HEAD_DIM), lambda b, i: (0, 0))]
    in_specs += [resid(a) for a in ogs] + [resid(a) for a in lses]
    in_specs += [pl.BlockSpec(w_out.shape, lambda b, i: (0, 0))] + tail_in
    return pl.pallas_call(
        _outproj0_kernel, grid=(bsz, s // tm), in_specs=in_specs, out_specs=out_specs, out_shape=out_shape,
        scratch_shapes=[pltpu.VMEM((gw // LANES, tm, LANES), F32)],
        compiler_params=_params(("parallel", "parallel")), name="outproj_deltanet_dilated",
    )(o_scan, o_scan, z, dn_norm.reshape(1, -1), *ogs, *lses, w_out, x, mod, ln_g.reshape(1, d), ln_b.reshape(1, d),
      router_w.T)


def _outproj1_call(o, w_out, x, mod, ln_g, ln_b, router_w, tm):
    bsz, s, d = x.shape
    n_e = router_w.shape[1]
    tail_in, out_specs, out_shape = _tail_specs(bsz, s, d, tm, n_e)
    in_specs = [pl.BlockSpec((None, tm, o.shape[-1]), lambda b, i: (b, i, 0)),
                pl.BlockSpec(w_out.shape, lambda b, i: (0, 0))] + tail_in
    return pl.pallas_call(
        _outproj1_kernel, grid=(bsz, s // tm), in_specs=in_specs, out_specs=out_specs, out_shape=out_shape,
        compiler_params=_params(("parallel", "parallel")), name="outproj_swa",
    )(o, w_out, x, mod, ln_g.reshape(1, d), ln_b.reshape(1, d), router_w.T)


def _topk_kernel(a_ref, idx_ref, *, cap, jb):
    a = a_ref[...]
    n_e, rows, _ = a.shape
    bits = pltpu.bitcast(a, I32)
    thr = jnp.zeros((n_e, 1, 1), I32)
    for bit in range(30, -1, -1):
        cand = thr | (1 << bit)
        cnt = jnp.sum(jnp.where(bits >= cand, 1, 0), axis=(1, 2), keepdims=True)
        thr = jnp.where(cnt >= cap, cand, thr)
    gt = jnp.where(bits > thr, 1.0, 0.0).astype(F32)
    eq = jnp.where(bits == thr, 1.0, 0.0).astype(F32)
    need = cap - jnp.sum(gt, axis=(1, 2), keepdims=True)
    ru = lax.broadcasted_iota(I32, (LANES, LANES), 0)
    cu = lax.broadcasted_iota(I32, (LANES, LANES), 1)
    upper = jnp.where(ru <= cu, 1.0, 0.0).astype(BF16)
    rl = lax.broadcasted_iota(I32, (rows, rows), 0)
    cl = lax.broadcasted_iota(I32, (rows, rows), 1)
    lstrict = jnp.where(cl < rl, 1.0, 0.0).astype(BF16)

    def fold_cumsum(xs):
        withins = [_dot(x.astype(BF16), upper) for x in xs]
        rowtots = [jnp.broadcast_to(w[:, LANES - 1:LANES], w.shape) for w in withins]
        befores = [_dot(lstrict, r.astype(BF16)) for r in rowtots]
        return [w + b for w, b in zip(withins, befores)], [b + r for b, r in zip(befores, rowtots)]

    experts = range(n_e)
    eq_counts, _ = fold_cumsum([eq[e] for e in experts])
    sels = [jnp.maximum(gt[e], jnp.where(eq_counts[e] - eq[e] < need[e], eq[e], 0.0)) for e in experts]
    counts, count_ends = fold_cumsum(sels)
    rowid = lax.broadcasted_iota(I32, (rows, jb), 0).astype(F32)
    for j0 in range(0, cap, jb):
        slot = (j0 + lax.broadcasted_iota(I32, (1, jb), 1)).astype(F32)
        row = [jnp.sum(jnp.where(count_ends[e][:, 0:1] <= slot, 1.0, 0.0), axis=0, keepdims=True) for e in experts]
        onehot = [jnp.where(rowid == row[e], 1.0, 0.0) for e in experts]
        count_row = [_dot_tn(counts[e], onehot[e], HI) for e in experts]
        for e in experts:
            lane = jnp.sum(jnp.where(count_row[e] <= slot, 1.0, 0.0), axis=0, keepdims=True)
            idx_ref[e:e + 1, j0:j0 + jb] = (row[e] * LANES + lane).astype(I32)


def _topk_call(aff, cap):
    bsz, n_e, s = aff.shape
    rows = s // LANES
    return pl.pallas_call(
        functools.partial(_topk_kernel, cap=cap, jb=min(512, cap)),
        grid=(bsz,),
        in_specs=[pl.BlockSpec((None, n_e, rows, LANES), lambda b: (b, 0, 0, 0))],
        out_specs=pl.BlockSpec((None, n_e, cap), lambda b: (b, 0, 0)),
        out_shape=jax.ShapeDtypeStruct((bsz, n_e, cap), I32),
        compiler_params=_params(("parallel",)), name="topk_route",
    )(aff.reshape(bsz, n_e, rows, LANES))


SUBLANES = 8


def _moe_kernel(idx_hbm, aff_hbm, x_hbm, mod_ref, wg_ref, wu_ref, wd_ref, out_hbm,
                idx_s, aff_s, xg0, xg1, hid_s, y, acc, sem_i, sem_a, sem_g, sem_o, *, cap, n_f, y_bounds):
    b = pl.program_id(0)
    e = pl.program_id(1)
    f = pl.program_id(2)
    n_b = pl.num_programs(0)
    n_e = pl.num_programs(1)
    per_f = cap // n_f
    table = idx_s.shape[0] // 2
    cur_table = (b % 2) * table
    more = b + 1 < n_b

    def row_copy(src_b, tok, j, dst, sem):
        return pltpu.make_async_copy(x_hbm.at[src_b, tok], dst.at[j], sem)

    def wait_rows(dst, sem):
        pltpu.make_async_copy(x_hbm.at[b, pl.ds(0, cap)], dst, sem).wait()

    def table_copy(src_b):
        return pltpu.make_async_copy(idx_hbm.at[src_b], idx_s.at[pl.ds((src_b % 2) * table, table)], sem_i)

    @pl.when((e == 0) & (f == 0))
    def _():
        acc[...] = jnp.zeros(acc.shape, acc.dtype)

        @pl.when(b == 0)
        def _():
            table_copy(b).start()
            table_copy(b).wait()

            def gather(jo, carry):
                for r in range(SUBLANES):
                    j = jo * SUBLANES + r
                    row_copy(b, idx_s[j], j, xg0, sem_g.at[0]).start(priority=r % 2)
                return carry

            lax.fori_loop(0, cap // SUBLANES, gather, 0)
            wait_rows(xg0, sem_g.at[0])

    @pl.when((f == 0) & more & (e == n_e - 2))
    def _():
        table_copy(b + 1).start()

    @pl.when((f == 0) & more & (e == n_e - 1))
    def _():
        table_copy(b + 1).wait()

    aff_copy = pltpu.make_async_copy(aff_hbm.at[b, e], aff_s, sem_a)

    @pl.when(f == 0)
    def _():
        aff_copy.start()

    def expert_step(xg_cur, xg_next, sem_next):
        last = e == n_e - 1
        src_b = jnp.where(last & more, b + 1, b)
        next_base = jnp.where(last, jnp.where(more, ((b + 1) % 2) * table, cur_table + e * cap),
                              cur_table + (e + 1) * cap) + f * per_f
        for r in range(per_f):
            row_copy(src_b, idx_s[next_base + r], f * per_f + r, xg_next, sem_next).start(priority=r % 2)
        m = mod_ref[...]
        xv = (xg_cur[...].reshape(cap, m.shape[-1]) * (1.0 + m[4:5]) + m[3:4]).astype(BF16)
        hid_s[f] = (_silu(_dot(xv, wg_ref[...])) * _dot(xv, wu_ref[...])).astype(BF16)

        @pl.when(f == n_f - 1)
        def _():
            aff_copy.wait()

            def scatter_rows(j0):
                yc = y[pl.ds(j0, SUBLANES), :].reshape((SUBLANES,) + acc.shape[1:])
                toks = [idx_s[cur_table + e * cap + j0 + r] for r in range(SUBLANES)]
                olds = [acc[toks[r]] for r in range(SUBLANES)]
                for r in range(SUBLANES):
                    acc[toks[r]] = olds[r] + aff_s[toks[r]] * yc[r]

            for p, (r0, r1) in enumerate(zip(y_bounds[:-1], y_bounds[1:])):
                hid = jnp.concatenate([hid_s[i, r0:r1, :] for i in range(n_f)], axis=1)
                y[r0:r1, :] = _dot(hid, wd_ref[...])
                if p > 0:
                    for j0 in range(y_bounds[p - 1], r0, SUBLANES):
                        scatter_rows(j0)

            def scatter_last(jo, carry):
                scatter_rows(pl.multiple_of(y_bounds[-2] + jo * SUBLANES, SUBLANES))
                return carry

            lax.fori_loop(0, (y_bounds[-1] - y_bounds[-2]) // SUBLANES, scatter_last, 0)
            wait_rows(xg_next, sem_next)

            @pl.when(e == n_e - 1)
            def _():
                co = pltpu.make_async_copy(acc, out_hbm.at[b], sem_o)
                co.start()
                co.wait()

    @pl.when(e % 2 == 0)
    def _():
        expert_step(xg0, xg1, sem_g.at[1])

    @pl.when(e % 2 == 1)
    def _():
        expert_step(xg1, xg0, sem_g.at[0])


def _moe_call(x1, mod, idx, aff, wg, wu, wd, fcw):
    bsz, s, nt, _ = x1.shape
    d = nt * LANES
    n_e, _, ff = wg.shape
    cap = idx.shape[-1]
    fcw = min(fcw, ff)
    n_f = ff // fcw
    y_bounds = (0, cap // 2, cap // 2 + 3 * cap // 8, cap)
    assert n_e % 2 == 0 and cap % n_f == 0 and cap % (8 * SUBLANES) == 0 and d // LANES == SUBLANES
    anyspec = pl.BlockSpec(memory_space=pl.ANY)
    return pl.pallas_call(
        functools.partial(_moe_kernel, cap=cap, n_f=n_f, y_bounds=y_bounds),
        grid=(bsz, n_e, n_f),
        in_specs=[anyspec, anyspec, anyspec,
                  pl.BlockSpec((None, 6, d), lambda b, e, f: (b, 0, 0)),
                  pl.BlockSpec((None, d, fcw), lambda b, e, f: (e, 0, f)),
                  pl.BlockSpec((None, d, fcw), lambda b, e, f: (e, 0, f)),
                  pl.BlockSpec((None, ff, d), lambda b, e, f: (e, 0, 0))],
        out_specs=anyspec,
        out_shape=jax.ShapeDtypeStruct((bsz, s, d // LANES, LANES), F32),
        scratch_shapes=[pltpu.SMEM((2 * n_e * cap,), I32), pltpu.SMEM((s,), F32),
                        pltpu.VMEM((cap, nt, LANES), F32), pltpu.VMEM((cap, nt, LANES), F32),
                        pltpu.VMEM((n_f, cap, fcw), BF16), pltpu.VMEM((cap, d), F32),
                        pltpu.VMEM((s, nt, LANES), F32),
                        pltpu.SemaphoreType.DMA(()), pltpu.SemaphoreType.DMA(()),
                        pltpu.SemaphoreType.DMA((2,)), pltpu.SemaphoreType.DMA(())],
        compiler_params=_params(("arbitrary", "arbitrary", "arbitrary"), MOE_VMEM_LIMIT),
        name="moe_experts",
    )(idx.reshape(bsz, n_e * cap), aff, x1, mod, wg, wu, wd)


def _ln2_kernel(x_ref, y_ref, g2_ref, g_ref, b_ref, o_ref):
    r = ALPHA * x_ref[...] + (1.0 + g2_ref[...]) * y_ref[...]
    o_ref[...] = _layer_norm(r.reshape(o_ref.shape), g_ref[...], b_ref[...])


def _ln2_call(x1, moe, mod, g, b, tm):
    bsz, s, nt, _ = x1.shape
    d = nt * LANES
    row = pl.BlockSpec((None, tm, d), lambda bb, i: (bb, i, 0))
    tiles = pl.BlockSpec((None, tm, nt, LANES), lambda bb, i: (bb, i, 0, 0))
    vec = pl.BlockSpec((1, d), lambda bb, i: (0, 0))
    return pl.pallas_call(
        _ln2_kernel, grid=(bsz, s // tm),
        in_specs=[tiles, tiles, pl.BlockSpec((None, 1, nt, LANES), lambda bb, i: (bb, 0, 0, 0)), vec, vec],
        out_specs=row, out_shape=jax.ShapeDtypeStruct((bsz, s, d), F32),
        compiler_params=_params(("parallel", "parallel")), name="ffn_postnorm",
    )(x1, moe, mod[:, 5].reshape(bsz, 1, nt, LANES), g.reshape(1, d), b.reshape(1, d))


def _ffn_block(x1, aff, mod, wg, wu, wd, ln_g, ln_b, tm):
    s = x1.shape[1]
    cap = (EC_FACTOR * s) // N_EXPERTS
    idx = _topk_call(aff, cap)
    moe = _moe_call(x1, mod, idx, aff, wg.astype(BF16), wu.astype(BF16), wd.astype(BF16), fcw=512)
    return _ln2_call(x1, moe, mod, ln_g, ln_b, tm)


def kernel(x, c, positions, ada_w, ada_b, ab_w_in, ab_conv_w, ab_a_log, ab_dt_bias, ab_dn_norm, ab_w_out, swa_w_in,
           swa_sinks, swa_w_out, ln_mix_g, ln_mix_b, router_w, moe_w_gate, moe_w_up, moe_w_down, ln_ffn_g, ln_ffn_b):
    bsz, s, d = x.shape
    tm = min(512, s)
    mod = _mod_call(c, ada_w, ada_b).reshape(DEPTH, bsz, 6, d)
    rope = _rope_tables(positions)

    w_in = ab_w_in[0]
    n_a = 4 * DN_WIDTH
    n_g = 4 * DN_HEADS
    gw = DIL_GROUP_WIDTH
    q_scale = HEAD_DIM ** -0.5
    cols = [w_in[:, :n_a], jnp.pad(w_in[:, n_a:n_a + n_g], ((0, 0), (0, LANES - n_g)))]
    groups = [(0, 3 * DN_WIDTH, 0, False), (3 * DN_WIDTH, DN_WIDTH, 1, False), (n_a, LANES, 2, False)]
    for gi in range(len(DIL_PAIRS)):
        for part in range(3):
            c0 = n_a + n_g + part * DIL_WIDTH + gi * gw
            cols.append(w_in[:, c0:c0 + gw] * (q_scale if part == 0 else 1.0))
            groups.append((n_a + LANES + (3 * gi + part) * gw, gw, 3 + 3 * gi + part, part < 2))
    w0 = jnp.concatenate(cols, axis=1).astype(BF16)
    dils = tuple(dil for _, dil in DIL_PAIRS for _ in range(3))
    outs0 = _inproj_call(
        x, mod[0], w0, rope, _chunk_plan(groups), (3 * DN_WIDTH, DN_WIDTH, LANES) + (gw,) * 9,
        (F32, F32, F32) + (BF16,) * 9, (1, 1, 1) + dils, tm, "inproj_deltanet_dilated")
    qkv_a, z, gates = outs0[:3]
    u, w, qd, kd, at, egl = _dn_prep_call(qkv_a, gates, ab_conv_w[0], ab_a_log[0], ab_dt_bias[0], tm)
    o_scan = _dn_scan_call(u, w, qd, kd, at, egl, min(256, s))
    ogs, lses = [], []
    for gi, (window, dil) in enumerate(DIL_PAIRS):
        qkv_g = [a.reshape(bsz, dil, s // dil, gw) for a in outs0[3 + 3 * gi:6 + 3 * gi]]
        o_g, lse_g = _dilated_group(*qkv_g, window, dil)
        ogs.append(o_g)
        lses.append(lse_g)
    x1, aff = _outproj0_call(o_scan, z, ab_dn_norm[0], ogs, lses, ab_w_out[0].astype(BF16), x, mod[0],
                             ln_mix_g[0], ln_mix_b[0], router_w[0], tm)
    x = _ffn_block(x1, aff, mod[0], moe_w_gate[0], moe_w_up[0], moe_w_down[0], ln_ffn_g[0], ln_ffn_b[0], tm)

    qw = SWA_Q_HEADS * HEAD_DIM
    kw = SWA_KV_HEADS * HEAD_DIM
    plan1 = _chunk_plan([(0, qw, 0, True), (qw, kw, 1, True), (qw + kw, kw, 2, False)])
    w1 = jnp.concatenate([swa_w_in[0][:, :qw] * q_scale, swa_w_in[0][:, qw:]], axis=1).astype(BF16)
    q1, k1, v1 = _inproj_call(x, mod[1], w1, rope, plan1, (qw, kw, kw), (BF16, BF16, BF16), (1, 1, 1), tm,
                              "inproj_swa")
    (o1,) = _band_attn_call(q1[:, None], k1[:, None], v1[:, None], n_kv=SWA_KV_HEADS,
                            grp=SWA_Q_HEADS // SWA_KV_HEADS, blk=SWA_WINDOW, sinks=swa_sinks[0], with_lse=False,
                            name="swa_attn")
    x1, aff = _outproj1_call(o1[:, 0], swa_w_out[0].astype(BF16), x, mod[1], ln_mix_g[1], ln_mix_b[1], router_w[1],
                             tm)
    x = _ffn_block(x1, aff, mod[1], moe_w_gate[1], moe_w_up[1], moe_w_down[1], ln_ffn_g[1], ln_ffn_b[1], tm)
    return x
```

```python
import functools
import math

import jax
import jax.numpy as jnp
from jax import lax
from jax.experimental import pallas as pl
from jax.experimental.pallas import tpu as pltpu

F32 = jnp.float32
BF16 = jnp.bfloat16
I32 = jnp.int32
HI = lax.Precision.HIGHEST

DEPTH = 2
HEAD_DIM = 64
ROT_DIM = HEAD_DIM // 4
ROPE_THETA = 500000.0
DN_HEADS = 4
DN_HEAD_DIM = 128
DN_CHUNK = 64
DN_CONV = 5
DN_WIDTH = DN_HEADS * DN_HEAD_DIM
DIL_PAIRS = ((128, 1), (512, 4), (2048, 16))
DIL_HEADS_PER_GROUP = 4
DIL_GROUP_WIDTH = DIL_HEADS_PER_GROUP * HEAD_DIM
DIL_WIDTH = DIL_GROUP_WIDTH * len(DIL_PAIRS)
SWA_Q_HEADS = 16
SWA_KV_HEADS = 4
SWA_WINDOW = 128
N_EXPERTS = 16
EC_FACTOR = 2
ALPHA = (2.0 * DEPTH) ** 0.25
LN_EPS = 1e-5
NORM_EPS = 1e-6
NEG = -1e30
LANES = 128
HALO = 8
VMEM_LIMIT = 56 * 1024 * 1024
MOE_VMEM_LIMIT = 60 * 1024 * 1024


def _dot(a, b, prec=None):
    return jnp.dot(a, b, preferred_element_type=F32, precision=prec)


def _dot_nt(a, b, prec=None):
    return lax.dot_general(a, b, (((1,), (1,)), ((), ())), preferred_element_type=F32, precision=prec)


def _dot_tn(a, b, prec=None):
    return lax.dot_general(a, b, (((0,), (0,)), ((), ())), preferred_element_type=F32, precision=prec)


def _silu(x):
    return x * (0.5 * jnp.tanh(0.5 * x) + 0.5)


def _params(sem, vmem_limit=VMEM_LIMIT):
    return pltpu.CompilerParams(dimension_semantics=sem, vmem_limit_bytes=vmem_limit)


def _mod_kernel(c_ref, w_ref, b_ref, o_ref):
    o_ref[...] = _dot(_silu(c_ref[...]), w_ref[...], HI) + b_ref[...]


def _mod_call(c, ada_w, ada_b):
    depth, d, n6 = ada_w.shape
    bsz = c.shape[0]
    tn = n6 // 4
    return pl.pallas_call(
        _mod_kernel,
        grid=(depth, n6 // tn),
        in_specs=[pl.BlockSpec((bsz, d), lambda i, j: (0, 0)),
                  pl.BlockSpec((None, d, tn), lambda i, j: (i, 0, j)),
                  pl.BlockSpec((None, 1, tn), lambda i, j: (i, 0, j))],
        out_specs=pl.BlockSpec((None, bsz, tn), lambda i, j: (i, 0, j)),
        out_shape=jax.ShapeDtypeStruct((depth, bsz, n6), F32),
        compiler_params=_params(("arbitrary", "arbitrary")),
        name="adaln_mod",
    )(c, ada_w, ada_b.reshape(depth, 1, n6))


def _inproj_kernel(x_ref, mod_ref, w_ref, rope_ref, *refs, plan):
    *out_refs, regroup_ref = refs
    m = mod_ref[...]
    h = (x_ref[...] * (1.0 + m[1:2]) + m[0:1]).astype(BF16)
    half = ROT_DIM // 2
    if any(p[4] for p in plan):
        src = lax.broadcasted_iota(I32, (ROT_DIM, LANES), 0)
        lane = lax.broadcasted_iota(I32, (ROT_DIM, LANES), 1) & (HEAD_DIM - 1)
        pick_cos = jnp.where(lane < ROT_DIM, jnp.where((lane & (half - 1)) == src, 1.0, 0.0), 0.0)
        pick_sin = jnp.where(lane < half, jnp.where(src == lane + half, -1.0, 0.0),
                             jnp.where(lane < ROT_DIM, jnp.where(src == lane, 1.0, 0.0), 0.0))
        cs = rope_ref[...]
        lane1 = lax.broadcasted_iota(I32, (1, LANES), 1) & (HEAD_DIM - 1)
        cos_t = _dot(cs, pick_cos, HI) + jnp.where(lane1 < ROT_DIM, 0.0, 1.0)
        sin_t = _dot(cs, pick_sin, HI)
        sin_a = jnp.where(lane1 < half, sin_t, 0.0)
        sin_b = sin_t - sin_a
    for c0, width, oi, o0, rope in plan:
        acc = _dot(h, w_ref[:, c0:c0 + width])
        if rope:
            reps = width // LANES
            tile = lambda a: jnp.concatenate([a] * reps, axis=1)
            acc = (acc * tile(cos_t) + pltpu.roll(acc, width - half, 1) * tile(sin_a)
                   + pltpu.roll(acc, half, 1) * tile(sin_b))
        out = out_refs[oi]
        if len(out.shape) == 2:
            out[:, o0:o0 + width] = acc.astype(out.dtype)
        else:
            dil, n = out.shape[0], out.shape[1]
            for c in range(width // LANES):
                regroup_ref[c] = acc[:, c * LANES:(c + 1) * LANES]
            for r in range(dil):
                out[r] = jnp.concatenate([regroup_ref[c, pl.ds(r, n, stride=dil), :] for c in range(width // LANES)],
                                         axis=1).astype(out.dtype)


def _inproj_call(x, mod, w, rope, plan, out_widths, out_dtypes, out_dils, tm, name):
    bsz, s, d = x.shape
    n = w.shape[1]
    out_shape, out_specs = [], []
    for ow, od, dil in zip(out_widths, out_dtypes, out_dils):
        if dil == 1:
            out_shape.append(jax.ShapeDtypeStruct((bsz, s, ow), od))
            out_specs.append(pl.BlockSpec((None, tm, ow), lambda b, i: (b, i, 0)))
        else:
            out_shape.append(jax.ShapeDtypeStruct((bsz, dil, s // dil, ow), od))
            out_specs.append(pl.BlockSpec((None, dil, tm // dil, ow), lambda b, i: (b, 0, i, 0)))
    chunk_w = max(p[1] for p in plan)
    return pl.pallas_call(
        functools.partial(_inproj_kernel, plan=plan),
        grid=(bsz, s // tm),
        in_specs=[pl.BlockSpec((None, tm, d), lambda b, i: (b, i, 0)),
                  pl.BlockSpec((None, 6, d), lambda b, i: (b, 0, 0)),
                  pl.BlockSpec((d, n), lambda b, i: (0, 0)),
                  pl.BlockSpec((None, tm, ROT_DIM), lambda b, i: (b, i, 0))],
        out_specs=out_specs,
        out_shape=out_shape,
        scratch_shapes=[pltpu.VMEM((chunk_w // LANES, tm, LANES), F32)],
        compiler_params=_params(("parallel", "parallel")),
        name=name,
    )(x, mod, w, rope)


def _chunk_plan(groups, chunk=256):
    plan = []
    for c0, width, oi, rope in groups:
        off = 0
        while off < width:
            wd = min(chunk, width - off)
            plan.append((c0 + off, wd, oi, off, rope))
            off += wd
    return tuple(plan)


def _rope_tables(positions):
    inv_freq = jnp.power(ROPE_THETA, -jnp.arange(0, ROT_DIM, 2, dtype=F32) / ROT_DIM)
    ang = positions.astype(F32)[..., None] * inv_freq
    return jnp.concatenate([jnp.cos(ang), jnp.sin(ang)], -1)


def _dn_prep_kernel(xa_ref, top_ref, bot_ref, gt_ref, cw_ref, alog_ref, dtb_ref, pre_ref, suf_ref,
                    u_ref, w_ref, qd_ref, kd_ref, at_ref, egl_ref,
                    q_s, k_s, v_s, gc_s, *, t):
    ch = DN_CHUNK
    dk = DN_HEAD_DIM
    pad = (DN_CONV - 1) // 2
    has_top = pl.program_id(1) > 0
    has_bot = pl.program_id(1) < pl.num_programs(1) - 1
    for grp, dst in enumerate((q_s, k_s, v_s)):
        cols = slice(grp * DN_WIDTH, (grp + 1) * DN_WIDTH)
        xe = jnp.concatenate([jnp.where(has_top, top_ref[:, cols], 0.0), xa_ref[:, cols],
                              jnp.where(has_bot, bot_ref[:, cols], 0.0)], axis=0)
        y = jnp.zeros((t, DN_WIDTH), F32)
        for k in range(DN_CONV):
            y = y + xe[HALO - pad + k:HALO - pad + k + t, :] * cw_ref[k:k + 1, cols]
        y = _silu(y)
        if grp < 2:
            scale = dk ** -0.5 if grp == 0 else 1.0
            parts = []
            for h in range(DN_HEADS):
                yh = y[:, h * dk:(h + 1) * dk]
                parts.append(yh * lax.rsqrt(jnp.sum(yh * yh, -1, keepdims=True) + NORM_EPS) * scale)
            y = jnp.concatenate(parts, axis=1)
        dst[...] = y

    g = gt_ref[...]
    lane = lax.broadcasted_iota(I32, (t, LANES), 1)
    z = g + dtb_ref[...]
    softplus = jnp.maximum(z, 0.0) + jnp.log1p(jnp.exp(-jnp.abs(z)))
    dec = -jnp.exp(alog_ref[...]) * softplus
    gv = jnp.where(lane < 2 * DN_HEADS, dec, jnp.where(lane < 4 * DN_HEADS, jax.nn.sigmoid(g), 0.0))
    g_hi = gv.astype(BF16)
    rest = gv - g_hi.astype(F32)
    g_mid = rest.astype(BF16)
    g_lo = (rest - g_mid.astype(F32)).astype(BF16)
    pre, suf = pre_ref[...], suf_ref[...]
    gcf = _dot(pre, g_hi) + (_dot(pre, g_mid) + _dot(pre, g_lo))
    gcb = _dot(suf, g_hi) + (_dot(suf, g_mid) + _dot(suf, g_lo))
    gc_s[...] = jnp.where(lane < DN_HEADS, gcf, jnp.where(lane < 2 * DN_HEADS, gcb, gv))

    rr = lax.broadcasted_iota(I32, (ch, 2 * ch), 0)
    cc = lax.broadcasted_iota(I32, (ch, 2 * ch), 1)
    fwd = cc < ch
    cj = jnp.where(fwd, cc, cc - ch)
    ahead = jnp.where(fwd, cj - rr, rr - cj)
    incl = ahead <= 0
    strict = ahead < 0
    eye2 = jnp.where(cj == rr, 1.0, 0.0).astype(F32)

    def blockdiag(p):
        return jnp.concatenate([jnp.where(fwd, p, 0.0), jnp.where(fwd, 0.0, p)], axis=0).astype(BF16)

    n_sq = int(math.log2(ch)) - 1
    per_iter = math.gcd(t // ch, 8)

    def chunk_body(ci, carry):
        units = []
        for sub in range(per_iter):
            c = ci * per_iter + sub
            rows = pl.ds(pl.multiple_of(c * ch, ch), ch)
            gcc = gc_s[rows, :]
            gct = gcc.T
            for h in range(DN_HEADS):
                units.append((c, rows, gcc, gct, h))

        st, pws = [], []
        for c, rows, gcc, gct, h in units:
            bcast = lambda col, gcc=gcc: jnp.broadcast_to(gcc[:, col:col + 1], (ch, LANES))
            gf, gb = bcast(h), bcast(DN_HEADS + h)
            bf, bb = bcast(2 * DN_HEADS + h), bcast(3 * DN_HEADS + h)
            grow = jnp.concatenate([gct[h:h + 1, :], gct[DN_HEADS + h:DN_HEADS + h + 1, :]], axis=1)
            diff = jnp.where(fwd, gf, gb) - grow
            decay = jnp.where(incl, jnp.exp(jnp.where(incl, diff, 0.0)), 0.0)
            hs = slice(h * dk, (h + 1) * dk)
            k16 = k_s[rows, hs].astype(BF16)
            kk = jnp.concatenate([k16, k16], axis=0)
            lower = jnp.where(strict, jnp.where(fwd, bf, bb) * _dot_nt(k16, kk) * decay, 0.0)
            intra = (_dot_nt(q_s[rows, hs].astype(BF16), kk) * decay).astype(BF16)
            at_ref[0, rows, h * ch:(h + 1) * ch] = intra[:, :ch]
            at_ref[1, rows, h * ch:(h + 1) * ch] = intra[:, ch:]
            st.append((gf, gb, bf, bb))
            pws.append(-lower)
        ainvs = [eye2 + p for p in pws]
        pbds = [blockdiag(p) for p in pws]
        for _ in range(n_sq):
            pws = [_dot(p.astype(BF16), bd) for p, bd in zip(pws, pbds)]
            pbds = [blockdiag(p) for p in pws]
            ainvs = [a + _dot(a.astype(BF16), bd) for a, bd in zip(ainvs, pbds)]

        egl_f, egl_b = [], []
        for (c, rows, _, _, h), (gf, gb, bf, bb), ainv in zip(units, st, ainvs):
            hs = slice(h * dk, (h + 1) * dk)
            q = q_s[rows, hs]
            k = k_s[rows, hs]
            v = v_s[rows, hs]
            egf, egb = jnp.exp(gf), jnp.exp(gb)
            rhs = jnp.concatenate([jnp.concatenate([v * bf, k * bf * egf], axis=1),
                                   jnp.concatenate([v * bb, k * bb * egb], axis=1)], axis=0).astype(BF16)
            uw_f = _dot(jnp.where(fwd, ainv, 0.0).astype(BF16), rhs)
            uw_b = _dot(jnp.where(fwd, 0.0, ainv).astype(BF16), rhs)
            glf = jnp.broadcast_to(gf[ch - 1:ch, :], (ch, LANES))
            glb = jnp.broadcast_to(gb[0:1, :], (ch, LANES))
            u_ref[0, rows, hs] = uw_f[:, :dk]
            u_ref[1, rows, hs] = uw_b[:, :dk]
            w_ref[0, rows, hs] = uw_f[:, dk:].astype(BF16)
            w_ref[1, rows, hs] = uw_b[:, dk:].astype(BF16)
            qd_ref[0, rows, hs] = (q * egf).astype(BF16)
            qd_ref[1, rows, hs] = (q * egb).astype(BF16)
            kd_ref[0, rows, hs] = (k * jnp.exp(glf - gf)).astype(BF16)
            kd_ref[1, rows, hs] = (k * jnp.exp(glb - gb)).astype(BF16)
            egl_f.append(jnp.exp(glf[0:1, :]))
            egl_b.append(jnp.exp(glb[0:1, :]))
            if h == DN_HEADS - 1:
                fill = [jnp.zeros((8 - DN_HEADS, LANES), F32)]
                egl_ref[0, c] = jnp.concatenate(egl_f + fill, axis=0)
                egl_ref[1, c] = jnp.concatenate(egl_b + fill, axis=0)
                egl_f, egl_b = [], []
        return carry

    lax.fori_loop(0, t // (ch * per_iter), chunk_body, 0)


def _dn_prep_call(qkv_a, gates, conv_w, a_log, dt_bias, t):
    bsz, s, cw = qkv_a.shape
    nt = s // t
    per_tile = t // HALO
    n_halo = s // HALO
    cwp = jnp.zeros((8, cw), F32).at[:DN_CONV].set(conv_w)
    alog = jnp.zeros((1, LANES), F32).at[0, :2 * DN_HEADS].set(a_log.reshape(-1))
    dtb = jnp.zeros((1, LANES), F32).at[0, :2 * DN_HEADS].set(dt_bias.reshape(-1))
    pos = jnp.arange(t)
    same_chunk = (pos[:, None] // DN_CHUNK) == (pos[None, :] // DN_CHUNK)
    pre = (same_chunk & (pos[None, :] <= pos[:, None])).astype(BF16)
    suf = (same_chunk & (pos[None, :] >= pos[:, None])).astype(BF16)
    nch = s // DN_CHUNK
    wide = lambda dt, wd: jax.ShapeDtypeStruct((2, bsz, s, wd), dt)
    spec = lambda wd: pl.BlockSpec((2, None, t, wd), lambda b, i: (0, b, i, 0))
    return pl.pallas_call(
        functools.partial(_dn_prep_kernel, t=t),
        grid=(bsz, nt),
        in_specs=[pl.BlockSpec((None, t, cw), lambda b, i: (b, i, 0)),
                  pl.BlockSpec((None, HALO, cw), lambda b, i: (b, jnp.maximum(i * per_tile - 1, 0), 0)),
                  pl.BlockSpec((None, HALO, cw), lambda b, i: (b, jnp.minimum((i + 1) * per_tile, n_halo - 1), 0)),
                  pl.BlockSpec((None, t, LANES), lambda b, i: (b, i, 0)),
                  pl.BlockSpec((8, cw), lambda b, i: (0, 0)),
                  pl.BlockSpec((1, LANES), lambda b, i: (0, 0)),
                  pl.BlockSpec((1, LANES), lambda b, i: (0, 0)),
                  pl.BlockSpec((t, t), lambda b, i: (0, 0)),
                  pl.BlockSpec((t, t), lambda b, i: (0, 0))],
        out_specs=[spec(DN_WIDTH), spec(DN_WIDTH), spec(DN_WIDTH), spec(DN_WIDTH), spec(DN_HEADS * DN_CHUNK),
                   pl.BlockSpec((2, None, t // DN_CHUNK, 8, LANES), lambda b, i: (0, b, i, 0, 0))],
        out_shape=[wide(F32, DN_WIDTH), wide(BF16, DN_WIDTH), wide(BF16, DN_WIDTH), wide(BF16, DN_WIDTH),
                   wide(BF16, DN_HEADS * DN_CHUNK),
                   jax.ShapeDtypeStruct((2, bsz, nch, 8, LANES), F32)],
        scratch_shapes=[pltpu.VMEM((t, DN_WIDTH), F32), pltpu.VMEM((t, DN_WIDTH), F32),
                        pltpu.VMEM((t, DN_WIDTH), F32), pltpu.VMEM((t, LANES), F32)],
        compiler_params=_params(("parallel", "parallel")),
        name="deltanet_prep",
    )(qkv_a, qkv_a, qkv_a, gates, cwp, alog, dtb, pre, suf)


def _dn_scan_kernel(u_ref, w_ref, qd_ref, kd_ref, at_ref, egl_ref, o_ref, st_ref, *, nc):
    ch = DN_CHUNK
    dk = DN_HEAD_DIM
    d = pl.program_id(0)

    @pl.when(pl.program_id(2) == 0)
    def _():
        st_ref[...] = jnp.zeros(st_ref.shape, st_ref.dtype)

    n_seq = u_ref.shape[0]
    units = [(sq, h) for sq in range(n_seq) for h in range(DN_HEADS)]
    hsl = [slice(h * dk, (h + 1) * dk) for h in range(DN_HEADS)]
    sts = [st_ref[sq * DN_HEADS + h] for sq, h in units]
    for j in range(nc):
        cc = j + d * (nc - 1 - 2 * j)
        rows = pl.ds(pl.multiple_of(cc * ch, ch), ch)
        egls = [egl_ref[sq, cc] for sq in range(n_seq)]
        sbs = [st.astype(BF16) for st in sts]
        vbs = [(u_ref[sq, rows, hsl[h]] - _dot(w_ref[sq, rows, hsl[h]], sb)).astype(BF16)
               for (sq, h), sb in zip(units, sbs)]
        qss = [_dot(qd_ref[sq, rows, hsl[h]], sb) for (sq, h), sb in zip(units, sbs)]
        sts = [st * egls[sq][h:h + 1, :] + _dot_tn(kd_ref[sq, rows, hsl[h]], vb)
               for (sq, h), st, vb in zip(units, sts, vbs)]
        for (sq, h), qs, vb in zip(units, qss, vbs):
            o_ref[sq, rows, hsl[h]] = qs + _dot(at_ref[sq, rows, h * ch:(h + 1) * ch], vb)
    for (sq, h), st in zip(units, sts):
        st_ref[sq * DN_HEADS + h] = st


def _dn_scan_call(u, w, qd, kd, at, egl, tc):
    _, bsz, s, wd = u.shape
    nb = s // tc
    nc = tc // DN_CHUNK
    n_seq = math.gcd(bsz, 4)

    def blk(d, n):
        return n + d * (nb - 1 - 2 * n)

    spec = lambda width: pl.BlockSpec((None, n_seq, tc, width), lambda d, b, n: (d, b, blk(d, n), 0))
    return pl.pallas_call(
        functools.partial(_dn_scan_kernel, nc=nc),
        grid=(2, bsz // n_seq, nb),
        in_specs=[spec(wd), spec(wd), spec(wd), spec(wd), spec(DN_HEADS * DN_CHUNK),
                  pl.BlockSpec((None, n_seq, nc, 8, LANES), lambda d, b, n: (d, b, blk(d, n), 0, 0))],
        out_specs=spec(wd),
        out_shape=jax.ShapeDtypeStruct((2, bsz, s, wd), F32),
        scratch_shapes=[pltpu.VMEM((n_seq * DN_HEADS, DN_HEAD_DIM, DN_HEAD_DIM), F32)],
        compiler_params=_params(("parallel", "parallel", "arbitrary")),
        name="deltanet_scan",
    )(u, w, qd, kd, at, egl)


def _band_attn_kernel(*refs, n_kv, grp, blk, tq, qs, t_len, with_sink, with_lse):
    q_ref, kp_ref, kc_ref, kn_ref, vp_ref, vc_ref, vn_ref = refs[:7]
    pos = 7
    sink_ref = None
    if with_sink:
        sink_ref = refs[pos]
        pos += 1
    o_ref = refs[pos]
    lse_ref = refs[pos + 1] if with_lse else None
    hd = HEAD_DIM
    i0 = pl.program_id(2) * tq
    kcat = jnp.concatenate([kp_ref[...], kc_ref[...], kn_ref[...]], axis=0)
    vcat = jnp.concatenate([vp_ref[...], vc_ref[...], vn_ref[...]], axis=0)
    kwin = qs + 2 * blk
    n_q = n_kv * grp
    n_sub = tq // qs
    lse_lane = lax.broadcasted_iota(I32, (qs, LSE_COLS), 1)
    ones_k = jnp.ones((kwin, LSE_COLS), BF16)
    lses = [jnp.zeros((qs, LSE_COLS), F32) for _ in range(n_sub)]
    outs = [[] for _ in range(n_sub)]
    biases, khs, vhs = [], [], []
    for sub in range(n_sub):
        k0 = sub * qs
        rowpos = i0 + k0 + lax.broadcasted_iota(I32, (qs, kwin), 0)
        keypos = i0 - blk + k0 + lax.broadcasted_iota(I32, (qs, kwin), 1)
        mask = (jnp.abs(keypos - rowpos) <= blk) & (keypos >= 0) & (keypos < t_len)
        biases.append(jnp.where(mask, 0.0, NEG))
        khs.append([kcat[k0:k0 + kwin, kv * hd:(kv + 1) * hd] for kv in range(n_kv)])
        vhs.append([vcat[k0:k0 + kwin, kv * hd:(kv + 1) * hd] for kv in range(n_kv)])
    units = [(sub, hq) for sub in range(n_sub) for hq in range(n_q)]
    for u0 in range(0, len(units), ATTN_UNITS_PER_STAGE):
        stage = units[u0:u0 + ATTN_UNITS_PER_STAGE]
        scs = [_dot_nt(q_ref[sub * qs:(sub + 1) * qs, hq * hd:(hq + 1) * hd], khs[sub][hq // grp]) + biases[sub]
               for sub, hq in stage]
        ps, ms = [], []
        for (sub, hq), sc in zip(stage, scs):
            m = jnp.max(sc, axis=-1, keepdims=True)
            if with_sink:
                m = jnp.maximum(m, sink_ref[hq])
            ps.append(jnp.exp((sc - m).astype(BF16)))
            ms.append(m)
        dens = [_dot(p, ones_k) for p in ps]
        for (sub, hq), p, m, den in zip(stage, ps, ms, dens):
            if with_sink:
                den = den + jnp.exp(sink_ref[hq] - m)
            if with_lse:
                lses[sub] = jnp.where(lse_lane == hq, m + jnp.log(den), lses[sub])
            outs[sub].append(_dot(p, vhs[sub][hq // grp]) / den[:, :hd])
    for sub in range(n_sub):
        o_ref[sub * qs:(sub + 1) * qs, :] = jnp.concatenate(outs[sub], axis=1).astype(o_ref.dtype)
        if with_lse:
            lse_ref[sub * qs:(sub + 1) * qs, :] = lses[sub]


LSE_COLS = LANES
ATTN_UNITS_PER_STAGE = 8


def _band_attn_call(q, k, v, *, n_kv, grp, blk, sinks, with_lse, name):
    bsz, n_res, t_len, _ = q.shape
    tq = min(512, t_len)
    qs = min(128, tq)
    nt = t_len // tq
    ratio = tq // blk
    nblk = t_len // blk
    qw = n_kv * grp * HEAD_DIM
    kw = n_kv * HEAD_DIM
    cur = lambda wd: pl.BlockSpec((None, None, tq, wd), lambda b, r, i: (b, r, i, 0))
    prev = lambda wd: pl.BlockSpec((None, None, blk, wd), lambda b, r, i: (b, r, jnp.maximum(i * ratio - 1, 0), 0))
    nxt = lambda wd: pl.BlockSpec((None, None, blk, wd),
                                  lambda b, r, i: (b, r, jnp.minimum((i + 1) * ratio, nblk - 1), 0))
    in_specs = [cur(qw), prev(kw), cur(kw), nxt(kw), prev(kw), cur(kw), nxt(kw)]
    args = [q, k, k, k, v, v, v]
    if sinks is not None:
        in_specs.append(pl.BlockSpec(memory_space=pltpu.SMEM))
        args.append(sinks)
    out_shape = [jax.ShapeDtypeStruct((bsz, n_res, t_len, qw), BF16)]
    out_specs = [cur(qw)]
    if with_lse:
        out_shape.append(jax.ShapeDtypeStruct((bsz, n_res, t_len, LSE_COLS), F32))
        out_specs.append(cur(LSE_COLS))
    return pl.pallas_call(
        functools.partial(_band_attn_kernel, n_kv=n_kv, grp=grp, blk=blk, tq=tq, qs=qs, t_len=t_len,
                          with_sink=sinks is not None, with_lse=with_lse),
        grid=(bsz, n_res, nt),
        in_specs=in_specs,
        out_specs=out_specs,
        out_shape=out_shape,
        compiler_params=_params(("parallel", "parallel", "parallel")),
        name=name,
    )(*args)


def _dilated_group(qg, kg, vg, window, dil):
    return _band_attn_call(qg, kg, vg, n_kv=DIL_HEADS_PER_GROUP, grp=1, blk=window // (2 * dil), sinks=None,
                           with_lse=True, name=f"dilated_attn_{dil}")


def _layer_norm(r, g, b):
    mu = jnp.mean(r, -1, keepdims=True)
    var = jnp.mean(jnp.square(r - mu), -1, keepdims=True)
    return (r - mu) * lax.rsqrt(var + LN_EPS) * g + b


def _post_mix(y, x_ref, mod_ref, lng_ref, lnb_ref, rwt_ref, x1_ref, aff_ref):
    m = mod_ref[...]
    x1 = _layer_norm(ALPHA * x_ref[...] + (1.0 + m[2:3]) * y, lng_ref[...], lnb_ref[...])
    x1_ref[...] = x1.reshape(x1_ref.shape)
    h2 = x1 * (1.0 + m[4:5]) + m[3:4]
    split = lambda a: (a.astype(BF16), (a - a.astype(BF16).astype(F32)).astype(BF16))
    (w_hi, w_lo), (h_hi, h_lo) = split(rwt_ref[...]), split(h2)
    logits = _dot_nt(w_hi, h_hi) + (_dot_nt(w_hi, h_lo) + _dot_nt(w_lo, h_hi))
    e = jnp.exp(logits - jnp.max(logits, axis=0, keepdims=True))
    aff_ref[...] = e / jnp.sum(e, axis=0, keepdims=True)


def _position_major(ref, scratch):
    dil, n, wd = ref.shape
    if dil == 1:
        return ref[0].astype(F32)
    for r in range(dil):
        a = ref[r].astype(F32)
        for c in range(wd // LANES):
            scratch[c, pl.ds(r, n, stride=dil), :] = a[:, c * LANES:(c + 1) * LANES]
    return jnp.concatenate([scratch[c] for c in range(wd // LANES)], axis=1)


def _outproj0_kernel(of_ref, ob_ref, z_ref, dnn_ref, og0_ref, og1_ref, og2_ref, l0_ref, l1_ref, l2_ref,
                     w_ref, x_ref, mod_ref, lng_ref, lnb_ref, rwt_ref, x1_ref, aff_ref, regroup_ref):
    dk = DN_HEAD_DIM
    od = of_ref[...] + ob_ref[...]
    z = z_ref[...]
    parts = []
    for h in range(DN_HEADS):
        oh = od[:, h * dk:(h + 1) * dk]
        oh = oh * lax.rsqrt(jnp.mean(oh * oh, -1, keepdims=True) + NORM_EPS) * dnn_ref[...]
        parts.append(oh * _silu(z[:, h * dk:(h + 1) * dk]))
    o_dn = jnp.concatenate(parts, axis=1).astype(BF16)
    l0, l1, l2 = (_position_major(r, regroup_ref) for r in (l0_ref, l1_ref, l2_ref))
    og0, og1, og2 = (_position_major(r, regroup_ref) for r in (og0_ref, og1_ref, og2_ref))
    mx = jnp.maximum(jnp.maximum(l0, l1), l2)
    e0, e1, e2 = jnp.exp(l0 - mx), jnp.exp(l1 - mx), jnp.exp(l2 - mx)
    den = e0 + e1 + e2
    head = lax.broadcasted_iota(I32, (l0.shape[0], DIL_GROUP_WIDTH), 1) >> int(math.log2(HEAD_DIM))

    def per_lane(wt):
        out = jnp.zeros(head.shape, F32)
        for h in range(DIL_HEADS_PER_GROUP):
            out = jnp.where(head == h, wt[:, h:h + 1], out)
        return out

    o_dil = (per_lane(e0 / den) * og0 + per_lane(e1 / den) * og1 + per_lane(e2 / den) * og2).astype(BF16)
    y = _dot(o_dn, w_ref[0:DN_WIDTH, :]) + _dot(o_dil, w_ref[DN_WIDTH:DN_WIDTH + DIL_GROUP_WIDTH, :])
    _post_mix(y, x_ref, mod_ref, lng_ref, lnb_ref, rwt_ref, x1_ref, aff_ref)


def _outproj1_kernel(o_ref, w_ref, x_ref, mod_ref, lng_ref, lnb_ref, rwt_ref, x1_ref, aff_ref):
    y = _dot(o_ref[...], w_ref[...])
    _post_mix(y, x_ref, mod_ref, lng_ref, lnb_ref, rwt_ref, x1_ref, aff_ref)


def _tail_specs(bsz, s, d, tm, n_e):
    row = lambda wd: pl.BlockSpec((None, tm, wd), lambda b, i: (b, i, 0))
    const = lambda shp: pl.BlockSpec(shp, lambda b, i: tuple(0 for _ in shp))
    in_specs = [row(d), pl.BlockSpec((None, 6, d), lambda b, i: (b, 0, 0)), const((1, d)), const((1, d)),
                const((n_e, d))]
    out_specs = [pl.BlockSpec((None, tm, d // LANES, LANES), lambda b, i: (b, i, 0, 0)),
                 pl.BlockSpec((None, n_e, tm), lambda b, i: (b, 0, i))]
    out_shape = [jax.ShapeDtypeStruct((bsz, s, d // LANES, LANES), F32), jax.ShapeDtypeStruct((bsz, n_e, s), F32)]
    return in_specs, out_specs, out_shape


def _outproj0_call(o_scan, z, dn_norm, ogs, lses, w_out, x, mod, ln_g, ln_b, router_w, tm):
    bsz, s, d = x.shape
    n_e = router_w.shape[1]
    tail_in, out_specs, out_shape = _tail_specs(bsz, s, d, tm, n_e)
    row = lambda wd: pl.BlockSpec((None, tm, wd), lambda b, i: (b, i, 0))
    dirspec = lambda dd: pl.BlockSpec((None, None, tm, DN_WIDTH), lambda b, i: (dd, b, i, 0))
    gw = DIL_GROUP_WIDTH
    resid = lambda a: pl.BlockSpec((None, a.shape[1], tm // a.shape[1], a.shape[3]), lambda b, i: (b, 0, i, 0))
    in_specs = [dirspec(0), dirspec(1), row(DN_WIDTH), pl.BlockSpec((1, DN_HEAD_DIM), lambda b, i: (0, 0))]
    in_specs += [resid(a) for a in ogs] + [resid(a) for a in lses]
    in_specs += [pl.BlockSpec(w_out.shape, lambda b, i: (0, 0))] + tail_in
    return pl.pallas_call(
        _outproj0_kernel, grid=(bsz, s // tm), in_specs=in_specs, out_specs=out_specs, out_shape=out_shape,
        scratch_shapes=[pltpu.VMEM((gw // LANES, tm, LANES), F32)],
        compiler_params=_params(("parallel", "parallel")), name="outproj_deltanet_dilated",
    )(o_scan, o_scan, z, dn_norm.reshape(1, -1), *ogs, *lses, w_out, x, mod, ln_g.reshape(1, d), ln_b.reshape(1, d),
      router_w.T)


def _outproj1_call(o, w_out, x, mod, ln_g, ln_b, router_w, tm):
    bsz, s, d = x.shape
    n_e = router_w.shape[1]
    tail_in, out_specs, out_shape = _tail_specs(bsz, s, d, tm, n_e)
    in_specs = [pl.BlockSpec((None, tm, o.shape[-1]), lambda b, i: (b, i, 0)),
                pl.BlockSpec(w_out.shape, lambda b, i: (0, 0))] + tail_in
    return pl.pallas_call(
        _outproj1_kernel, grid=(bsz, s // tm), in_specs=in_specs, out_specs=out_specs, out_shape=out_shape,
        compiler_params=_params(("parallel", "parallel")), name="outproj_swa",
    )(o, w_out, x, mod, ln_g.reshape(1, d), ln_b.reshape(1, d), router_w.T)


def _topk_kernel(a_ref, idx_ref, *, cap, jb):
    a = a_ref[...]
    n_e, rows, _ = a.shape
    bits = pltpu.bitcast(a, I32)
    thr = jnp.zeros((n_e, 1, 1), I32)
    for bit in range(30, -1, -1):
        cand = thr | (1 << bit)
        cnt = jnp.sum(jnp.where(bits >= cand, 1, 0), axis=(1, 2), keepdims=True)
        thr = jnp.where(cnt >= cap, cand, thr)
    gt = jnp.where(bits > thr, 1.0, 0.0).astype(F32)
    eq = jnp.where(bits == thr, 1.0, 0.0).astype(F32)
    need = cap - jnp.sum(gt, axis=(1, 2), keepdims=True)
    ru = lax.broadcasted_iota(I32, (LANES, LANES), 0)
    cu = lax.broadcasted_iota(I32, (LANES, LANES), 1)
    upper = jnp.where(ru <= cu, 1.0, 0.0).astype(BF16)
    rl = lax.broadcasted_iota(I32, (rows, rows), 0)
    cl = lax.broadcasted_iota(I32, (rows, rows), 1)
    lstrict = jnp.where(cl < rl, 1.0, 0.0).astype(BF16)

    def fold_cumsum(xs):
        withins = [_dot(x.astype(BF16), upper) for x in xs]
        rowtots = [jnp.broadcast_to(w[:, LANES - 1:LANES], w.shape) for w in withins]
        befores = [_dot(lstrict, r.astype(BF16)) for r in rowtots]
        return [w + b for w, b in zip(withins, befores)], [b + r for b, r in zip(befores, rowtots)]

    experts = range(n_e)
    eq_counts, _ = fold_cumsum([eq[e] for e in experts])
    sels = [jnp.maximum(gt[e], jnp.where(eq_counts[e] - eq[e] < need[e], eq[e], 0.0)) for e in experts]
    counts, count_ends = fold_cumsum(sels)
    rowid = lax.broadcasted_iota(I32, (rows, jb), 0).astype(F32)
    for j0 in range(0, cap, jb):
        slot = (j0 + lax.broadcasted_iota(I32, (1, jb), 1)).astype(F32)
        row = [jnp.sum(jnp.where(count_ends[e][:, 0:1] <= slot, 1.0, 0.0), axis=0, keepdims=True) for e in experts]
        onehot = [jnp.where(rowid == row[e], 1.0, 0.0) for e in experts]
        count_row = [_dot_tn(counts[e], onehot[e], HI) for e in experts]
        for e in experts:
            lane = jnp.sum(jnp.where(count_row[e] <= slot, 1.0, 0.0), axis=0, keepdims=True)
            idx_ref[e:e + 1, j0:j0 + jb] = (row[e] * LANES + lane).astype(I32)


def _topk_call(aff, cap):
    bsz, n_e, s = aff.shape
    rows = s // LANES
    return pl.pallas_call(
        functools.partial(_topk_kernel, cap=cap, jb=min(512, cap)),
        grid=(bsz,),
        in_specs=[pl.BlockSpec((None, n_e, rows, LANES), lambda b: (b, 0, 0, 0))],
        out_specs=pl.BlockSpec((None, n_e, cap), lambda b: (b, 0, 0)),
        out_shape=jax.ShapeDtypeStruct((bsz, n_e, cap), I32),
        compiler_params=_params(("parallel",)), name="topk_route",
    )(aff.reshape(bsz, n_e, rows, LANES))


SUBLANES = 8


def _moe_kernel(idx_hbm, aff_hbm, x_hbm, mod_ref, wg_ref, wu_ref, wd_ref, out_hbm,
                idx_s, aff_s, xg0, xg1, hid_s, y, acc, sem_i, sem_a, sem_g, sem_o, *, cap, n_f, y_bounds):
    b = pl.program_id(0)
    e = pl.program_id(1)
    f = pl.program_id(2)
    n_b = pl.num_programs(0)
    n_e = pl.num_programs(1)
    per_f = cap // n_f
    table = idx_s.shape[0] // 2
    cur_table = (b % 2) * table
    more = b + 1 < n_b

    def row_copy(src_b, tok, j, dst, sem):
        return pltpu.make_async_copy(x_hbm.at[src_b, tok], dst.at[j], sem)

    def wait_rows(dst, sem):
        pltpu.make_async_copy(x_hbm.at[b, pl.ds(0, cap)], dst, sem).wait()

    def table_copy(src_b):
        return pltpu.make_async_copy(idx_hbm.at[src_b], idx_s.at[pl.ds((src_b % 2) * table, table)], sem_i)

    @pl.when((e == 0) & (f == 0))
    def _():
        acc[...] = jnp.zeros(acc.shape, acc.dtype)

        @pl.when(b == 0)
        def _():
            table_copy(b).start()
            table_copy(b).wait()

            def gather(jo, carry):
                for r in range(SUBLANES):
                    j = jo * SUBLANES + r
                    row_copy(b, idx_s[j], j, xg0, sem_g.at[0]).start(priority=r % 2)
                return carry

            lax.fori_loop(0, cap // SUBLANES, gather, 0)
            wait_rows(xg0, sem_g.at[0])

    @pl.when((f == 0) & more & (e == n_e - 2))
    def _():
        table_copy(b + 1).start()

    @pl.when((f == 0) & more & (e == n_e - 1))
    def _():
        table_copy(b + 1).wait()

    aff_copy = pltpu.make_async_copy(aff_hbm.at[b, e], aff_s, sem_a)

    @pl.when(f == 0)
    def _():
        aff_copy.start()

    def expert_step(xg_cur, xg_next, sem_next):
        last = e == n_e - 1
        src_b = jnp.where(last & more, b + 1, b)
        next_base = jnp.where(last, jnp.where(more, ((b + 1) % 2) * table, cur_table + e * cap),
                              cur_table + (e + 1) * cap) + f * per_f
        for r in range(per_f):
            row_copy(src_b, idx_s[next_base + r], f * per_f + r, xg_next, sem_next).start(priority=r % 2)
        m = mod_ref[...]
        xv = (xg_cur[...].reshape(cap, m.shape[-1]) * (1.0 + m[4:5]) + m[3:4]).astype(BF16)
        hid_s[f] = (_silu(_dot(xv, wg_ref[...])) * _dot(xv, wu_ref[...])).astype(BF16)

        @pl.when(f == n_f - 1)
        def _():
            aff_copy.wait()

            def scatter_rows(j0):
                yc = y[pl.ds(j0, SUBLANES), :].reshape((SUBLANES,) + acc.shape[1:])
                toks = [idx_s[cur_table + e * cap + j0 + r] for r in range(SUBLANES)]
                olds = [acc[toks[r]] for r in range(SUBLANES)]
                for r in range(SUBLANES):
                    acc[toks[r]] = olds[r] + aff_s[toks[r]] * yc[r]

            for p, (r0, r1) in enumerate(zip(y_bounds[:-1], y_bounds[1:])):
                hid = jnp.concatenate([hid_s[i, r0:r1, :] for i in range(n_f)], axis=1)
                y[r0:r1, :] = _dot(hid, wd_ref[...])
                if p > 0:
                    for j0 in range(y_bounds[p - 1], r0, SUBLANES):
                        scatter_rows(j0)

            def scatter_last(jo, carry):
                scatter_rows(pl.multiple_of(y_bounds[-2] + jo * SUBLANES, SUBLANES))
                return carry

            lax.fori_loop(0, (y_bounds[-1] - y_bounds[-2]) // SUBLANES, scatter_last, 0)
            wait_rows(xg_next, sem_next)

            @pl.when(e == n_e - 1)
            def _():
                co = pltpu.make_async_copy(acc, out_hbm.at[b], sem_o)
                co.start()
                co.wait()

    @pl.when(e % 2 == 0)
    def _():
        expert_step(xg0, xg1, sem_g.at[1])

    @pl.when(e % 2 == 1)
    def _():
        expert_step(xg1, xg0, sem_g.at[0])


def _moe_call(x1, mod, idx, aff, wg, wu, wd, fcw):
    bsz, s, nt, _ = x1.shape
    d = nt * LANES
    n_e, _, ff = wg.shape
    cap = idx.shape[-1]
    fcw = min(fcw, ff)
    n_f = ff // fcw
    y_bounds = (0, cap // 2, cap // 2 + 3 * cap // 8, cap)
    assert n_e % 2 == 0 and cap % n_f == 0 and cap % (8 * SUBLANES) == 0 and d // LANES == SUBLANES
    anyspec = pl.BlockSpec(memory_space=pl.ANY)
    return pl.pallas_call(
        functools.partial(_moe_kernel, cap=cap, n_f=n_f, y_bounds=y_bounds),
        grid=(bsz, n_e, n_f),
        in_specs=[anyspec, anyspec, anyspec,
                  pl.BlockSpec((None, 6, d), lambda b, e, f: (b, 0, 0)),
                  pl.BlockSpec((None, d, fcw), lambda b, e, f: (e, 0, f)),
                  pl.BlockSpec((None, d, fcw), lambda b, e, f: (e, 0, f)),
                  pl.BlockSpec((None, ff, d), lambda b, e, f: (e, 0, 0))],
        out_specs=anyspec,
        out_shape=jax.ShapeDtypeStruct((bsz, s, d // LANES, LANES), F32),
        scratch_shapes=[pltpu.SMEM((2 * n_e * cap,), I32), pltpu.SMEM((s,), F32),
                        pltpu.VMEM((cap, nt, LANES), F32), pltpu.VMEM((cap, nt, LANES), F32),
                        pltpu.VMEM((n_f, cap, fcw), BF16), pltpu.VMEM((cap, d), F32),
                        pltpu.VMEM((s, nt, LANES), F32),
                        pltpu.SemaphoreType.DMA(()), pltpu.SemaphoreType.DMA(()),
                        pltpu.SemaphoreType.DMA((2,)), pltpu.SemaphoreType.DMA(())],
        compiler_params=_params(("arbitrary", "arbitrary", "arbitrary"), MOE_VMEM_LIMIT),
        name="moe_experts",
    )(idx.reshape(bsz, n_e * cap), aff, x1, mod, wg, wu, wd)


def _ln2_kernel(x_ref, y_ref, g2_ref, g_ref, b_ref, o_ref):
    r = ALPHA * x_ref[...] + (1.0 + g2_ref[...]) * y_ref[...]
    o_ref[...] = _layer_norm(r.reshape(o_ref.shape), g_ref[...], b_ref[...])


def _ln2_call(x1, moe, mod, g, b, tm):
    bsz, s, nt, _ = x1.shape
    d = nt * LANES
    row = pl.BlockSpec((None, tm, d), lambda bb, i: (bb, i, 0))
    tiles = pl.BlockSpec((None, tm, nt, LANES), lambda bb, i: (bb, i, 0, 0))
    vec = pl.BlockSpec((1, d), lambda bb, i: (0, 0))
    return pl.pallas_call(
        _ln2_kernel, grid=(bsz, s // tm),
        in_specs=[tiles, tiles, pl.BlockSpec((None, 1, nt, LANES), lambda bb, i: (bb, 0, 0, 0)), vec, vec],
        out_specs=row, out_shape=jax.ShapeDtypeStruct((bsz, s, d), F32),
        compiler_params=_params(("parallel", "parallel")), name="ffn_postnorm",
    )(x1, moe, mod[:, 5].reshape(bsz, 1, nt, LANES), g.reshape(1, d), b.reshape(1, d))


def _ffn_block(x1, aff, mod, wg, wu, wd, ln_g, ln_b, tm):
    s = x1.shape[1]
    cap = (EC_FACTOR * s) // N_EXPERTS
    idx = _topk_call(aff, cap)
    moe = _moe_call(x1, mod, idx, aff, wg.astype(BF16), wu.astype(BF16), wd.astype(BF16), fcw=512)
    return _ln2_call(x1, moe, mod, ln_g, ln_b, tm)


def kernel(x, c, positions, ada_w, ada_b, ab_w_in, ab_conv_w, ab_a_log, ab_dt_bias, ab_dn_norm, ab_w_out, swa_w_in,
           swa_sinks, swa_w_out, ln_mix_g, ln_mix_b, router_w, moe_w_gate, moe_w_up, moe_w_down, ln_ffn_g, ln_ffn_b):
    bsz, s, d = x.shape
    tm = min(512, s)
    mod = _mod_call(c, ada_w, ada_b).reshape(DEPTH, bsz, 6, d)
    rope = _rope_tables(positions)

    w_in = ab_w_in[0]
    n_a = 4 * DN_WIDTH
    n_g = 4 * DN_HEADS
    gw = DIL_GROUP_WIDTH
    q_scale = HEAD_DIM ** -0.5
    cols = [w_in[:, :n_a], jnp.pad(w_in[:, n_a:n_a + n_g], ((0, 0), (0, LANES - n_g)))]
    groups = [(0, 3 * DN_WIDTH, 0, False), (3 * DN_WIDTH, DN_WIDTH, 1, False), (n_a, LANES, 2, False)]
    for gi in range(len(DIL_PAIRS)):
        for part in range(3):
            c0 = n_a + n_g + part * DIL_WIDTH + gi * gw
            cols.append(w_in[:, c0:c0 + gw] * (q_scale if part == 0 else 1.0))
            groups.append((n_a + LANES + (3 * gi + part) * gw, gw, 3 + 3 * gi + part, part < 2))
    w0 = jnp.concatenate(cols, axis=1).astype(BF16)
    dils = tuple(dil for _, dil in DIL_PAIRS for _ in range(3))
    outs0 = _inproj_call(
        x, mod[0], w0, rope, _chunk_plan(groups), (3 * DN_WIDTH, DN_WIDTH, LANES) + (gw,) * 9,
        (F32, F32, F32) + (BF16,) * 9, (1, 1, 1) + dils, tm, "inproj_deltanet_dilated")
    qkv_a, z, gates = outs0[:3]
    u, w, qd, kd, at, egl = _dn_prep_call(qkv_a, gates, ab_conv_w[0], ab_a_log[0], ab_dt_bias[0], tm)
    o_scan = _dn_scan_call(u, w, qd, kd, at, egl, min(256, s))
    ogs, lses = [], []
    for gi, (window, dil) in enumerate(DIL_PAIRS):
        qkv_g = [a.reshape(bsz, dil, s // dil, gw) for a in outs0[3 + 3 * gi:6 + 3 * gi]]
        o_g, lse_g = _dilated_group(*qkv_g, window, dil)
        ogs.append(o_g)
        lses.append(lse_g)
    x1, aff = _outproj0_call(o_scan, z, ab_dn_norm[0], ogs, lses, ab_w_out[0].astype(BF16), x, mod[0],
                             ln_mix_g[0], ln_mix_b[0], router_w[0], tm)
    x = _ffn_block(x1, aff, mod[0], moe_w_gate[0], moe_w_up[0], moe_w_down[0], ln_ffn_g[0], ln_ffn_b[0], tm)

    qw = SWA_Q_HEADS * HEAD_DIM
    kw = SWA_KV_HEADS * HEAD_DIM
    plan1 = _chunk_plan([(0, qw, 0, True), (qw, kw, 1, True), (qw + kw, kw, 2, False)])
    w1 = jnp.concatenate([swa_w_in[0][:, :qw] * q_scale, swa_w_in[0][:, qw:]], axis=1).astype(BF16)
    q1, k1, v1 = _inproj_call(x, mod[1], w1, rope, plan1, (qw, kw, kw), (BF16, BF16, BF16), (1, 1, 1), tm,
                              "inproj_swa")
    (o1,) = _band_attn_call(q1[:, None], k1[:, None], v1[:, None], n_kv=SWA_KV_HEADS,
                            grp=SWA_Q_HEADS // SWA_KV_HEADS, blk=SWA_WINDOW, sinks=swa_sinks[0], with_lse=False,
                            name="swa_attn")
    x1, aff = _outproj1_call(o1[:, 0], swa_w_out[0].astype(BF16), x, mod[1], ln_mix_g[1], ln_mix_b[1], router_w[1],
                             tm)
    x = _ffn_block(x1, aff, mod[1], moe_w_gate[1], moe_w_up[1], moe_w_down[1], ln_ffn_g[1], ln_ffn_b[1], tm)
    return x
```
